```python
import math
import jax, jax.numpy as jnp
from jax import lax
import numpy as np

D_MODEL = 1024
BATCH = 4
SEQ = 8192
DEPTH = 1
DEC_BATCH = 128
DEC_SEQ = 4
PAST_LEN = 8192
PAGE_SIZE = 128

D_CONV = D_MODEL
CONV_WIDTH = 3
HEAD_DIM = 64
HEADS_PER_GROUP = 8
DILATED_GROUPS = ((128, 1), (512, 4), (2048, 16))
N_GROUPS = 3
D_ATT = HEADS_PER_GROUP * HEAD_DIM
D_QKV = N_GROUPS * D_ATT
D_FF = 4 * D_MODEL
N_BUCKETS = 32
MAX_DISTANCE = 2048
BLOCK = 128
N_MOD = 6
RMS_EPS = 1e-6
ATT_SCALE = HEAD_DIM ** -0.5
PROJ_SIZES = (D_CONV, D_CONV, D_CONV, D_QKV, D_QKV, D_QKV, D_MODEL, D_MODEL)
D_PROJ = 3 * D_CONV + 3 * D_QKV + 2 * D_MODEL

kernel_name = "gated_conv_dilated_swa_hybrid_step"


def rms_norm(x, g):
    xf = x.astype(jnp.float32)
    y = xf * lax.rsqrt(jnp.mean(xf * xf, axis=-1, keepdims=True) + RMS_EPS)
    return (y * g.astype(jnp.float32)).astype(x.dtype)


def t5_causal_bucket(dist):
    max_exact = N_BUCKETS // 2
    ratio = jnp.maximum(dist, 1).astype(jnp.float32) / max_exact
    large = max_exact + (jnp.log(ratio) / math.log(MAX_DISTANCE / max_exact)
                         * (N_BUCKETS - max_exact)).astype(jnp.int32)
    large = jnp.minimum(large, N_BUCKETS - 1)
    return jnp.where(dist < max_exact, dist, large)


def group_bias(rel_bias, g):
    window, dil = DILATED_GROUPS[g]
    steps = jnp.arange(window // dil + 1)
    b = rel_bias[t5_causal_bucket(steps * dil), g * HEADS_PER_GROUP:(g + 1) * HEADS_PER_GROUP]
    return b.T.astype(jnp.float32)


def softmax_stats(logits, valid):
    logits = jnp.where(valid, logits, -jnp.inf)
    m = jnp.max(logits, axis=-1, keepdims=True)
    p = jnp.exp(logits - m)
    s = jnp.sum(p, axis=-1, keepdims=True)
    return p, s, m + jnp.log(s)


def dilated_attn_prompt(q, k, v, bias, dil):
    nb, ns, nh, ne = q.shape
    wk = bias.shape[-1] - 1
    span = dil * BLOCK
    s_pad = -(-ns // span) * span
    nblk = s_pad // span

    def to_blocks(t):
        t = jnp.pad(t, ((0, 0), (0, s_pad - ns), (0, 0), (0, 0)))
        t = t.reshape(nb, nblk, BLOCK, dil, nh, ne)
        return t.transpose(0, 3, 1, 2, 4, 5)

    def with_prev(t):
        prev = jnp.pad(t[:, :, :-1], ((0, 0), (0, 0), (1, 0), (0, 0), (0, 0), (0, 0)))
        return jnp.concatenate([prev, t], axis=3)

    qb = to_blocks(q)
    kk = with_prev(to_blocks(k))
    vv = with_prev(to_blocks(v))
    i = jnp.arange(BLOCK)[:, None]
    j = jnp.arange(2 * BLOCK)[None, :]
    off = i + BLOCK - j
    band = (off >= 0) & (off <= wk)
    first = (jnp.arange(nblk)[:, None, None] > 0) | (j >= BLOCK)
    valid = (band[None] & first)[:, None]
    b = bias[:, jnp.clip(off, 0, wk)]
    logits = jnp.einsum('brnihe,brnjhe->brnhij', qb, kk,
                        preferred_element_type=jnp.float32) * ATT_SCALE + b
    p, s, lse = softmax_stats(logits, valid)
    o = jnp.einsum('brnhij,brnjhe->brnihe', p, vv.astype(jnp.float32)) / jnp.swapaxes(s, 3, 4)
    o = o.transpose(0, 2, 3, 1, 4, 5).reshape(nb, s_pad, nh, ne)[:, :ns]
    lse = jnp.swapaxes(lse[..., 0], 3, 4).transpose(0, 2, 3, 1, 4).reshape(nb, s_pad, nh)[:, :ns]
    return o, lse


def dilated_attn_decode(q, k, v, cache_k, cache_v, bias, window, dil):
    nt = q.shape[1]
    wb = cache_k.shape[1]
    wk = bias.shape[-1] - 1
    k_all = jnp.concatenate([cache_k, k], axis=1)
    v_all = jnp.concatenate([cache_v, v], axis=1)
    idx = wb + jnp.arange(nt)[:, None] - dil * jnp.arange(wk + 1)[None, :]
    valid = idx >= 0
    idc = jnp.clip(idx, 0, None)
    kg = k_all[:, idc]
    vg = v_all[:, idc]
    logits = jnp.einsum('bthe,btkhe->bhtk', q, kg,
                        preferred_element_type=jnp.float32) * ATT_SCALE + bias[:, None, :]
    p, s, lse = softmax_stats(logits, valid)
    o = jnp.einsum('bhtk,btkhe->bthe', p, vg.astype(jnp.float32)) / jnp.swapaxes(s, 1, 2)
    lse = jnp.swapaxes(lse[..., 0], 1, 2)
    keep = min(window, wb + nt)
    return o, lse, k_all[:, -keep:], v_all[:, -keep:]


def hybrid_layer(x, c, conv_state, kv_cache, rel_bias, norm1_g, norm2_g, w_ada, b_ada,
                 w_in, conv_w, q_norm_g, k_norm_g, w_conv_out, w_attn_out, w_o,
                 w_mlp_in, w_mlp_out):
    nb, ns, _ = x.shape
    decode = conv_state is not None
    mod = jnp.dot(jax.nn.silu(c), w_ada) + b_ada
    shift1, scale1, gate1, shift2, scale2, gate2 = jnp.split(mod[:, None, :], N_MOD, axis=-1)
    xn = rms_norm(x, norm1_g) * (1 + scale1) + shift1
    split_at = np.cumsum(PROJ_SIZES)[:-1].tolist()
    h, b_gate, c_gate, q, k, v, g_conv, g_att = jnp.split(jnp.dot(xn, w_in), split_at, axis=-1)

    u = c_gate * h
    if decode:
        u_ext = jnp.concatenate([conv_state, u], axis=1)
    else:
        u_ext = jnp.pad(u, ((0, 0), (CONV_WIDTH - 1, 0), (0, 0)))
    y_conv = conv_w[0] * u_ext[:, 0:ns]
    for tap in range(1, CONV_WIDTH):
        y_conv = y_conv + conv_w[tap] * u_ext[:, tap:tap + ns]
    new_conv = u_ext[:, -(CONV_WIDTH - 1):]
    conv_out = jnp.dot(b_gate * y_conv, w_conv_out)

    shp = (nb, ns, N_GROUPS, HEADS_PER_GROUP, HEAD_DIM)
    q = rms_norm(q.reshape(shp), q_norm_g)
    k = rms_norm(k.reshape(shp), k_norm_g)
    v = v.reshape(shp)
    outs, lses, new_kv = [], [], []
    for g, (window, dil) in enumerate(DILATED_GROUPS):
        bias = group_bias(rel_bias, g)
        if decode:
            o, lse, kb, vb = dilated_attn_decode(q[:, :, g], k[:, :, g], v[:, :, g],
                                                 kv_cache[2 * g], kv_cache[2 * g + 1],
                                                 bias, window, dil)
        else:
            o, lse = dilated_attn_prompt(q[:, :, g], k[:, :, g], v[:, :, g], bias, dil)
            keep = min(window, ns)
            kb, vb = k[:, -keep:, g], v[:, -keep:, g]
        outs.append(o)
        lses.append(lse)
        new_kv.extend([kb, vb])
    wts = jax.nn.softmax(jnp.stack(lses), axis=0)
    o = jnp.sum(wts[..., None] * jnp.stack(outs), axis=0)
    attn_out = jnp.dot(o.reshape(nb, ns, D_ATT).astype(x.dtype), w_attn_out)

    mixed = jax.nn.sigmoid(g_conv) * conv_out + jax.nn.sigmoid(g_att) * attn_out
    x = x + gate1 * jnp.dot(mixed, w_o)
    xn2 = rms_norm(x, norm2_g) * (1 + scale2) + shift2
    hid = jax.nn.relu(jnp.dot(xn2, w_mlp_in))
    x = x + gate2 * jnp.dot(hid * hid, w_mlp_out)
    return x, new_conv, new_kv


def setup_inputs(seed: int = 0) -> dict:
    key = jax.random.key(seed)
    ks = jax.random.split(key, 32)
    f32 = jnp.float32

    def nrm(k, shape, scale=1.0):
        return jax.random.normal(k, shape, f32) * scale

    wb = [min(w, PAST_LEN) for (w, _) in DILATED_GROUPS]
    kv = lambda kk, n: nrm(kk, (DEPTH, DEC_BATCH, n, HEADS_PER_GROUP, HEAD_DIM))
    return {
        "x_prompt": nrm(ks[0], (BATCH, SEQ, D_MODEL)),
        "x_sample": nrm(ks[1], (DEC_BATCH, DEC_SEQ, D_MODEL)),
        "c_prompt": nrm(ks[2], (BATCH, D_MODEL)),
        "c_sample": nrm(ks[3], (DEC_BATCH, D_MODEL)),
        "state_conv": nrm(ks[4], (DEPTH, DEC_BATCH, CONV_WIDTH - 1, D_CONV)),
        "cache_k1": kv(ks[5], wb[0]),
        "cache_v1": kv(ks[6], wb[0]),
        "cache_k2": kv(ks[7], wb[1]),
        "cache_v2": kv(ks[8], wb[1]),
        "cache_k3": kv(ks[9], wb[2]),
        "cache_v3": kv(ks[10], wb[2]),
        "rel_bias": nrm(ks[11], (N_BUCKETS, N_GROUPS * HEADS_PER_GROUP), 0.5),
        "norm1_g": 1.0 + nrm(ks[12], (DEPTH, D_MODEL), 0.02),
        "norm2_g": 1.0 + nrm(ks[13], (DEPTH, D_MODEL), 0.02),
        "w_ada": nrm(ks[14], (DEPTH, D_MODEL, N_MOD * D_MODEL), 0.5 * D_MODEL ** -0.5),
        "b_ada": nrm(ks[15], (DEPTH, N_MOD * D_MODEL), 0.02),
        "w_in": nrm(ks[16], (DEPTH, D_MODEL, D_PROJ), D_MODEL ** -0.5),
        "conv_w": nrm(ks[17], (DEPTH, CONV_WIDTH, D_CONV), CONV_WIDTH ** -0.5),
        "q_norm_g": 1.0 + nrm(ks[18], (DEPTH, HEAD_DIM), 0.02),
        "k_norm_g": 1.0 + nrm(ks[19], (DEPTH, HEAD_DIM), 0.02),
        "w_conv_out": nrm(ks[20], (DEPTH, D_CONV, D_MODEL), D_CONV ** -0.5),
        "w_attn_out": nrm(ks[21], (DEPTH, D_ATT, D_MODEL), D_ATT ** -0.5),
        "w_o": nrm(ks[22], (DEPTH, D_MODEL, D_MODEL), D_MODEL ** -0.5),
        "w_mlp_in": nrm(ks[23], (DEPTH, D_MODEL, D_FF), D_MODEL ** -0.5),
        "w_mlp_out": nrm(ks[24], (DEPTH, D_FF, D_MODEL), D_FF ** -0.5),
    }


def reference(x_prompt, x_sample, c_prompt, c_sample, state_conv, cache_k1, cache_v1,
              cache_k2, cache_v2, cache_k3, cache_v3, rel_bias, norm1_g, norm2_g, w_ada,
              b_ada, w_in, conv_w, q_norm_g, k_norm_g, w_conv_out, w_attn_out, w_o,
              w_mlp_in, w_mlp_out):
    yp, ys = x_prompt, x_sample
    p_states = [[] for _ in range(1 + 2 * N_GROUPS)]
    s_states = [[] for _ in range(1 + 2 * N_GROUPS)]
    caches = (cache_k1, cache_v1, cache_k2, cache_v2, cache_k3, cache_v3)
    for l in range(DEPTH):
        lw = (norm1_g[l], norm2_g[l], w_ada[l], b_ada[l], w_in[l], conv_w[l], q_norm_g[l],
              k_norm_g[l], w_conv_out[l], w_attn_out[l], w_o[l], w_mlp_in[l], w_mlp_out[l])
        yp, pc, pkv = hybrid_layer(yp, c_prompt, None, None, rel_bias, *lw)
        ys, sc, skv = hybrid_layer(ys, c_sample, state_conv[l],
                                   [cc[l] for cc in caches], rel_bias, *lw)
        for lst, a in zip(p_states, [pc] + pkv):
            lst.append(a)
        for lst, a in zip(s_states, [sc] + skv):
            lst.append(a)
    p_conv, p_k1, p_v1, p_k2, p_v2, p_k3, p_v3 = [jnp.stack(a) for a in p_states]
    s_conv, s_k1, s_v1, s_k2, s_v2, s_k3, s_v3 = [jnp.stack(a) for a in s_states]
    return (yp, ys, p_conv, p_k1, p_v1, p_k2, p_v2, p_k3, p_v3,
            s_conv, s_k1, s_v1, s_k2, s_v2, s_k3, s_v3)
```

```python
import functools
import math

import numpy as np
import jax
import jax.numpy as jnp
from jax import lax
from jax.experimental import pallas as pl
from jax.experimental.pallas import tpu as pltpu

F32 = jnp.float32
BF16 = jnp.bfloat16

D_MODEL = 1024
D_CONV = D_MODEL
CONV_WIDTH = 3
HEAD_DIM = 64
N_HEADS = 8
D_ATT = N_HEADS * HEAD_DIM
DILATED_GROUPS = ((128, 1), (512, 4), (2048, 16))
N_GROUPS = len(DILATED_GROUPS)
D_QKV = N_GROUPS * D_ATT
D_FF = 4 * D_MODEL
N_BUCKETS = 32
MAX_DISTANCE = 2048
ATT_BLOCK = 128
WK = 128
N_MOD = 6
RMS_EPS = 1e-6
ATT_SCALE = HEAD_DIM ** -0.5
NEG = -1e30

OFF_H, OFF_B, OFF_C = 0, D_CONV, 2 * D_CONV
OFF_Q = 3 * D_CONV
OFF_K = OFF_Q + D_QKV
OFF_V = OFF_K + D_QKV
OFF_GC = OFF_V + D_QKV
OFF_GA = OFF_GC + D_MODEL
D_PROJ = OFF_GA + D_MODEL

LANES = 128
SUBLANES = 8
MXU_DIM = 256
VMEM_LIMIT_BYTES = 56 * 1024 * 1024

TM_IN = 256
TM_OUT = 256
ATT_NB = 4
FF_CHUNK = 512
DEC_PACK = LANES

assert all(w // d == WK for w, d in DILATED_GROUPS)


def _mm(a, b):
    return jnp.dot(a, b, preferred_element_type=F32)


def _const_spec(shape):
    nd = len(shape)
    return pl.BlockSpec(shape, lambda *_: (0,) * nd, pipeline_mode=pl.Buffered(1))


def _ada_kernel(c_ref, w_ref, b_ref, o_ref):
    c = c_ref[...]
    s = c * jax.nn.sigmoid(c)
    o_ref[...] = _mm(s.astype(BF16), w_ref[...].astype(BF16)) + b_ref[...]


def _ada(c_all, w_ada, b_ada):
    n, d = c_all.shape
    nout = w_ada.shape[1]
    tn = 1024
    return pl.pallas_call(
        _ada_kernel,
        grid=(nout // tn,),
        in_specs=[pl.BlockSpec((n, d), lambda j: (0, 0)),
                  pl.BlockSpec((d, tn), lambda j: (0, j)),
                  pl.BlockSpec((1, tn), lambda j: (0, j))],
        out_specs=pl.BlockSpec((n, tn), lambda j: (0, j)),
        out_shape=jax.ShapeDtypeStruct((n, nout), F32),
        compiler_params=pltpu.CompilerParams(dimension_semantics=("arbitrary",),
                                             vmem_limit_bytes=VMEM_LIMIT_BYTES),
        name="ada",
    )(c_all, w_ada, b_ada.reshape(1, nout))


def _in_kernel(*refs, decode, tm, dec_seq):
    if decode:
        (x_ref, sh_ref, sc_ref, g1_ref, win_ref, cw_ref, wco_ref, qg_ref, kg_ref, s1_ref, s2_ref,
         cm_ref, sga_ref, qkv_ref, u_ref, uext) = refs
    else:
        (x_ref, sh_ref, sc_ref, g1_ref, win_ref, cw_ref, wco_ref, qg_ref, kg_ref,
         cm_ref, sga_ref, qkv_ref, utail_ref, uext) = refs

    x = x_ref[0]
    ms = jnp.mean(x * x, axis=-1, keepdims=True)
    xn = x * lax.rsqrt(ms + RMS_EPS) * g1_ref[...]
    xn = (xn * (1.0 + sc_ref[0]) + sh_ref[0]).astype(BF16)

    h = _mm(xn, win_ref[:, OFF_H:OFF_H + D_CONV])
    c = _mm(xn, win_ref[:, OFF_C:OFF_C + D_CONV])
    u = c * h
    hist = SUBLANES
    if decode:
        uext[0:hist, :] = jnp.zeros((hist, D_CONV), F32)
    else:
        @pl.when(pl.program_id(1) == 0)
        def _():
            uext[0:hist, :] = jnp.zeros((hist, D_CONV), F32)
    uext[hist:hist + tm, :] = u
    um1 = uext[hist - 1:hist - 1 + tm, :]
    um2 = uext[hist - 2:hist - 2 + tm, :]
    if decode:
        t = lax.broadcasted_iota(jnp.int32, (tm, 1), 0) % dec_seq
        um1 = jnp.where(t >= 1, um1, s1_ref[0])
        um2 = jnp.where(t >= 2, um2, s2_ref[0])
        u_ref[0] = u
    else:
        uext[0:hist, :] = uext[tm:tm + hist, :]
        utail_ref[0] = u[tm - hist:tm, :]
    y = cw_ref[0:1, :] * um2 + cw_ref[1:2, :] * um1 + cw_ref[2:3, :] * u
    bg = _mm(xn, win_ref[:, OFF_B:OFF_B + D_CONV])
    conv_out = _mm((bg * y).astype(BF16), wco_ref[...])
    gc = _mm(xn, win_ref[:, OFF_GC:OFF_GC + D_MODEL])
    cm_ref[0] = (jax.nn.sigmoid(gc) * conv_out).astype(cm_ref.dtype)
    ga = _mm(xn, win_ref[:, OFF_GA:OFF_GA + D_MODEL])
    sga_ref[0] = jax.nn.sigmoid(ga).astype(sga_ref.dtype)

    seg_r = lax.broadcasted_iota(jnp.int32, (MXU_DIM, MXU_DIM), 0) // HEAD_DIM
    seg_c = lax.broadcasted_iota(jnp.int32, (MXU_DIM, MXU_DIM), 1) // HEAD_DIM
    seg = jnp.where(seg_r == seg_c, 1.0, 0.0).astype(BF16)
    for j in range(D_QKV // MXU_DIM):
        lo = j * MXU_DIM
        q = _mm(xn, win_ref[:, OFF_Q + lo:OFF_Q + lo + MXU_DIM])
        ssq = _mm((q * q).astype(BF16), seg)
        qn = q * lax.rsqrt(ssq * (1.0 / HEAD_DIM) + RMS_EPS) * (qg_ref[...] * ATT_SCALE)
        qkv_ref[0, :, lo:lo + MXU_DIM] = qn.astype(qkv_ref.dtype)
        k = _mm(xn, win_ref[:, OFF_K + lo:OFF_K + lo + MXU_DIM])
        ssk = _mm((k * k).astype(BF16), seg)
        kn = k * lax.rsqrt(ssk * (1.0 / HEAD_DIM) + RMS_EPS) * kg_ref[...]
        qkv_ref[0, :, D_QKV + lo:D_QKV + lo + MXU_DIM] = kn.astype(qkv_ref.dtype)
        v = _mm(xn, win_ref[:, OFF_V + lo:OFF_V + lo + MXU_DIM])
        qkv_ref[0, :, 2 * D_QKV + lo:2 * D_QKV + lo + MXU_DIM] = v.astype(qkv_ref.dtype)


def _in_proj(x, shift1, scale1, g1, w_in, conv_w, w_conv_out, qg, kg, *, tm, hist1=None, hist2=None,
             dec_seq=1):
    nb, ns, d = x.shape
    decode = hist1 is not None
    assert ns % tm == 0
    nmod = shift1.shape[1]
    tmod = 1 if nmod == 1 else tm
    row = lambda b, s: (b, s, 0)
    mod_map = (lambda b, s: (b, 0, 0)) if nmod == 1 else row
    act_dtype = F32 if decode else BF16
    in_specs = [pl.BlockSpec((1, tm, d), row),
                pl.BlockSpec((1, tmod, d), mod_map),
                pl.BlockSpec((1, tmod, d), mod_map),
                _const_spec((1, d)),
                _const_spec((d, D_PROJ)),
                _const_spec((CONV_WIDTH, D_CONV)),
                _const_spec((D_CONV, D_MODEL)),
                _const_spec((1, MXU_DIM)),
                _const_spec((1, MXU_DIM))]
    args = [x, shift1, scale1, g1, w_in, conv_w, w_conv_out, qg, kg]
    out_shape = [jax.ShapeDtypeStruct((nb, ns, D_MODEL), BF16),
                 jax.ShapeDtypeStruct((nb, ns, D_MODEL), BF16),
                 jax.ShapeDtypeStruct((nb, ns, 3 * D_QKV), act_dtype)]
    out_specs = [pl.BlockSpec((1, tm, D_MODEL), row),
                 pl.BlockSpec((1, tm, D_MODEL), row),
                 pl.BlockSpec((1, tm, 3 * D_QKV), row)]
    if decode:
        in_specs += [pl.BlockSpec((1, tm, D_CONV), row), pl.BlockSpec((1, tm, D_CONV), row)]
        args += [hist1, hist2]
        out_shape.append(jax.ShapeDtypeStruct((nb, ns, D_CONV), F32))
        out_specs.append(pl.BlockSpec((1, tm, D_CONV), row))
    else:
        out_shape.append(jax.ShapeDtypeStruct((nb, SUBLANES, D_CONV), F32))
        out_specs.append(pl.BlockSpec((1, SUBLANES, D_CONV), lambda b, s: (b, 0, 0)))
    return pl.pallas_call(
        functools.partial(_in_kernel, decode=decode, tm=tm, dec_seq=dec_seq),
        grid=(nb, ns // tm),
        in_specs=in_specs,
        out_specs=out_specs,
        out_shape=out_shape,
        scratch_shapes=[pltpu.VMEM((tm + 2 * SUBLANES, D_CONV), F32)],
        compiler_params=pltpu.CompilerParams(dimension_semantics=("arbitrary", "arbitrary"),
                                             vmem_limit_bytes=VMEM_LIMIT_BYTES),
        name="in_proj_decode" if decode else "in_proj",
    )(*args)


HEADS_PER_PASS = MXU_DIM // HEAD_DIM


def _attn_kernel(q_ref, kc_ref, kp_ref, vc_ref, vp_ref, tb_ref, o_ref, lse_ref, *, nb):
    step = pl.program_id(2)
    blk = ATT_BLOCK
    lane = lax.broadcasted_iota(jnp.int32, (1, MXU_DIM), 1)
    hmask = [(lane >= HEAD_DIM * h) & (lane < HEAD_DIM * (h + 1)) for h in range(HEADS_PER_PASS)]
    col = lax.broadcasted_iota(jnp.int32, (1, 2 * blk), 1)
    lane_out = lax.broadcasted_iota(jnp.int32, (1, LANES), 1)
    for n in range(nb):
        rows = slice(n * blk, (n + 1) * blk)
        lse_acc = jnp.zeros((blk, LANES), F32)
        for hp in range(N_HEADS // HEADS_PER_PASS):
            cols = slice(hp * MXU_DIM, (hp + 1) * MXU_DIM)
            q4 = q_ref[0, rows, cols]
            lhs = jnp.concatenate([jnp.where(hmask[h], q4, jnp.zeros_like(q4))
                                   for h in range(HEADS_PER_PASS)], axis=0)
            if n == 0:
                k_prev, v_prev = kp_ref[0, :, cols], vp_ref[0, :, cols]
            else:
                prev = slice((n - 1) * blk, n * blk)
                k_prev, v_prev = kc_ref[0, prev, cols], vc_ref[0, prev, cols]
            kk = jnp.concatenate([k_prev, kc_ref[0, rows, cols]], axis=0)
            vv = jnp.concatenate([v_prev, vc_ref[0, rows, cols]], axis=0)
            s = lax.dot_general(lhs, kk, (((1,), (1,)), ((), ())), preferred_element_type=F32)
            s = s + tb_ref[hp * HEADS_PER_PASS * blk:(hp + 1) * HEADS_PER_PASS * blk, :]
            if n == 0:
                s = jnp.where((step == 0) & (col < blk), NEG, s)
            m = jnp.max(s, axis=-1, keepdims=True)
            p = jnp.exp(s - m)
            l = jnp.sum(p, axis=-1, keepdims=True)
            pv = _mm(p.astype(BF16), vv) * (1.0 / l)
            lse = m + jnp.log(l)
            o4 = jnp.zeros((blk, MXU_DIM), F32)
            for h in range(HEADS_PER_PASS):
                hr = slice(h * blk, (h + 1) * blk)
                o4 = jnp.where(hmask[h], pv[hr, :], o4)
                lse_acc = jnp.where(lane_out == hp * HEADS_PER_PASS + h, lse[hr, :], lse_acc)
            o_ref[0, rows, cols] = o4.astype(o_ref.dtype)
        lse_ref[0, rows, :] = lse_acc


def _prompt_attention(qkv, tb, g):
    nbatch, ns, _ = qkv.shape
    _, dil = DILATED_GROUPS[g]
    assert ns % (dil * ATT_BLOCK) == 0
    nl = ns // dil
    nblk = nl // ATT_BLOCK
    nb = min(ATT_NB, nblk)
    assert nblk % nb == 0
    view = qkv.reshape(nbatch, nl, dil * 3 * D_QKV)
    per_r = 3 * N_GROUPS
    tq = nb * ATT_BLOCK
    cur = lambda off: (lambda b, r, i: (b, i, r * per_r + off + g))
    prev = lambda off: (lambda b, r, i: (b, jnp.maximum(i * nb - 1, 0), r * per_r + off + g))
    o, lse = pl.pallas_call(
        functools.partial(_attn_kernel, nb=nb),
        grid=(nbatch, dil, nblk // nb),
        in_specs=[pl.BlockSpec((1, tq, D_ATT), cur(0)),
                  pl.BlockSpec((1, tq, D_ATT), cur(N_GROUPS)),
                  pl.BlockSpec((1, ATT_BLOCK, D_ATT), prev(N_GROUPS)),
                  pl.BlockSpec((1, tq, D_ATT), cur(2 * N_GROUPS)),
                  pl.BlockSpec((1, ATT_BLOCK, D_ATT), prev(2 * N_GROUPS)),
                  pl.BlockSpec((N_HEADS * ATT_BLOCK, 2 * ATT_BLOCK), lambda b, r, i: (0, 0))],
        out_specs=[pl.BlockSpec((1, tq, D_ATT), lambda b, r, i: (b, i, r)),
                   pl.BlockSpec((1, tq, LANES), lambda b, r, i: (b, i, r))],
        out_shape=[jax.ShapeDtypeStruct((nbatch, nl, dil * D_ATT), BF16),
                   jax.ShapeDtypeStruct((nbatch, nl, dil * LANES), F32)],
        compiler_params=pltpu.CompilerParams(dimension_semantics=("arbitrary",) * 3,
                                             vmem_limit_bytes=VMEM_LIMIT_BYTES),
        name=f"prompt_attn_g{g}",
    )(view, view, view, view, view, tb)
    return o.reshape(nbatch, ns, D_ATT), lse.reshape(nbatch, ns, LANES)


def _dec_kernel(qt_ref, knt_ref, vnt_ref, kt_ref, vt_ref, bdec_ref, bnew_ref,
                skt_ref, svt_ref, ot_ref, lse_ref, *, width, dec_seq):
    b = pl.program_id(0)
    nch = width // LANES
    lane = lax.broadcasted_iota(jnp.int32, (1, LANES), 1)
    qt, knt, vnt = qt_ref[0], knt_ref[0], vnt_ref[0]
    keep = LANES - dec_seq

    for src, new, dst in ((kt_ref, knt, skt_ref), (vt_ref, vnt, svt_ref)):
        tail = jnp.zeros((D_ATT, LANES), F32)
        for t in range(dec_seq):
            tail = jnp.where(lane == keep + t, new[:, t:t + 1], tail)
        cur = pltpu.roll(src[0, :, 0:LANES], keep, 1)
        for c in range(nch):
            nxt = pltpu.roll(src[0, :, (c + 1) * LANES:(c + 2) * LANES], keep, 1) if c + 1 < nch else tail
            dst[0, :, c * LANES:(c + 1) * LANES] = jnp.where(lane < keep, cur, nxt)
            cur = nxt

    pack = DEC_PACK // dec_seq
    slot = (b % pack) * dec_seq

    @pl.when(b % pack == 0)
    def _():
        ot_ref[...] = jnp.zeros_like(ot_ref)
        lse_ref[...] = jnp.zeros_like(lse_ref)

    def head_sum(prod):
        return jnp.sum(prod.reshape(N_HEADS, HEAD_DIM, prod.shape[-1]), axis=1)

    def head_bcast(w):
        return jnp.broadcast_to(w[:, None, :], (N_HEADS, HEAD_DIM, w.shape[-1])).reshape(D_ATT, w.shape[-1])

    for t in range(dec_seq):
        qb = jnp.broadcast_to(qt[:, t:t + 1], (D_ATT, LANES))
        s_c = [head_sum(kt_ref[0, :, c * LANES:(c + 1) * LANES] * qb)
               + bdec_ref[t, :, c * LANES:(c + 1) * LANES] for c in range(nch)]
        s_n = head_sum(knt * qt[:, t:t + 1]) + bnew_ref[t]
        m = jnp.max(s_n, axis=-1, keepdims=True)
        for c in range(nch):
            m = jnp.maximum(m, jnp.max(s_c[c], axis=-1, keepdims=True))
        p_n = jnp.exp(s_n - m)
        l = jnp.sum(p_n, axis=-1, keepdims=True)
        acc = jnp.zeros((D_ATT, LANES), F32)
        for c in range(nch):
            p_c = jnp.exp(s_c[c] - m)
            l = l + jnp.sum(p_c, axis=-1, keepdims=True)
            acc = acc + head_bcast(p_c) * vt_ref[0, :, c * LANES:(c + 1) * LANES]
        o = jnp.sum(acc, axis=-1, keepdims=True) + jnp.sum(head_bcast(p_n) * vnt, axis=-1, keepdims=True)
        o = o * head_bcast(1.0 / l)
        lse = m + jnp.log(l)
        ot_ref[0] = jnp.where(lane == slot + t, o, ot_ref[0])
        lse_ref[0] = jnp.where(lane == slot + t, lse, lse_ref[0])


def _decode_attention(qt, knt, vnt, kt, vt, bdec, bnew):
    db, _, width = kt.shape
    dec_seq = qt.shape[-1]
    pack = DEC_PACK // dec_seq
    assert db % pack == 0 and width % LANES == 0
    per_b = lambda b: (b, 0, 0)
    packed = lambda b: (b // pack, 0, 0)
    return pl.pallas_call(
        functools.partial(_dec_kernel, width=width, dec_seq=dec_seq),
        grid=(db,),
        in_specs=[pl.BlockSpec((1, D_ATT, dec_seq), per_b),
                  pl.BlockSpec((1, D_ATT, dec_seq), per_b),
                  pl.BlockSpec((1, D_ATT, dec_seq), per_b),
                  pl.BlockSpec((1, D_ATT, width), per_b),
                  pl.BlockSpec((1, D_ATT, width), per_b),
                  pl.BlockSpec((dec_seq, N_HEADS, width), lambda b: (0, 0, 0)),
                  pl.BlockSpec((dec_seq, N_HEADS, dec_seq), lambda b: (0, 0, 0))],
        out_specs=[pl.BlockSpec((1, D_ATT, width), per_b),
                   pl.BlockSpec((1, D_ATT, width), per_b),
                   pl.BlockSpec((1, D_ATT, DEC_PACK), packed),
                   pl.BlockSpec((1, N_HEADS, DEC_PACK), packed)],
        out_shape=[jax.ShapeDtypeStruct((db, D_ATT, width), F32),
                   jax.ShapeDtypeStruct((db, D_ATT, width), F32),
                   jax.ShapeDtypeStruct((db // pack, D_ATT, DEC_PACK), F32),
                   jax.ShapeDtypeStruct((db // pack, N_HEADS, DEC_PACK), F32)],
        compiler_params=pltpu.CompilerParams(dimension_semantics=("arbitrary",),
                                             vmem_limit_bytes=VMEM_LIMIT_BYTES),
        name=f"decode_attn_w{width}",
    )(qt, knt, vnt, kt, vt, bdec, bnew)


def _out_kernel(x_ref, cm_ref, sga_ref, o0_ref, o1_ref, o2_ref, l0_ref, l1_ref, l2_ref,
                g1_ref, sh2_ref, sc2_ref, g2_ref, n2_ref, wao_ref, wo_ref, w1_ref, w2_ref, y_ref):
    lses = [l0_ref[0], l1_ref[0], l2_ref[0]]
    mx = jnp.maximum(jnp.maximum(lses[0], lses[1]), lses[2])
    es = [jnp.exp(l - mx) for l in lses]
    inv = 1.0 / (es[0] + es[1] + es[2])
    er = lax.broadcasted_iota(jnp.int32, (LANES, D_ATT), 0)
    ec = lax.broadcasted_iota(jnp.int32, (LANES, D_ATT), 1) // HEAD_DIM
    expand = jnp.where(er == ec, 1.0, 0.0).astype(BF16)
    o = None
    for e, o_ref in zip(es, (o0_ref, o1_ref, o2_ref)):
        w = e * inv
        w_hi = w.astype(BF16)
        w_lo = (w - w_hi.astype(F32)).astype(BF16)
        wexp = _mm(w_hi, expand) + _mm(w_lo, expand)
        term = wexp * o_ref[0].astype(F32)
        o = term if o is None else o + term
    attn_out = _mm(o.astype(BF16), wao_ref[...])
    mixed = cm_ref[0].astype(F32) + sga_ref[0].astype(F32) * attn_out
    x1 = x_ref[0] + g1_ref[0] * _mm(mixed.astype(BF16), wo_ref[...])
    ms = jnp.mean(x1 * x1, axis=-1, keepdims=True)
    xn2 = x1 * lax.rsqrt(ms + RMS_EPS) * n2_ref[...]
    xn2 = (xn2 * (1.0 + sc2_ref[0]) + sh2_ref[0]).astype(BF16)
    acc = None
    for f in range(D_FF // FF_CHUNK):
        cols = slice(f * FF_CHUNK, (f + 1) * FF_CHUNK)
        hid = jnp.maximum(_mm(xn2, w1_ref[:, cols]), 0.0)
        part = _mm((hid * hid).astype(BF16), w2_ref[cols, :])
        acc = part if acc is None else acc + part
    y_ref[0] = x1 + g2_ref[0] * acc


def _out_proj(x, cm, sga, os, lses, gate1, shift2, scale2, gate2, n2, wao, wo, w1, w2, *, tm):
    nb, ns, d = x.shape
    assert ns % tm == 0
    nmod = gate1.shape[1]
    tmod = 1 if nmod == 1 else tm
    row = lambda b, s: (b, s, 0)
    mod_map = (lambda b, s: (b, 0, 0)) if nmod == 1 else row
    mod_spec = pl.BlockSpec((1, tmod, d), mod_map)
    in_specs = ([pl.BlockSpec((1, tm, d), row)] * 3
                + [pl.BlockSpec((1, tm, D_ATT), row)] * 3
                + [pl.BlockSpec((1, tm, LANES), row)] * 3
                + [mod_spec] * 4
                + [_const_spec((1, d)), _const_spec((D_ATT, d)), _const_spec((d, d)),
                   _const_spec((d, D_FF)), _const_spec((D_FF, d))])
    return pl.pallas_call(
        _out_kernel,
        grid=(nb, ns // tm),
        in_specs=in_specs,
        out_specs=pl.BlockSpec((1, tm, d), row),
        out_shape=jax.ShapeDtypeStruct((nb, ns, d), F32),
        compiler_params=pltpu.CompilerParams(dimension_semantics=("arbitrary", "arbitrary"),
                                             vmem_limit_bytes=VMEM_LIMIT_BYTES),
        name="out_proj",
    )(x, cm, sga, *os, *lses, gate1, shift2, scale2, gate2, n2, wao, wo, w1, w2)


def _t5_causal_bucket(dist):
    max_exact = N_BUCKETS // 2
    ratio = jnp.maximum(dist, 1).astype(F32) / max_exact
    large = max_exact + (jnp.log(ratio) / math.log(MAX_DISTANCE / max_exact)
                         * (N_BUCKETS - max_exact)).astype(jnp.int32)
    large = jnp.minimum(large, N_BUCKETS - 1)
    return jnp.where(dist < max_exact, dist, large)


def _group_bias(rel_bias, g):
    _, dil = DILATED_GROUPS[g]
    steps = jnp.arange(WK + 1)
    b = rel_bias[_t5_causal_bucket(steps * dil), g * N_HEADS:(g + 1) * N_HEADS]
    return b.T.astype(F32)


def _prompt_bias_table(bias):
    i = np.arange(ATT_BLOCK)[:, None]
    j = np.arange(2 * ATT_BLOCK)[None, :]
    off = i + ATT_BLOCK - j
    band = (off >= 0) & (off <= WK)
    tb = jnp.where(band[None], bias[:, np.clip(off, 0, WK)], NEG)
    return tb.reshape(N_HEADS * ATT_BLOCK, 2 * ATT_BLOCK)


def _decode_bias_tables(bias, width, dil, dec_seq):
    t = np.arange(dec_seq)[:, None]
    w = np.arange(width)[None, :]
    back = width + t - w
    ok = (back % dil == 0) & (back // dil <= WK)
    bdec = jnp.where(ok[:, None, :], bias[:, np.clip(back // dil, 0, WK)].transpose(1, 0, 2), NEG)
    tn = np.arange(dec_seq)[None, :]
    backn = t - tn
    okn = (backn >= 0) & (backn % dil == 0) & (backn // dil <= WK)
    bnew = jnp.where(okn[:, None, :], bias[:, np.clip(backn // dil, 0, WK)].transpose(1, 0, 2), NEG)
    return bdec, bnew


def _layer(x_prompt, x_sample, c_prompt, c_sample, state_conv, caches, rel_bias, norm1_g, norm2_g,
           w_ada, b_ada, w_in, conv_w, q_norm_g, k_norm_g, w_conv_out, w_attn_out, w_o, w_mlp_in,
           w_mlp_out):
    nbatch, ns, d = x_prompt.shape
    db, dec_seq, _ = x_sample.shape
    ntok = db * dec_seq

    w_in_b = w_in.astype(BF16)
    wco_b = w_conv_out.astype(BF16)
    wao_b = w_attn_out.astype(BF16)
    wo_b = w_o.astype(BF16)
    w1_b = w_mlp_in.astype(BF16)
    w2_b = w_mlp_out.astype(BF16)
    g1 = norm1_g.reshape(1, d)
    g2n = norm2_g.reshape(1, d)
    qg = jnp.tile(q_norm_g.reshape(1, HEAD_DIM), (1, MXU_DIM // HEAD_DIM))
    kg = jnp.tile(k_norm_g.reshape(1, HEAD_DIM), (1, MXU_DIM // HEAD_DIM))

    n_c = nbatch + db
    n_pad = -(-n_c // SUBLANES) * SUBLANES
    c_all = jnp.concatenate([c_prompt, c_sample, jnp.zeros((n_pad - n_c, d), F32)], axis=0)
    mod = _ada(c_all, w_ada, b_ada)
    mod_p = mod[:nbatch].reshape(nbatch, 1, N_MOD, d)
    mod_s = jnp.broadcast_to(mod[nbatch:n_c].reshape(db, 1, N_MOD, d), (db, dec_seq, N_MOD, d))
    mod_s = mod_s.reshape(1, ntok, N_MOD, d)
    mp = [mod_p[:, :, i] for i in range(N_MOD)]
    msn = [mod_s[:, :, i] for i in range(N_MOD)]

    biases = [_group_bias(rel_bias, g) for g in range(N_GROUPS)]

    cm, sga, qkv, utail = _in_proj(x_prompt, mp[0], mp[1], g1, w_in_b, conv_w, wco_b, qg, kg, tm=TM_IN)
    os, lses = [], []
    for g in range(N_GROUPS):
        o, lse = _prompt_attention(qkv, _prompt_bias_table(biases[g]), g)
        os.append(o)
        lses.append(lse)
    y_prompt = _out_proj(x_prompt, cm, sga, os, lses, mp[2], mp[3], mp[4], mp[5], g2n,
                         wao_b, wo_b, w1_b, w2_b, tm=TM_OUT)
    p_conv = utail[:, SUBLANES - (CONV_WIDTH - 1):, :]
    p_kv = []
    for g, (window, _) in enumerate(DILATED_GROUPS):
        keep = min(window, ns)
        for part in (1, 2):
            lo = part * D_QKV + g * D_ATT
            p_kv.append(qkv[:, ns - keep:, lo:lo + D_ATT].astype(F32).reshape(nbatch, keep, N_HEADS, HEAD_DIM))

    xs = x_sample.reshape(1, ntok, d)
    zero_row = jnp.zeros((db, 1, D_CONV), F32)
    hist1 = jnp.concatenate([state_conv[:, 1:2]] + [zero_row] * (dec_seq - 1), axis=1).reshape(1, ntok, D_CONV)
    hist2 = jnp.concatenate([state_conv[:, 0:1], state_conv[:, 1:2]] + [zero_row] * (dec_seq - 2),
                            axis=1).reshape(1, ntok, D_CONV)
    cm_s, sga_s, qkv_s, u_s = _in_proj(xs, msn[0], msn[1], g1, w_in_b, conv_w, wco_b, qg, kg, tm=ntok,
                                       hist1=hist1, hist2=hist2, dec_seq=dec_seq)
    s_conv = u_s.reshape(db, dec_seq, D_CONV)[:, dec_seq - (CONV_WIDTH - 1):]
    qkv_t = qkv_s.reshape(db, dec_seq, 3, N_GROUPS, D_ATT).transpose(2, 3, 0, 4, 1)
    os_s, lses_s, s_kv = [], [], []
    for g, (window, dil) in enumerate(DILATED_GROUPS):
        ck, cv = caches[2 * g], caches[2 * g + 1]
        width = ck.shape[1]
        assert width == window and width + dec_seq >= window
        kt = ck.transpose(0, 2, 3, 1).reshape(db, D_ATT, width)
        vt = cv.transpose(0, 2, 3, 1).reshape(db, D_ATT, width)
        bdec, bnew = _decode_bias_tables(biases[g], width, dil, dec_seq)
        skt, svt, ot, lset = _decode_attention(qkv_t[0, g], qkv_t[1, g], qkv_t[2, g], kt, vt, bdec, bnew)
        s_kv.append(skt.reshape(db, N_HEADS, HEAD_DIM, width).transpose(0, 3, 1, 2))
        s_kv.append(svt.reshape(db, N_HEADS, HEAD_DIM, width).transpose(0, 3, 1, 2))
        os_s.append(ot.transpose(0, 2, 1).reshape(1, ntok, D_ATT).astype(BF16))
        lse_tok = lset.transpose(0, 2, 1).reshape(1, ntok, N_HEADS)
        lses_s.append(jnp.pad(lse_tok, ((0, 0), (0, 0), (0, LANES - N_HEADS))))
    y_sample = _out_proj(xs, cm_s, sga_s, os_s, lses_s, msn[2], msn[3], msn[4], msn[5], g2n,
                         wao_b, wo_b, w1_b, w2_b, tm=ntok).reshape(db, dec_seq, d)
    return y_prompt, y_sample, [p_conv] + p_kv, [s_conv] + s_kv


def kernel(x_prompt, x_sample, c_prompt, c_sample, state_conv, cache_k1, cache_v1, cache_k2, cache_v2,
           cache_k3, cache_v3, rel_bias, norm1_g, norm2_g, w_ada, b_ada, w_in, conv_w, q_norm_g, k_norm_g,
           w_conv_out, w_attn_out, w_o, w_mlp_in, w_mlp_out):
    depth = w_in.shape[0]
    caches = (cache_k1, cache_v1, cache_k2, cache_v2, cache_k3, cache_v3)
    yp, ys = x_prompt, x_sample
    p_states = [[] for _ in range(1 + 2 * N_GROUPS)]
    s_states = [[] for _ in range(1 + 2 * N_GROUPS)]
    for l in range(depth):
        yp, ys, p_new, s_new = _layer(
            yp, ys, c_prompt, c_sample, state_conv[l], [c[l] for c in caches], rel_bias,
            norm1_g[l], norm2_g[l], w_ada[l], b_ada[l], w_in[l], conv_w[l], q_norm_g[l], k_norm_g[l],
            w_conv_out[l], w_attn_out[l], w_o[l], w_mlp_in[l], w_mlp_out[l])
        for lst, a in zip(p_states, p_new):
            lst.append(a)
        for lst, a in zip(s_states, s_new):
            lst.append(a)
    p_out = [jnp.stack(a) for a in p_states]
    s_out = [jnp.stack(a) for a in s_states]
    return (yp, ys, *p_out, *s_out)
```

```python
import functools
import math

import numpy as np
import jax
import jax.numpy as jnp
from jax import lax
from jax.experimental import pallas as pl
from jax.experimental.pallas import tpu as pltpu

F32 = jnp.float32
BF16 = jnp.bfloat16

D_MODEL = 1024
D_CONV = D_MODEL
CONV_WIDTH = 3
HEAD_DIM = 64
N_HEADS = 8
D_ATT = N_HEADS * HEAD_DIM
DILATED_GROUPS = ((128, 1), (512, 4), (2048, 16))
N_GROUPS = len(DILATED_GROUPS)
DILS = tuple(d for _, d in DILATED_GROUPS)
D_QKV = N_GROUPS * D_ATT
D_FF = 4 * D_MODEL
N_BUCKETS = 32
MAX_DISTANCE = 2048
ATT_BLOCK = 128
WK = 128
N_MOD = 6
RMS_EPS = 1e-6
ATT_SCALE = HEAD_DIM ** -0.5
NEG = -1e30

OFF_H, OFF_B, OFF_C = 0, D_CONV, 2 * D_CONV
OFF_Q = 3 * D_CONV
OFF_K = OFF_Q + D_QKV
OFF_V = OFF_K + D_QKV
OFF_GC = OFF_V + D_QKV
OFF_GA = OFF_GC + D_MODEL
D_PROJ = OFF_GA + D_MODEL

LANES = 128
SUBLANES = 8
MXU_DIM = 256
VMEM_LIMIT_BYTES = 56 * 1024 * 1024

TM_IN = 256
TM_OUT = 256
ATT_NB = 4
FF_CHUNK = 512
DEC_PACK = LANES
DEC_BLOCK_BYTES = 4 * 1024 * 1024

assert all(w // d == WK for w, d in DILATED_GROUPS)


def _mm(a, b):
    return jnp.dot(a, b, preferred_element_type=F32)


def _const_spec(shape):
    nd = len(shape)
    return pl.BlockSpec(shape, lambda *_: (0,) * nd, pipeline_mode=pl.Buffered(1))


def _ada_kernel(c_ref, w_ref, b_ref, o_ref):
    c = c_ref[...]
    s = c * jax.nn.sigmoid(c)
    o_ref[...] = _mm(s.astype(BF16), w_ref[...].astype(BF16)) + b_ref[...]


def _ada(c_all, w_ada, b_ada):
    n, d = c_all.shape
    nout = w_ada.shape[1]
    tn = 1024
    return pl.pallas_call(
        _ada_kernel,
        grid=(nout // tn,),
        in_specs=[pl.BlockSpec((n, d), lambda j: (0, 0)),
                  pl.BlockSpec((d, tn), lambda j: (0, j)),
                  pl.BlockSpec((1, tn), lambda j: (0, j))],
        out_specs=pl.BlockSpec((n, tn), lambda j: (0, j)),
        out_shape=jax.ShapeDtypeStruct((n, nout), F32),
        compiler_params=pltpu.CompilerParams(dimension_semantics=("arbitrary",),
                                             vmem_limit_bytes=VMEM_LIMIT_BYTES),
        name="ada",
    )(c_all, w_ada, b_ada.reshape(1, nout))


def _in_kernel(*refs, decode, tm, dec_seq, dils):
    n_in = 11 if decode else 9
    x_ref, sh_ref, sc_ref, g1_ref, win_ref, cw_ref, wco_ref, qg_ref, kg_ref = refs[:9]
    if decode:
        s1_ref, s2_ref = refs[9:11]
    cm_ref, sga_ref = refs[n_in:n_in + 2]
    qkv_refs = refs[n_in + 2:n_in + 2 + 3 * N_GROUPS]
    ustate_ref, uext, stg = refs[n_in + 2 + 3 * N_GROUPS:]

    x = x_ref[0]
    ms = jnp.mean(x * x, axis=-1, keepdims=True)
    xn = x * lax.rsqrt(ms + RMS_EPS) * g1_ref[...]
    xn = (xn * (1.0 + sc_ref[0]) + sh_ref[0]).astype(BF16)

    h = _mm(xn, win_ref[:, OFF_H:OFF_H + D_CONV])
    c = _mm(xn, win_ref[:, OFF_C:OFF_C + D_CONV])
    u = c * h
    hist = SUBLANES
    if decode:
        uext[0:hist, :] = jnp.zeros((hist, D_CONV), F32)
    else:
        @pl.when(pl.program_id(1) == 0)
        def _():
            uext[0:hist, :] = jnp.zeros((hist, D_CONV), F32)
    uext[hist:hist + tm, :] = u
    um1 = uext[hist - 1:hist - 1 + tm, :]
    um2 = uext[hist - 2:hist - 2 + tm, :]
    if decode:
        t = lax.broadcasted_iota(jnp.int32, (tm, 1), 0) % dec_seq
        um1 = jnp.where(t >= 1, um1, s1_ref[0])
        um2 = jnp.where(t >= 2, um2, s2_ref[0])
        ustate_ref[0] = u
    else:
        uext[0:hist, :] = uext[tm:tm + hist, :]
        ustate_ref[0] = u[tm - hist:tm, :]
    y = cw_ref[0:1, :] * um2 + cw_ref[1:2, :] * um1 + cw_ref[2:3, :] * u
    bg = _mm(xn, win_ref[:, OFF_B:OFF_B + D_CONV])
    conv_out = _mm((bg * y).astype(BF16), wco_ref[...])
    gc = _mm(xn, win_ref[:, OFF_GC:OFF_GC + D_MODEL])
    cm_ref[0] = (jax.nn.sigmoid(gc) * conv_out).astype(cm_ref.dtype)
    ga = _mm(xn, win_ref[:, OFF_GA:OFF_GA + D_MODEL])
    sga_ref[0] = jax.nn.sigmoid(ga).astype(sga_ref.dtype)

    seg_r = lax.broadcasted_iota(jnp.int32, (MXU_DIM, MXU_DIM), 0) // HEAD_DIM
    seg_c = lax.broadcasted_iota(jnp.int32, (MXU_DIM, MXU_DIM), 1) // HEAD_DIM
    seg = jnp.where(seg_r == seg_c, 1.0, 0.0).astype(BF16)

    def put(out_ref, val, dil, cols):
        if dil == 1:
            out_ref[0, 0, :, cols] = val.astype(out_ref.dtype)
            return
        for i in range(MXU_DIM // LANES):
            stg[i] = val[:, i * LANES:(i + 1) * LANES]
        for r in range(dil):
            for i in range(MXU_DIM // LANES):
                lo_c = cols.start + i * LANES
                out_ref[0, r, :, lo_c:lo_c + LANES] = (
                    stg[i, pl.ds(r, tm // dil, stride=dil), :].astype(out_ref.dtype))

    for j in range(D_QKV // MXU_DIM):
        lo = j * MXU_DIM
        g = lo // D_ATT
        cols = slice(lo - g * D_ATT, lo - g * D_ATT + MXU_DIM)
        q_ref, k_ref, v_ref = qkv_refs[3 * g:3 * g + 3]
        q = _mm(xn, win_ref[:, OFF_Q + lo:OFF_Q + lo + MXU_DIM])
        ssq = _mm((q * q).astype(BF16), seg)
        put(q_ref, q * lax.rsqrt(ssq * (1.0 / HEAD_DIM) + RMS_EPS) * (qg_ref[...] * ATT_SCALE), dils[g], cols)
        k = _mm(xn, win_ref[:, OFF_K + lo:OFF_K + lo + MXU_DIM])
        ssk = _mm((k * k).astype(BF16), seg)
        put(k_ref, k * lax.rsqrt(ssk * (1.0 / HEAD_DIM) + RMS_EPS) * kg_ref[...], dils[g], cols)
        put(v_ref, _mm(xn, win_ref[:, OFF_V + lo:OFF_V + lo + MXU_DIM]), dils[g], cols)


def _in_proj(x, shift1, scale1, g1, w_in, conv_w, w_conv_out, qg, kg, *, tm, dils, hist1=None, hist2=None,
             dec_seq=1):
    nb, ns, d = x.shape
    decode = hist1 is not None
    assert ns % tm == 0 and all(tm % (dil * 2 * SUBLANES) == 0 for dil in dils)
    nmod = shift1.shape[1]
    tmod = 1 if nmod == 1 else tm
    row = lambda b, s: (b, s, 0)
    mod_map = (lambda b, s: (b, 0, 0)) if nmod == 1 else row
    act_dtype = F32 if decode else BF16
    in_specs = [pl.BlockSpec((1, tm, d), row),
                pl.BlockSpec((1, tmod, d), mod_map),
                pl.BlockSpec((1, tmod, d), mod_map),
                _const_spec((1, d)),
                _const_spec((d, D_PROJ)),
                _const_spec((CONV_WIDTH, D_CONV)),
                _const_spec((D_CONV, D_MODEL)),
                _const_spec((1, MXU_DIM)),
                _const_spec((1, MXU_DIM))]
    args = [x, shift1, scale1, g1, w_in, conv_w, w_conv_out, qg, kg]
    if decode:
        in_specs += [pl.BlockSpec((1, tm, D_CONV), row), pl.BlockSpec((1, tm, D_CONV), row)]
        args += [hist1, hist2]
    out_shape = [jax.ShapeDtypeStruct((nb, ns, D_MODEL), BF16),
                 jax.ShapeDtypeStruct((nb, ns, D_MODEL), BF16)]
    out_specs = [pl.BlockSpec((1, tm, D_MODEL), row),
                 pl.BlockSpec((1, tm, D_MODEL), row)]
    for dil in dils:
        for _ in range(3):
            out_shape.append(jax.ShapeDtypeStruct((nb, dil, ns // dil, D_ATT), act_dtype))
            out_specs.append(pl.BlockSpec((1, dil, tm // dil, D_ATT), lambda b, s: (b, 0, s, 0)))
    if decode:
        out_shape.append(jax.ShapeDtypeStruct((nb, ns, D_CONV), F32))
        out_specs.append(pl.BlockSpec((1, tm, D_CONV), row))
    else:
        out_shape.append(jax.ShapeDtypeStruct((nb, SUBLANES, D_CONV), F32))
        out_specs.append(pl.BlockSpec((1, SUBLANES, D_CONV), lambda b, s: (b, 0, 0)))
    outs = pl.pallas_call(
        functools.partial(_in_kernel, decode=decode, tm=tm, dec_seq=dec_seq, dils=tuple(dils)),
        grid=(nb, ns // tm),
        in_specs=in_specs,
        out_specs=out_specs,
        out_shape=out_shape,
        scratch_shapes=[pltpu.VMEM((tm + SUBLANES, D_CONV), F32),
                        pltpu.VMEM((MXU_DIM // LANES, tm, LANES), F32)],
        compiler_params=pltpu.CompilerParams(dimension_semantics=("arbitrary", "arbitrary"),
                                             vmem_limit_bytes=VMEM_LIMIT_BYTES),
        name="in_proj_decode" if decode else "in_proj",
    )(*args)
    return outs[0], outs[1], outs[2:2 + 3 * N_GROUPS], outs[-1]


HEADS_PER_PASS = MXU_DIM // HEAD_DIM


def _attn_kernel(q_ref, kc_ref, kp_ref, vc_ref, vp_ref, tb_ref, o_ref, lse_ref, *, nb):
    step = pl.program_id(2)
    blk = ATT_BLOCK
    lane = lax.broadcasted_iota(jnp.int32, (1, MXU_DIM), 1)
    hmask = [(lane >= HEAD_DIM * h) & (lane < HEAD_DIM * (h + 1)) for h in range(HEADS_PER_PASS)]
    col = lax.broadcasted_iota(jnp.int32, (1, 2 * blk), 1)
    lane_out = lax.broadcasted_iota(jnp.int32, (1, LANES), 1)
    for n in range(nb):
        rows = slice(n * blk, (n + 1) * blk)
        lse_acc = jnp.zeros((blk, LANES), F32)
        for hp in range(N_HEADS // HEADS_PER_PASS):
            cols = slice(hp * MXU_DIM, (hp + 1) * MXU_DIM)
            q4 = q_ref[0, 0, rows, cols]
            lhs = jnp.concatenate([jnp.where(hmask[h], q4, jnp.zeros_like(q4))
                                   for h in range(HEADS_PER_PASS)], axis=0)
            if n == 0:
                k_prev, v_prev = kp_ref[0, 0, :, cols], vp_ref[0, 0, :, cols]
            else:
                prev = slice((n - 1) * blk, n * blk)
                k_prev, v_prev = kc_ref[0, 0, prev, cols], vc_ref[0, 0, prev, cols]
            kk = jnp.concatenate([k_prev, kc_ref[0, 0, rows, cols]], axis=0)
            vv = jnp.concatenate([v_prev, vc_ref[0, 0, rows, cols]], axis=0)
            s = lax.dot_general(lhs, kk, (((1,), (1,)), ((), ())), preferred_element_type=F32)
            s = s + tb_ref[hp * HEADS_PER_PASS * blk:(hp + 1) * HEADS_PER_PASS * blk, :]
            if n == 0:
                s = jnp.where((step == 0) & (col < blk), NEG, s)
            m = jnp.max(s, axis=-1, keepdims=True)
            p = jnp.exp(s - m)
            l = jnp.sum(p, axis=-1, keepdims=True)
            pv = _mm(p.astype(BF16), vv) * (1.0 / l)
            lse = m + jnp.log(l)
            o4 = jnp.zeros((blk, MXU_DIM), F32)
            for h in range(HEADS_PER_PASS):
                hr = slice(h * blk, (h + 1) * blk)
                o4 = jnp.where(hmask[h], pv[hr, :], o4)
                lse_acc = jnp.where(lane_out == hp * HEADS_PER_PASS + h, lse[hr, :], lse_acc)
            o_ref[0, 0, rows, cols] = o4.astype(o_ref.dtype)
        lse_ref[0, 0, rows, :] = lse_acc


def _prompt_attention(q, k, v, tb):
    nbatch, dil, nl, _ = q.shape
    assert nl % ATT_BLOCK == 0
    nblk = nl // ATT_BLOCK
    nb = min(ATT_NB, nblk)
    assert nblk % nb == 0
    tq = nb * ATT_BLOCK
    cur = lambda b, r, i: (b, r, i, 0)
    prev = lambda b, r, i: (b, r, jnp.maximum(i * nb - 1, 0), 0)
    return pl.pallas_call(
        functools.partial(_attn_kernel, nb=nb),
        grid=(nbatch, dil, nblk // nb),
        in_specs=[pl.BlockSpec((1, 1, tq, D_ATT), cur),
                  pl.BlockSpec((1, 1, tq, D_ATT), cur),
                  pl.BlockSpec((1, 1, ATT_BLOCK, D_ATT), prev),
                  pl.BlockSpec((1, 1, tq, D_ATT), cur),
                  pl.BlockSpec((1, 1, ATT_BLOCK, D_ATT), prev),
                  pl.BlockSpec((N_HEADS * ATT_BLOCK, 2 * ATT_BLOCK), lambda b, r, i: (0, 0))],
        out_specs=[pl.BlockSpec((1, 1, tq, D_ATT), cur),
                   pl.BlockSpec((1, 1, tq, LANES), cur)],
        out_shape=[jax.ShapeDtypeStruct((nbatch, dil, nl, D_ATT), BF16),
                   jax.ShapeDtypeStruct((nbatch, dil, nl, LANES), F32)],
        compiler_params=pltpu.CompilerParams(dimension_semantics=("arbitrary",) * 3,
                                             vmem_limit_bytes=VMEM_LIMIT_BYTES),
        name=f"prompt_attn_d{dil}",
    )(q, k, k, v, v, tb)


def _dec_kernel(qp_ref, knp_ref, vnp_ref, kt_ref, vt_ref, bdec_ref, bnew_ref,
                skt_ref, svt_ref, ot_ref, lse_ref, *, width, dil, dec_seq, bb):
    step = pl.program_id(0)
    nch = width // LANES
    lane = lax.broadcasted_iota(jnp.int32, (1, LANES), 1)
    keep = LANES - dec_seq
    pack = DEC_PACK // dec_seq
    shared = dil >= dec_seq
    n_pass = 1 if shared else dec_seq

    @pl.when((step * bb) % pack == 0)
    def _():
        ot_ref[...] = jnp.zeros_like(ot_ref)
        lse_ref[...] = jnp.zeros_like(lse_ref)

    def head_sum(prod):
        return jnp.sum(prod.reshape(N_HEADS, HEAD_DIM, prod.shape[-1]), axis=1)

    def head_bcast(w):
        return jnp.broadcast_to(w[:, None, :], (N_HEADS, HEAD_DIM, w.shape[-1])).reshape(D_ATT, w.shape[-1])

    def lane_max(x, mask=None):
        return jnp.max(x if mask is None else jnp.where(mask, x, NEG), axis=-1, keepdims=True)

    def lane_sum(x, mask=None):
        return jnp.sum(x if mask is None else jnp.where(mask, x, 0.0), axis=-1, keepdims=True)

    def one_batch(i, carry):
        slot = ((step * bb + i) % pack) * dec_seq
        to_front = (LANES - slot) % LANES
        qr = pltpu.roll(qp_ref[0], to_front, 1)
        knr = pltpu.roll(knp_ref[0], to_front, 1)
        vnr = pltpu.roll(vnp_ref[0], to_front, 1)
        tok = [lane == t for t in range(dec_seq)]
        res = [lane % dil == t for t in range(dec_seq)] if shared else [None] * dec_seq
        qb = [jnp.broadcast_to(qr[:, t:t + 1], (D_ATT, LANES)) for t in range(dec_seq)]
        if shared:
            qsel = jnp.zeros((D_ATT, LANES), F32)
            for t in range(dec_seq):
                qsel = jnp.where(res[t], qb[t], qsel)
            q_pass = [qsel]
            s_new = [head_sum(knr * qr) + bnew_ref[0]]
        else:
            q_pass = qb
            s_new = [head_sum(knr * qb[t]) + bnew_ref[t] for t in range(dec_seq)]

        def shifted_copy(src_chunks_rolled, tail, dst):
            for c in range(nch):
                nxt = src_chunks_rolled[c + 1] if c + 1 < nch else tail
                dst[i, :, c * LANES:(c + 1) * LANES] = jnp.where(lane < keep, src_chunks_rolled[c], nxt)

        s = [[None] * nch for _ in range(n_pass)]
        rolled = []
        for c in range(nch):
            kc = kt_ref[i, :, c * LANES:(c + 1) * LANES]
            rolled.append(pltpu.roll(kc, keep, 1))
            for p in range(n_pass):
                s[p][c] = head_sum(kc * q_pass[p]) + bdec_ref[p, :, c * LANES:(c + 1) * LANES]
        shifted_copy(rolled, pltpu.roll(knr, keep, 1), skt_ref)

        smax = []
        for p in range(n_pass):
            mx = s[p][0]
            for c in range(1, nch):
                mx = jnp.maximum(mx, s[p][c])
            smax.append(mx)
        m = []
        for t in range(dec_seq):
            p = 0 if shared else t
            m_new = lane_max(s_new[p], tok[t] if shared else None)
            m.append(jnp.maximum(lane_max(smax[p], res[t]), m_new))
        if shared:
            m_cache = [sum(jnp.where(res[t], m[t], 0.0) for t in range(dec_seq))]
            m_tok = [sum(jnp.where(tok[t], m[t], 0.0) for t in range(dec_seq))]
        else:
            m_cache, m_tok = m, m
        p_new = [jnp.exp(s_new[p] - m_tok[p]) for p in range(n_pass)]

        acc = [jnp.zeros((D_ATT, LANES), F32) for _ in range(n_pass)]
        psum = [jnp.zeros((N_HEADS, LANES), F32) for _ in range(n_pass)]
        rolled = []
        for c in range(nch):
            vc = vt_ref[i, :, c * LANES:(c + 1) * LANES]
            rolled.append(pltpu.roll(vc, keep, 1))
            for p in range(n_pass):
                pc = jnp.exp(s[p][c] - m_cache[p])
                psum[p] = psum[p] + pc
                acc[p] = acc[p] + head_bcast(pc) * vc
        shifted_copy(rolled, pltpu.roll(vnr, keep, 1), svt_ref)

        o4 = jnp.zeros((D_ATT, LANES), F32)
        l4 = jnp.ones((N_HEADS, LANES), F32)
        lse4 = jnp.zeros((N_HEADS, LANES), F32)
        for t in range(dec_seq):
            p = 0 if shared else t
            o_t = lane_sum(acc[p], res[t]) + lane_sum(head_bcast(p_new[p]) * vnr, tok[t] if shared else None)
            l_t = lane_sum(psum[p], res[t]) + lane_sum(p_new[p], tok[t] if shared else None)
            o4 = jnp.where(tok[t], o_t, o4)
            l4 = jnp.where(tok[t], l_t, l4)
            lse4 = jnp.where(tok[t], m[t] + jnp.log(l_t), lse4)
        o4 = o4 * head_bcast(1.0 / l4)
        mine = (lane >= slot) & (lane < slot + dec_seq)
        ot_ref[0] = jnp.where(mine, pltpu.roll(o4, slot, 1), ot_ref[0])
        lse_ref[0] = jnp.where(mine, pltpu.roll(lse4, slot, 1), lse_ref[0])
        return carry

    lax.fori_loop(0, bb, one_batch, 0)


def _decode_attention(qp, knp, vnp, kt, vt, bdec, bnew, *, dil, dec_seq):
    db, _, width = kt.shape
    pack = DEC_PACK // dec_seq
    assert db % pack == 0 and width % LANES == 0 and (dil == 1 or dil >= dec_seq)
    bb = max(1, min(pack, DEC_BLOCK_BYTES // (D_ATT * width * 4)))
    assert pack % bb == 0
    per_b = lambda s: (s, 0, 0)
    packed = lambda s: (s * bb // pack, 0, 0)
    n_pass = bdec.shape[0]
    return pl.pallas_call(
        functools.partial(_dec_kernel, width=width, dil=dil, dec_seq=dec_seq, bb=bb),
        grid=(db // bb,),
        in_specs=[pl.BlockSpec((1, D_ATT, DEC_PACK), packed),
                  pl.BlockSpec((1, D_ATT, DEC_PACK), packed),
                  pl.BlockSpec((1, D_ATT, DEC_PACK), packed),
                  pl.BlockSpec((bb, D_ATT, width), per_b),
                  pl.BlockSpec((bb, D_ATT, width), per_b),
                  pl.BlockSpec((n_pass, N_HEADS, width), lambda s: (0, 0, 0)),
                  pl.BlockSpec((n_pass, N_HEADS, LANES), lambda s: (0, 0, 0))],
        out_specs=[pl.BlockSpec((bb, D_ATT, width), per_b),
                   pl.BlockSpec((bb, D_ATT, width), per_b),
                   pl.BlockSpec((1, D_ATT, DEC_PACK), packed),
                   pl.BlockSpec((1, N_HEADS, DEC_PACK), packed)],
        out_shape=[jax.ShapeDtypeStruct((db, D_ATT, width), F32),
                   jax.ShapeDtypeStruct((db, D_ATT, width), F32),
                   jax.ShapeDtypeStruct((db // pack, D_ATT, DEC_PACK), F32),
                   jax.ShapeDtypeStruct((db // pack, N_HEADS, DEC_PACK), F32)],
        compiler_params=pltpu.CompilerParams(dimension_semantics=("arbitrary",),
                                             vmem_limit_bytes=VMEM_LIMIT_BYTES),
        name=f"decode_attn_w{width}",
    )(qp, knp, vnp, kt, vt, bdec, bnew)


def _out_kernel(*refs, tm, dils):
    x_ref, cm_ref, sga_ref = refs[:3]
    o_refs = refs[3:3 + N_GROUPS]
    l_refs = refs[3 + N_GROUPS:3 + 2 * N_GROUPS]
    (g1_ref, sh2_ref, sc2_ref, g2_ref, n2_ref, wao_ref, wo_ref, w1_ref, w2_ref,
     y_ref, ostg, lstg) = refs[3 + 2 * N_GROUPS:]

    def token_order(ref, stg, dil):
        if dil == 1:
            return ref[0, 0].astype(F32)
        n_slab = ref.shape[-1] // LANES
        for r in range(dil):
            blk = ref[0, r].astype(F32)
            for i in range(n_slab):
                stg[i, pl.ds(r, tm // dil, stride=dil), :] = blk[:, i * LANES:(i + 1) * LANES]
        return jnp.concatenate([stg[i] for i in range(n_slab)], axis=1)

    lses = [token_order(l_refs[g], lstg, dils[g]) for g in range(N_GROUPS)]
    mx = jnp.maximum(jnp.maximum(lses[0], lses[1]), lses[2])
    es = [jnp.exp(l - mx) for l in lses]
    inv = 1.0 / (es[0] + es[1] + es[2])
    er = lax.broadcasted_iota(jnp.int32, (LANES, D_ATT), 0)
    ec = lax.broadcasted_iota(jnp.int32, (LANES, D_ATT), 1) // HEAD_DIM
    expand = jnp.where(er == ec, 1.0, 0.0).astype(BF16)
    o = None
    for g in range(N_GROUPS):
        w = es[g] * inv
        w_hi = w.astype(BF16)
        w_lo = (w - w_hi.astype(F32)).astype(BF16)
        wexp = _mm(w_hi, expand) + _mm(w_lo, expand)
        term = wexp * token_order(o_refs[g], ostg, dils[g])
        o = term if o is None else o + term
    attn_out = _mm(o.astype(BF16), wao_ref[...])
    mixed = cm_ref[0].astype(F32) + sga_ref[0].astype(F32) * attn_out
    x1 = x_ref[0] + g1_ref[0] * _mm(mixed.astype(BF16), wo_ref[...])
    ms = jnp.mean(x1 * x1, axis=-1, keepdims=True)
    xn2 = x1 * lax.rsqrt(ms + RMS_EPS) * n2_ref[...]
    xn2 = (xn2 * (1.0 + sc2_ref[0]) + sh2_ref[0]).astype(BF16)
    acc = None
    for f in range(D_FF // FF_CHUNK):
        cols = slice(f * FF_CHUNK, (f + 1) * FF_CHUNK)
        hid = jnp.maximum(_mm(xn2, w1_ref[:, cols]), 0.0)
        part = _mm((hid * hid).astype(BF16), w2_ref[cols, :])
        acc = part if acc is None else acc + part
    y_ref[0] = x1 + g2_ref[0] * acc


def _out_proj(x, cm, sga, os, lses, gate1, shift2, scale2, gate2, n2, wao, wo, w1, w2, *, tm):
    nb, ns, d = x.shape
    dils = tuple(o.shape[1] for o in os)
    assert ns % tm == 0 and all(tm % (dil * 2 * SUBLANES) == 0 for dil in dils)
    nmod = gate1.shape[1]
    tmod = 1 if nmod == 1 else tm
    row = lambda b, s: (b, s, 0)
    mod_map = (lambda b, s: (b, 0, 0)) if nmod == 1 else row
    mod_spec = pl.BlockSpec((1, tmod, d), mod_map)
    res_spec = lambda dil, n: pl.BlockSpec((1, dil, tm // dil, n), lambda b, s: (b, 0, s, 0))
    in_specs = ([pl.BlockSpec((1, tm, d), row)] * 3
                + [res_spec(dil, D_ATT) for dil in dils]
                + [res_spec(dil, LANES) for dil in dils]
                + [mod_spec] * 4
                + [_const_spec((1, d)), _const_spec((D_ATT, d)), _const_spec((d, d)),
                   _const_spec((d, D_FF)), _const_spec((D_FF, d))])
    return pl.pallas_call(
        functools.partial(_out_kernel, tm=tm, dils=dils),
        grid=(nb, ns // tm),
        in_specs=in_specs,
        out_specs=pl.BlockSpec((1, tm, d), row),
        out_shape=jax.ShapeDtypeStruct((nb, ns, d), F32),
        scratch_shapes=[pltpu.VMEM((D_ATT // LANES, tm, LANES), F32), pltpu.VMEM((1, tm, LANES), F32)],
        compiler_params=pltpu.CompilerParams(dimension_semantics=("arbitrary", "arbitrary"),
                                             vmem_limit_bytes=VMEM_LIMIT_BYTES),
        name="out_proj",
    )(x, cm, sga, *os, *lses, gate1, shift2, scale2, gate2, n2, wao, wo, w1, w2)


def _t5_causal_bucket(dist):
    max_exact = N_BUCKETS // 2
    ratio = jnp.maximum(dist, 1).astype(F32) / max_exact
    large = max_exact + (jnp.log(ratio) / math.log(MAX_DISTANCE / max_exact)
                         * (N_BUCKETS - max_exact)).astype(jnp.int32)
    large = jnp.minimum(large, N_BUCKETS - 1)
    return jnp.where(dist < max_exact, dist, large)


def _group_bias(rel_bias, g):
    steps = jnp.arange(WK + 1)
    bucket = _t5_causal_bucket(steps * DILS[g])
    onehot = (bucket[:, None] == jnp.arange(N_BUCKETS)[None, :]).astype(F32)
    b = jnp.dot(onehot, rel_bias[:, g * N_HEADS:(g + 1) * N_HEADS].astype(F32), precision=lax.Precision.HIGHEST)
    return b.T


def _prompt_bias_table(bias):
    blk, period = ATT_BLOCK, 4 * ATT_BLOCK
    rev = bias[:, ::-1]
    row = jnp.concatenate([rev, jnp.full((N_HEADS, period - (WK + 1)), NEG, F32)], axis=1)
    tiled = jnp.tile(row, (1, blk))[:, :blk * (period - 1)].reshape(N_HEADS, blk, period - 1)
    return tiled[:, :, :2 * blk].reshape(N_HEADS * blk, 2 * blk)


def _decode_bias_tables(bias, width, dil, dec_seq):
    assert width == WK * dil
    rev = bias[:, ::-1]
    if dil >= dec_seq:
        seen = (np.arange(dil) < dec_seq)[None, None, :]
        bdec = jnp.where(seen, rev[:, :WK, None], NEG).reshape(1, N_HEADS, width)
        bnew = jnp.where(np.arange(LANES)[None, :] < dec_seq, bias[:, 0:1], NEG)[None]
        return bdec, bnew
    assert dil == 1
    bdec, bnew = [], []
    for t in range(dec_seq):
        bdec.append(jnp.pad(rev[:, :width - t], ((0, 0), (t, 0)), constant_values=NEG))
        bnew.append(jnp.pad(bias[:, :t + 1][:, ::-1], ((0, 0), (0, LANES - t - 1)), constant_values=NEG))
    return jnp.stack(bdec), jnp.stack(bnew)


def _layer(x_prompt, x_sample, c_prompt, c_sample, state_conv, caches, rel_bias, norm1_g, norm2_g,
           w_ada, b_ada, w_in, conv_w, q_norm_g, k_norm_g, w_conv_out, w_attn_out, w_o, w_mlp_in,
           w_mlp_out):
    nbatch, ns, d = x_prompt.shape
    db, dec_seq, _ = x_sample.shape
    ntok = db * dec_seq

    w_in_b = w_in.astype(BF16)
    wco_b = w_conv_out.astype(BF16)
    wao_b = w_attn_out.astype(BF16)
    wo_b = w_o.astype(BF16)
    w1_b = w_mlp_in.astype(BF16)
    w2_b = w_mlp_out.astype(BF16)
    g1 = norm1_g.reshape(1, d)
    g2n = norm2_g.reshape(1, d)
    qg = jnp.tile(q_norm_g.reshape(1, HEAD_DIM), (1, MXU_DIM // HEAD_DIM))
    kg = jnp.tile(k_norm_g.reshape(1, HEAD_DIM), (1, MXU_DIM // HEAD_DIM))

    n_c = nbatch + db
    n_pad = -(-n_c // SUBLANES) * SUBLANES
    c_all = jnp.concatenate([c_prompt, c_sample, jnp.zeros((n_pad - n_c, d), F32)], axis=0)
    mod = _ada(c_all, w_ada, b_ada)
    mod_p = mod[:nbatch].reshape(nbatch, 1, N_MOD, d)
    mod_s = jnp.broadcast_to(mod[nbatch:n_c].reshape(db, 1, N_MOD, d), (db, dec_seq, N_MOD, d))
    mod_s = mod_s.reshape(1, ntok, N_MOD, d)
    mp = [mod_p[:, :, i] for i in range(N_MOD)]
    msn = [mod_s[:, :, i] for i in range(N_MOD)]

    biases = [_group_bias(rel_bias, g) for g in range(N_GROUPS)]

    cm, sga, qkv, utail = _in_proj(x_prompt, mp[0], mp[1], g1, w_in_b, conv_w, wco_b, qg, kg,
                                   tm=TM_IN, dils=DILS)
    os, lses = [], []
    for g in range(N_GROUPS):
        o, lse = _prompt_attention(qkv[3 * g], qkv[3 * g + 1], qkv[3 * g + 2], _prompt_bias_table(biases[g]))
        os.append(o)
        lses.append(lse)
    y_prompt = _out_proj(x_prompt, cm, sga, os, lses, mp[2], mp[3], mp[4], mp[5], g2n,
                         wao_b, wo_b, w1_b, w2_b, tm=TM_OUT)
    p_conv = utail[:, SUBLANES - (CONV_WIDTH - 1):, :]
    p_kv = []
    for g, (window, dil) in enumerate(DILATED_GROUPS):
        keep = min(window, ns)
        assert keep % dil == 0
        for part in (1, 2):
            tail = qkv[3 * g + part][:, :, (ns - keep) // dil:, :]
            p_kv.append(tail.transpose(0, 2, 1, 3).astype(F32).reshape(nbatch, keep, N_HEADS, HEAD_DIM))

    xs = x_sample.reshape(1, ntok, d)
    zero_row = jnp.zeros((db, 1, D_CONV), F32)
    hist1 = jnp.concatenate([state_conv[:, 1:2]] + [zero_row] * (dec_seq - 1), axis=1).reshape(1, ntok, D_CONV)
    hist2 = jnp.concatenate([state_conv[:, 0:1], state_conv[:, 1:2]] + [zero_row] * (dec_seq - 2),
                            axis=1).reshape(1, ntok, D_CONV)
    cm_s, sga_s, qkv_s, u_s = _in_proj(xs, msn[0], msn[1], g1, w_in_b, conv_w, wco_b, qg, kg, tm=ntok,
                                       dils=(1,) * N_GROUPS, hist1=hist1, hist2=hist2, dec_seq=dec_seq)
    s_conv = u_s.reshape(db, dec_seq, D_CONV)[:, dec_seq - (CONV_WIDTH - 1):]

    def pack_cols(a):
        return a.reshape(ntok // DEC_PACK, DEC_PACK, D_ATT).transpose(0, 2, 1)

    def unpack_cols(a):
        return a.transpose(0, 2, 1).reshape(1, 1, ntok, a.shape[1])

    os_s, lses_s, s_kv = [], [], []
    for g, (window, dil) in enumerate(DILATED_GROUPS):
        ck, cv = caches[2 * g], caches[2 * g + 1]
        width = ck.shape[1]
        assert width == window
        kt = ck.transpose(0, 2, 3, 1).reshape(db, D_ATT, width)
        vt = cv.transpose(0, 2, 3, 1).reshape(db, D_ATT, width)
        bdec, bnew = _decode_bias_tables(biases[g], width, dil, dec_seq)
        skt, svt, ot, lset = _decode_attention(pack_cols(qkv_s[3 * g]), pack_cols(qkv_s[3 * g + 1]),
                                               pack_cols(qkv_s[3 * g + 2]), kt, vt, bdec, bnew,
                                               dil=dil, dec_seq=dec_seq)
        s_kv.append(skt.reshape(db, N_HEADS, HEAD_DIM, width).transpose(0, 3, 1, 2))
        s_kv.append(svt.reshape(db, N_HEADS, HEAD_DIM, width).transpose(0, 3, 1, 2))
        os_s.append(unpack_cols(ot).astype(BF16))
        lses_s.append(jnp.pad(unpack_cols(lset), ((0, 0), (0, 0), (0, 0), (0, LANES - N_HEADS))))
    y_sample = _out_proj(xs, cm_s, sga_s, os_s, lses_s, msn[2], msn[3], msn[4], msn[5], g2n,
                         wao_b, wo_b, w1_b, w2_b, tm=ntok).reshape(db, dec_seq, d)
    return y_prompt, y_sample, [p_conv] + p_kv, [s_conv] + s_kv


def kernel(x_prompt, x_sample, c_prompt, c_sample, state_conv, cache_k1, cache_v1, cache_k2, cache_v2,
           cache_k3, cache_v3, rel_bias, norm1_g, norm2_g, w_ada, b_ada, w_in, conv_w, q_norm_g, k_norm_g,
           w_conv_out, w_attn_out, w_o, w_mlp_in, w_mlp_out):
    depth = w_in.shape[0]
    caches = (cache_k1, cache_v1, cache_k2, cache_v2, cache_k3, cache_v3)
    yp, ys = x_prompt, x_sample
    p_states = [[] for _ in range(1 + 2 * N_GROUPS)]
    s_states = [[] for _ in range(1 + 2 * N_GROUPS)]
    for l in range(depth):
        yp, ys, p_new, s_new = _layer(
            yp, ys, c_prompt, c_sample, state_conv[l], [c[l] for c in caches], rel_bias,
            norm1_g[l], norm2_g[l], w_ada[l], b_ada[l], w_in[l], conv_w[l], q_norm_g[l], k_norm_g[l],
            w_conv_out[l], w_attn_out[l], w_o[l], w_mlp_in[l], w_mlp_out[l])
        for lst, a in zip(p_states, p_new):
            lst.append(a)
        for lst, a in zip(s_states, s_new):
            lst.append(a)
    p_out = [jnp.stack(a) for a in p_states]
    s_out = [jnp.stack(a) for a in s_states]
    return (yp, ys, *p_out, *s_out)
```

```python
import functools
import math

import numpy as np
import jax
import jax.numpy as jnp
from jax import lax
from jax.experimental import pallas as pl
from jax.experimental.pallas import tpu as pltpu

F32 = jnp.float32
BF16 = jnp.bfloat16

D_MODEL = 1024
D_CONV = D_MODEL
CONV_WIDTH = 3
HEAD_DIM = 64
N_HEADS = 8
D_ATT = N_HEADS * HEAD_DIM
DILATED_GROUPS = ((128, 1), (512, 4), (2048, 16))
N_GROUPS = len(DILATED_GROUPS)
DILS = tuple(d for _, d in DILATED_GROUPS)
D_QKV = N_GROUPS * D_ATT
D_FF = 4 * D_MODEL
N_BUCKETS = 32
MAX_DISTANCE = 2048
ATT_BLOCK = 128
WK = 128
N_MOD = 6
RMS_EPS = 1e-6
ATT_SCALE = HEAD_DIM ** -0.5
NEG = -1e30

OFF_H, OFF_B, OFF_C = 0, D_CONV, 2 * D_CONV
OFF_Q = 3 * D_CONV
OFF_K = OFF_Q + D_QKV
OFF_V = OFF_K + D_QKV
OFF_GC = OFF_V + D_QKV
OFF_GA = OFF_GC + D_MODEL
D_PROJ = OFF_GA + D_MODEL

LANES = 128
SUBLANES = 8
MXU_DIM = 256
VMEM_LIMIT_BYTES = 56 * 1024 * 1024

TM_IN = 512
TM_OUT = 512
ATT_NB = 4
FF_CHUNK = 512
DEC_PACK = LANES
DEC_BLOCK_BYTES = 4 * 1024 * 1024

assert all(w // d == WK for w, d in DILATED_GROUPS)


def _mm(a, b):
    return jnp.dot(a, b, preferred_element_type=F32)


def _const_spec(shape):
    nd = len(shape)
    return pl.BlockSpec(shape, lambda *_: (0,) * nd, pipeline_mode=pl.Buffered(1))


def _ada_kernel(c_ref, w_ref, b_ref, o_ref):
    c = c_ref[...]
    s = c * jax.nn.sigmoid(c)
    o_ref[...] = _mm(s.astype(BF16), w_ref[...].astype(BF16)) + b_ref[...]


def _ada(c_all, w_ada, b_ada):
    n, d = c_all.shape
    nout = w_ada.shape[1]
    tn = 1024
    return pl.pallas_call(
        _ada_kernel,
        grid=(nout // tn,),
        in_specs=[pl.BlockSpec((n, d), lambda j: (0, 0)),
                  pl.BlockSpec((d, tn), lambda j: (0, j)),
                  pl.BlockSpec((1, tn), lambda j: (0, j))],
        out_specs=pl.BlockSpec((n, tn), lambda j: (0, j)),
        out_shape=jax.ShapeDtypeStruct((n, nout), F32),
        compiler_params=pltpu.CompilerParams(dimension_semantics=("arbitrary",),
                                             vmem_limit_bytes=VMEM_LIMIT_BYTES),
        name="ada",
    )(c_all, w_ada, b_ada.reshape(1, nout))


def _in_kernel(*refs, decode, tm, dec_seq, dils):
    n_in = 11 if decode else 9
    x_ref, sh_ref, sc_ref, g1_ref, win_ref, cw_ref, wco_ref, qg_ref, kg_ref = refs[:9]
    if decode:
        s1_ref, s2_ref = refs[9:11]
    cm_ref, sga_ref = refs[n_in:n_in + 2]
    qkv_refs = refs[n_in + 2:n_in + 2 + 3 * N_GROUPS]
    ustate_ref, uext, stg = refs[n_in + 2 + 3 * N_GROUPS:]

    x = x_ref[0]
    ms = jnp.mean(x * x, axis=-1, keepdims=True)
    xn = x * lax.rsqrt(ms + RMS_EPS) * g1_ref[...]
    xn = (xn * (1.0 + sc_ref[0]) + sh_ref[0]).astype(BF16)

    h = _mm(xn, win_ref[:, OFF_H:OFF_H + D_CONV])
    c = _mm(xn, win_ref[:, OFF_C:OFF_C + D_CONV])
    u = c * h
    hist = SUBLANES
    if decode:
        uext[0:hist, :] = jnp.zeros((hist, D_CONV), F32)
    else:
        @pl.when(pl.program_id(1) == 0)
        def _():
            uext[0:hist, :] = jnp.zeros((hist, D_CONV), F32)
    uext[hist:hist + tm, :] = u
    um1 = uext[hist - 1:hist - 1 + tm, :]
    um2 = uext[hist - 2:hist - 2 + tm, :]
    if decode:
        t = lax.broadcasted_iota(jnp.int32, (tm, 1), 0) % dec_seq
        um1 = jnp.where(t >= 1, um1, s1_ref[0])
        um2 = jnp.where(t >= 2, um2, s2_ref[0])
        ustate_ref[0] = u
    else:
        uext[0:hist, :] = uext[tm:tm + hist, :]
        ustate_ref[0] = u[tm - hist:tm, :]
    y = cw_ref[0:1, :] * um2 + cw_ref[1:2, :] * um1 + cw_ref[2:3, :] * u
    bg = _mm(xn, win_ref[:, OFF_B:OFF_B + D_CONV])
    conv_out = _mm((bg * y).astype(BF16), wco_ref[...])
    gc = _mm(xn, win_ref[:, OFF_GC:OFF_GC + D_MODEL])
    cm_ref[0] = (jax.nn.sigmoid(gc) * conv_out).astype(cm_ref.dtype)
    ga = _mm(xn, win_ref[:, OFF_GA:OFF_GA + D_MODEL])
    sga_ref[0] = jax.nn.sigmoid(ga).astype(sga_ref.dtype)

    seg_r = lax.broadcasted_iota(jnp.int32, (MXU_DIM, MXU_DIM), 0) // HEAD_DIM
    seg_c = lax.broadcasted_iota(jnp.int32, (MXU_DIM, MXU_DIM), 1) // HEAD_DIM
    seg = jnp.where(seg_r == seg_c, 1.0, 0.0).astype(BF16)

    def put(out_ref, val, dil, part):
        if dil == 1:
            out_ref[0, 0] = val.astype(out_ref.dtype)
            return
        for i in range(D_ATT // LANES):
            stg[part, i] = val[:, i * LANES:(i + 1) * LANES]
        for r in range(dil):
            for i in range(D_ATT // LANES):
                out_ref[0, r, :, i * LANES:(i + 1) * LANES] = (
                    stg[part, i, pl.ds(r, tm // dil, stride=dil), :].astype(out_ref.dtype))

    def head_norm(a, gain):
        sq = (a * a).astype(BF16)
        ss = jnp.concatenate([_mm(sq[:, i * MXU_DIM:(i + 1) * MXU_DIM], seg)
                              for i in range(D_ATT // MXU_DIM)], axis=1)
        return a * lax.rsqrt(ss * (1.0 / HEAD_DIM) + RMS_EPS) * gain

    for g in range(N_GROUPS):
        q_ref, k_ref, v_ref = qkv_refs[3 * g:3 * g + 3]
        lo = g * D_ATT
        q = _mm(xn, win_ref[:, OFF_Q + lo:OFF_Q + lo + D_ATT])
        put(q_ref, head_norm(q, qg_ref[...] * ATT_SCALE), dils[g], 0)
        k = _mm(xn, win_ref[:, OFF_K + lo:OFF_K + lo + D_ATT])
        put(k_ref, head_norm(k, kg_ref[...]), dils[g], 1)
        put(v_ref, _mm(xn, win_ref[:, OFF_V + lo:OFF_V + lo + D_ATT]), dils[g], 2)


def _in_proj(x, shift1, scale1, g1, w_in, conv_w, w_conv_out, qg, kg, *, tm, dils, hist1=None, hist2=None,
             dec_seq=1):
    nb, ns, d = x.shape
    decode = hist1 is not None
    assert ns % tm == 0 and all(tm % (dil * 2 * SUBLANES) == 0 for dil in dils)
    nmod = shift1.shape[1]
    tmod = 1 if nmod == 1 else tm
    row = lambda b, s: (b, s, 0)
    mod_map = (lambda b, s: (b, 0, 0)) if nmod == 1 else row
    act_dtype = F32 if decode else BF16
    in_specs = [pl.BlockSpec((1, tm, d), row),
                pl.BlockSpec((1, tmod, d), mod_map),
                pl.BlockSpec((1, tmod, d), mod_map),
                _const_spec((1, d)),
                _const_spec((d, D_PROJ)),
                _const_spec((CONV_WIDTH, D_CONV)),
                _const_spec((D_CONV, D_MODEL)),
                _const_spec((1, D_ATT)),
                _const_spec((1, D_ATT))]
    args = [x, shift1, scale1, g1, w_in, conv_w, w_conv_out, qg, kg]
    if decode:
        in_specs += [pl.BlockSpec((1, tm, D_CONV), row), pl.BlockSpec((1, tm, D_CONV), row)]
        args += [hist1, hist2]
    out_shape = [jax.ShapeDtypeStruct((nb, ns, D_MODEL), BF16),
                 jax.ShapeDtypeStruct((nb, ns, D_MODEL), BF16)]
    out_specs = [pl.BlockSpec((1, tm, D_MODEL), row),
                 pl.BlockSpec((1, tm, D_MODEL), row)]
    for dil in dils:
        for _ in range(3):
            out_shape.append(jax.ShapeDtypeStruct((nb, dil, ns // dil, D_ATT), act_dtype))
            out_specs.append(pl.BlockSpec((1, dil, tm // dil, D_ATT), lambda b, s: (b, 0, s, 0)))
    if decode:
        out_shape.append(jax.ShapeDtypeStruct((nb, ns, D_CONV), F32))
        out_specs.append(pl.BlockSpec((1, tm, D_CONV), row))
    else:
        out_shape.append(jax.ShapeDtypeStruct((nb, SUBLANES, D_CONV), F32))
        out_specs.append(pl.BlockSpec((1, SUBLANES, D_CONV), lambda b, s: (b, 0, 0)))
    outs = pl.pallas_call(
        functools.partial(_in_kernel, decode=decode, tm=tm, dec_seq=dec_seq, dils=tuple(dils)),
        grid=(nb, ns // tm),
        in_specs=in_specs,
        out_specs=out_specs,
        out_shape=out_shape,
        scratch_shapes=[pltpu.VMEM((tm + SUBLANES, D_CONV), F32),
                        pltpu.VMEM((3, D_ATT // LANES, tm, LANES), F32)],
        compiler_params=pltpu.CompilerParams(dimension_semantics=("arbitrary", "arbitrary"),
                                             vmem_limit_bytes=VMEM_LIMIT_BYTES),
        name="in_proj_decode" if decode else "in_proj",
    )(*args)
    return outs[0], outs[1], outs[2:2 + 3 * N_GROUPS], outs[-1]


HEADS_PER_PASS = MXU_DIM // HEAD_DIM


def _attn_kernel(q_ref, kc_ref, kp_ref, vc_ref, vp_ref, tb_ref, o_ref, lse_ref, *, nb):
    step = pl.program_id(2)
    blk = ATT_BLOCK
    lane = lax.broadcasted_iota(jnp.int32, (1, MXU_DIM), 1)
    hmask = [(lane >= HEAD_DIM * h) & (lane < HEAD_DIM * (h + 1)) for h in range(HEADS_PER_PASS)]
    col = lax.broadcasted_iota(jnp.int32, (1, 2 * blk), 1)
    lane_out = lax.broadcasted_iota(jnp.int32, (1, LANES), 1)
    for n in range(nb):
        rows = slice(n * blk, (n + 1) * blk)
        lse_acc = jnp.zeros((blk, LANES), F32)
        for hp in range(N_HEADS // HEADS_PER_PASS):
            cols = slice(hp * MXU_DIM, (hp + 1) * MXU_DIM)
            q4 = q_ref[0, 0, rows, cols]
            lhs = jnp.concatenate([jnp.where(hmask[h], q4, jnp.zeros_like(q4))
                                   for h in range(HEADS_PER_PASS)], axis=0)
            if n == 0:
                k_prev, v_prev = kp_ref[0, 0, :, cols], vp_ref[0, 0, :, cols]
            else:
                prev = slice((n - 1) * blk, n * blk)
                k_prev, v_prev = kc_ref[0, 0, prev, cols], vc_ref[0, 0, prev, cols]
            kk = jnp.concatenate([k_prev, kc_ref[0, 0, rows, cols]], axis=0)
            vv = jnp.concatenate([v_prev, vc_ref[0, 0, rows, cols]], axis=0)
            s = lax.dot_general(lhs, kk, (((1,), (1,)), ((), ())), preferred_element_type=F32)
            s = s + tb_ref[hp * HEADS_PER_PASS * blk:(hp + 1) * HEADS_PER_PASS * blk, :]
            if n == 0:
                s = jnp.where((step == 0) & (col < blk), NEG, s)
            m = jnp.max(s, axis=-1, keepdims=True)
            p = jnp.exp(s - m)
            l = jnp.sum(p, axis=-1, keepdims=True)
            pv = _mm(p.astype(BF16), vv) * (1.0 / l)
            lse = m + jnp.log(l)
            o4 = jnp.zeros((blk, MXU_DIM), F32)
            for h in range(HEADS_PER_PASS):
                hr = slice(h * blk, (h + 1) * blk)
                o4 = jnp.where(hmask[h], pv[hr, :], o4)
                lse_acc = jnp.where(lane_out == hp * HEADS_PER_PASS + h, lse[hr, :], lse_acc)
            o_ref[0, 0, rows, cols] = o4.astype(o_ref.dtype)
        lse_ref[0, 0, rows, :] = lse_acc


def _prompt_attention(q, k, v, tb):
    nbatch, dil, nl, _ = q.shape
    assert nl % ATT_BLOCK == 0
    nblk = nl // ATT_BLOCK
    nb = min(ATT_NB, nblk)
    assert nblk % nb == 0
    tq = nb * ATT_BLOCK
    cur = lambda b, r, i: (b, r, i, 0)
    prev = lambda b, r, i: (b, r, jnp.maximum(i * nb - 1, 0), 0)
    return pl.pallas_call(
        functools.partial(_attn_kernel, nb=nb),
        grid=(nbatch, dil, nblk // nb),
        in_specs=[pl.BlockSpec((1, 1, tq, D_ATT), cur),
                  pl.BlockSpec((1, 1, tq, D_ATT), cur),
                  pl.BlockSpec((1, 1, ATT_BLOCK, D_ATT), prev),
                  pl.BlockSpec((1, 1, tq, D_ATT), cur),
                  pl.BlockSpec((1, 1, ATT_BLOCK, D_ATT), prev),
                  pl.BlockSpec((N_HEADS * ATT_BLOCK, 2 * ATT_BLOCK), lambda b, r, i: (0, 0))],
        out_specs=[pl.BlockSpec((1, 1, tq, D_ATT), cur),
                   pl.BlockSpec((1, 1, tq, LANES), cur)],
        out_shape=[jax.ShapeDtypeStruct((nbatch, dil, nl, D_ATT), BF16),
                   jax.ShapeDtypeStruct((nbatch, dil, nl, LANES), F32)],
        compiler_params=pltpu.CompilerParams(dimension_semantics=("arbitrary",) * 3,
                                             vmem_limit_bytes=VMEM_LIMIT_BYTES),
        name=f"prompt_attn_d{dil}",
    )(q, k, k, v, v, tb)


def _dec_kernel(qp_ref, knp_ref, vnp_ref, kt_ref, vt_ref, bdec_ref, bnew_ref,
                skt_ref, svt_ref, ot_ref, lse_ref, *, width, dil, dec_seq, bb):
    step = pl.program_id(0)
    nch = width // LANES
    lane = lax.broadcasted_iota(jnp.int32, (1, LANES), 1)
    keep = LANES - dec_seq
    pack = DEC_PACK // dec_seq
    shared = dil >= dec_seq
    n_pass = 1 if shared else dec_seq

    @pl.when((step * bb) % pack == 0)
    def _():
        ot_ref[...] = jnp.zeros_like(ot_ref)
        lse_ref[...] = jnp.zeros_like(lse_ref)

    def head_sum(prod):
        return jnp.sum(prod.reshape(N_HEADS, HEAD_DIM, prod.shape[-1]), axis=1)

    def head_bcast(w):
        return jnp.broadcast_to(w[:, None, :], (N_HEADS, HEAD_DIM, w.shape[-1])).reshape(D_ATT, w.shape[-1])

    def lane_max(x, mask=None):
        return jnp.max(x if mask is None else jnp.where(mask, x, NEG), axis=-1, keepdims=True)

    def lane_sum(x, mask=None):
        return jnp.sum(x if mask is None else jnp.where(mask, x, 0.0), axis=-1, keepdims=True)

    def one_batch(i, carry):
        slot = ((step * bb + i) % pack) * dec_seq
        to_front = (LANES - slot) % LANES
        qr = pltpu.roll(qp_ref[0], to_front, 1)
        knr = pltpu.roll(knp_ref[0], to_front, 1)
        vnr = pltpu.roll(vnp_ref[0], to_front, 1)
        tok = [lane == t for t in range(dec_seq)]
        res = [lane % dil == t for t in range(dec_seq)] if shared else [None] * dec_seq
        qb = [jnp.broadcast_to(qr[:, t:t + 1], (D_ATT, LANES)) for t in range(dec_seq)]
        if shared:
            qsel = jnp.zeros((D_ATT, LANES), F32)
            for t in range(dec_seq):
                qsel = jnp.where(res[t], qb[t], qsel)
            q_pass = [qsel]
            s_new = [head_sum(knr * qr) + bnew_ref[0]]
        else:
            q_pass = qb
            s_new = [head_sum(knr * qb[t]) + bnew_ref[t] for t in range(dec_seq)]

        def shifted_copy(src_chunks_rolled, tail, dst):
            for c in range(nch):
                nxt = src_chunks_rolled[c + 1] if c + 1 < nch else tail
                dst[i, :, c * LANES:(c + 1) * LANES] = jnp.where(lane < keep, src_chunks_rolled[c], nxt)

        s = [[None] * nch for _ in range(n_pass)]
        rolled = []
        for c in range(nch):
            kc = kt_ref[i, :, c * LANES:(c + 1) * LANES]
            rolled.append(pltpu.roll(kc, keep, 1))
            for p in range(n_pass):
                s[p][c] = head_sum(kc * q_pass[p]) + bdec_ref[p, :, c * LANES:(c + 1) * LANES]
        shifted_copy(rolled, pltpu.roll(knr, keep, 1), skt_ref)

        smax = []
        for p in range(n_pass):
            mx = s[p][0]
            for c in range(1, nch):
                mx = jnp.maximum(mx, s[p][c])
            smax.append(mx)
        m = []
        for t in range(dec_seq):
            p = 0 if shared else t
            m_new = lane_max(s_new[p], tok[t] if shared else None)
            m.append(jnp.maximum(lane_max(smax[p], res[t]), m_new))
        if shared:
            m_cache = [sum(jnp.where(res[t], m[t], 0.0) for t in range(dec_seq))]
            m_tok = [sum(jnp.where(tok[t], m[t], 0.0) for t in range(dec_seq))]
        else:
            m_cache, m_tok = m, m
        p_new = [jnp.exp(s_new[p] - m_tok[p]) for p in range(n_pass)]

        acc = [jnp.zeros((D_ATT, LANES), F32) for _ in range(n_pass)]
        psum = [jnp.zeros((N_HEADS, LANES), F32) for _ in range(n_pass)]
        rolled = []
        for c in range(nch):
            vc = vt_ref[i, :, c * LANES:(c + 1) * LANES]
            rolled.append(pltpu.roll(vc, keep, 1))
            for p in range(n_pass):
                pc = jnp.exp(s[p][c] - m_cache[p])
                psum[p] = psum[p] + pc
                acc[p] = acc[p] + head_bcast(pc) * vc
        shifted_copy(rolled, pltpu.roll(vnr, keep, 1), svt_ref)

        o4 = jnp.zeros((D_ATT, LANES), F32)
        l4 = jnp.ones((N_HEADS, LANES), F32)
        lse4 = jnp.zeros((N_HEADS, LANES), F32)
        for t in range(dec_seq):
            p = 0 if shared else t
            o_t = lane_sum(acc[p], res[t]) + lane_sum(head_bcast(p_new[p]) * vnr, tok[t] if shared else None)
            l_t = lane_sum(psum[p], res[t]) + lane_sum(p_new[p], tok[t] if shared else None)
            o4 = jnp.where(tok[t], o_t, o4)
            l4 = jnp.where(tok[t], l_t, l4)
            lse4 = jnp.where(tok[t], m[t] + jnp.log(l_t), lse4)
        o4 = o4 * head_bcast(1.0 / l4)
        mine = (lane >= slot) & (lane < slot + dec_seq)
        ot_ref[0] = jnp.where(mine, pltpu.roll(o4, slot, 1), ot_ref[0])
        lse_ref[0] = jnp.where(mine, pltpu.roll(lse4, slot, 1), lse_ref[0])
        return carry

    lax.fori_loop(0, bb, one_batch, 0)


def _decode_attention(qp, knp, vnp, kt, vt, bdec, bnew, *, dil, dec_seq):
    db, _, width = kt.shape
    pack = DEC_PACK // dec_seq
    assert db % pack == 0 and width % LANES == 0 and (dil == 1 or dil >= dec_seq)
    bb = max(1, min(pack, DEC_BLOCK_BYTES // (D_ATT * width * 4)))
    assert pack % bb == 0
    per_b = lambda s: (s, 0, 0)
    packed = lambda s: (s * bb // pack, 0, 0)
    n_pass = bdec.shape[0]
    return pl.pallas_call(
        functools.partial(_dec_kernel, width=width, dil=dil, dec_seq=dec_seq, bb=bb),
        grid=(db // bb,),
        in_specs=[pl.BlockSpec((1, D_ATT, DEC_PACK), packed),
                  pl.BlockSpec((1, D_ATT, DEC_PACK), packed),
                  pl.BlockSpec((1, D_ATT, DEC_PACK), packed),
                  pl.BlockSpec((bb, D_ATT, width), per_b),
                  pl.BlockSpec((bb, D_ATT, width), per_b),
                  pl.BlockSpec((n_pass, N_HEADS, width), lambda s: (0, 0, 0)),
                  pl.BlockSpec((n_pass, N_HEADS, LANES), lambda s: (0, 0, 0))],
        out_specs=[pl.BlockSpec((bb, D_ATT, width), per_b),
                   pl.BlockSpec((bb, D_ATT, width), per_b),
                   pl.BlockSpec((1, D_ATT, DEC_PACK), packed),
                   pl.BlockSpec((1, N_HEADS, DEC_PACK), packed)],
        out_shape=[jax.ShapeDtypeStruct((db, D_ATT, width), F32),
                   jax.ShapeDtypeStruct((db, D_ATT, width), F32),
                   jax.ShapeDtypeStruct((db // pack, D_ATT, DEC_PACK), F32),
                   jax.ShapeDtypeStruct((db // pack, N_HEADS, DEC_PACK), F32)],
        compiler_params=pltpu.CompilerParams(dimension_semantics=("arbitrary",),
                                             vmem_limit_bytes=VMEM_LIMIT_BYTES),
        name=f"decode_attn_w{width}",
    )(qp, knp, vnp, kt, vt, bdec, bnew)


def _out_kernel(*refs, tm, dils):
    x_ref, cm_ref, sga_ref = refs[:3]
    o_refs = refs[3:3 + N_GROUPS]
    l_refs = refs[3 + N_GROUPS:3 + 2 * N_GROUPS]
    (g1_ref, sh2_ref, sc2_ref, g2_ref, n2_ref, wao_ref, wo_ref, w1_ref, w2_ref,
     y_ref, ostg, lstg) = refs[3 + 2 * N_GROUPS:]

    def token_order(ref, stg, dil):
        if dil == 1:
            return ref[0, 0].astype(F32)
        n_slab = ref.shape[-1] // LANES
        for r in range(dil):
            blk = ref[0, r].astype(F32)
            for i in range(n_slab):
                stg[i, pl.ds(r, tm // dil, stride=dil), :] = blk[:, i * LANES:(i + 1) * LANES]
        return jnp.concatenate([stg[i] for i in range(n_slab)], axis=1)

    lses = [token_order(l_refs[g], lstg, dils[g]) for g in range(N_GROUPS)]
    mx = jnp.maximum(jnp.maximum(lses[0], lses[1]), lses[2])
    es = [jnp.exp(l - mx) for l in lses]
    inv = 1.0 / (es[0] + es[1] + es[2])
    er = lax.broadcasted_iota(jnp.int32, (LANES, D_ATT), 0)
    ec = lax.broadcasted_iota(jnp.int32, (LANES, D_ATT), 1) // HEAD_DIM
    expand = jnp.where(er == ec, 1.0, 0.0).astype(BF16)
    o = None
    for g in range(N_GROUPS):
        w = es[g] * inv
        w_hi = w.astype(BF16)
        w_lo = (w - w_hi.astype(F32)).astype(BF16)
        wexp = _mm(w_hi, expand) + _mm(w_lo, expand)
        term = wexp * token_order(o_refs[g], ostg, dils[g])
        o = term if o is None else o + term
    attn_out = _mm(o.astype(BF16), wao_ref[...])
    mixed = cm_ref[0].astype(F32) + sga_ref[0].astype(F32) * attn_out
    x1 = x_ref[0] + g1_ref[0] * _mm(mixed.astype(BF16), wo_ref[...])
    ms = jnp.mean(x1 * x1, axis=-1, keepdims=True)
    xn2 = x1 * lax.rsqrt(ms + RMS_EPS) * n2_ref[...]
    xn2 = (xn2 * (1.0 + sc2_ref[0]) + sh2_ref[0]).astype(BF16)
    acc = None
    for f in range(D_FF // FF_CHUNK):
        cols = slice(f * FF_CHUNK, (f + 1) * FF_CHUNK)
        hid = jnp.maximum(_mm(xn2, w1_ref[:, cols]), 0.0)
        part = _mm((hid * hid).astype(BF16), w2_ref[cols, :])
        acc = part if acc is None else acc + part
    y_ref[0] = x1 + g2_ref[0] * acc


def _out_proj(x, cm, sga, os, lses, gate1, shift2, scale2, gate2, n2, wao, wo, w1, w2, *, tm):
    nb, ns, d = x.shape
    dils = tuple(o.shape[1] for o in os)
    assert ns % tm == 0 and all(tm % (dil * 2 * SUBLANES) == 0 for dil in dils)
    nmod = gate1.shape[1]
    tmod = 1 if nmod == 1 else tm
    row = lambda b, s: (b, s, 0)
    mod_map = (lambda b, s: (b, 0, 0)) if nmod == 1 else row
    mod_spec = pl.BlockSpec((1, tmod, d), mod_map)
    res_spec = lambda dil, n: pl.BlockSpec((1, dil, tm // dil, n), lambda b, s: (b, 0, s, 0))
    in_specs = ([pl.BlockSpec((1, tm, d), row)] * 3
                + [res_spec(dil, D_ATT) for dil in dils]
                + [res_spec(dil, LANES) for dil in dils]
                + [mod_spec] * 4
                + [_const_spec((1, d)), _const_spec((D_ATT, d)), _const_spec((d, d)),
                   _const_spec((d, D_FF)), _const_spec((D_FF, d))])
    return pl.pallas_call(
        functools.partial(_out_kernel, tm=tm, dils=dils),
        grid=(nb, ns // tm),
        in_specs=in_specs,
        out_specs=pl.BlockSpec((1, tm, d), row),
        out_shape=jax.ShapeDtypeStruct((nb, ns, d), F32),
        scratch_shapes=[pltpu.VMEM((D_ATT // LANES, tm, LANES), F32), pltpu.VMEM((1, tm, LANES), F32)],
        compiler_params=pltpu.CompilerParams(dimension_semantics=("arbitrary", "arbitrary"),
                                             vmem_limit_bytes=VMEM_LIMIT_BYTES),
        name="out_proj",
    )(x, cm, sga, *os, *lses, gate1, shift2, scale2, gate2, n2, wao, wo, w1, w2)


def _t5_causal_bucket(dist):
    max_exact = N_BUCKETS // 2
    ratio = jnp.maximum(dist, 1).astype(F32) / max_exact
    large = max_exact + (jnp.log(ratio) / math.log(MAX_DISTANCE / max_exact)
                         * (N_BUCKETS - max_exact)).astype(jnp.int32)
    large = jnp.minimum(large, N_BUCKETS - 1)
    return jnp.where(dist < max_exact, dist, large)


def _group_bias(rel_bias, g):
    steps = jnp.arange(WK + 1)
    bucket = _t5_causal_bucket(steps * DILS[g])
    onehot = (bucket[:, None] == jnp.arange(N_BUCKETS)[None, :]).astype(F32)
    b = jnp.dot(onehot, rel_bias[:, g * N_HEADS:(g + 1) * N_HEADS].astype(F32), precision=lax.Precision.HIGHEST)
    return b.T


def _prompt_bias_table(bias):
    blk, period = ATT_BLOCK, 4 * ATT_BLOCK
    rev = bias[:, ::-1]
    row = jnp.concatenate([rev, jnp.full((N_HEADS, period - (WK + 1)), NEG, F32)], axis=1)
    tiled = jnp.tile(row, (1, blk))[:, :blk * (period - 1)].reshape(N_HEADS, blk, period - 1)
    return tiled[:, :, :2 * blk].reshape(N_HEADS * blk, 2 * blk)


def _decode_bias_tables(bias, width, dil, dec_seq):
    assert width == WK * dil
    rev = bias[:, ::-1]
    if dil >= dec_seq:
        seen = (np.arange(dil) < dec_seq)[None, None, :]
        bdec = jnp.where(seen, rev[:, :WK, None], NEG).reshape(1, N_HEADS, width)
        bnew = jnp.where(np.arange(LANES)[None, :] < dec_seq, bias[:, 0:1], NEG)[None]
        return bdec, bnew
    assert dil == 1
    bdec, bnew = [], []
    for t in range(dec_seq):
        bdec.append(jnp.pad(rev[:, :width - t], ((0, 0), (t, 0)), constant_values=NEG))
        bnew.append(jnp.pad(bias[:, :t + 1][:, ::-1], ((0, 0), (0, LANES - t - 1)), constant_values=NEG))
    return jnp.stack(bdec), jnp.stack(bnew)


def _layer(x_prompt, x_sample, c_prompt, c_sample, state_conv, caches, rel_bias, norm1_g, norm2_g,
           w_ada, b_ada, w_in, conv_w, q_norm_g, k_norm_g, w_conv_out, w_attn_out, w_o, w_mlp_in,
           w_mlp_out):
    nbatch, ns, d = x_prompt.shape
    db, dec_seq, _ = x_sample.shape
    ntok = db * dec_seq

    w_in_b = w_in.astype(BF16)
    wco_b = w_conv_out.astype(BF16)
    wao_b = w_attn_out.astype(BF16)
    wo_b = w_o.astype(BF16)
    w1_b = w_mlp_in.astype(BF16)
    w2_b = w_mlp_out.astype(BF16)
    g1 = norm1_g.reshape(1, d)
    g2n = norm2_g.reshape(1, d)
    qg = jnp.tile(q_norm_g.reshape(1, HEAD_DIM), (1, N_HEADS))
    kg = jnp.tile(k_norm_g.reshape(1, HEAD_DIM), (1, N_HEADS))

    n_c = nbatch + db
    n_pad = -(-n_c // SUBLANES) * SUBLANES
    c_all = jnp.concatenate([c_prompt, c_sample, jnp.zeros((n_pad - n_c, d), F32)], axis=0)
    mod = _ada(c_all, w_ada, b_ada)
    mod_p = mod[:nbatch].reshape(nbatch, 1, N_MOD, d)
    mod_s = jnp.broadcast_to(mod[nbatch:n_c].reshape(db, 1, N_MOD, d), (db, dec_seq, N_MOD, d))
    mod_s = mod_s.reshape(1, ntok, N_MOD, d)
    mp = [mod_p[:, :, i] for i in range(N_MOD)]
    msn = [mod_s[:, :, i] for i in range(N_MOD)]

    biases = [_group_bias(rel_bias, g) for g in range(N_GROUPS)]

    cm, sga, qkv, utail = _in_proj(x_prompt, mp[0], mp[1], g1, w_in_b, conv_w, wco_b, qg, kg,
                                   tm=TM_IN, dils=DILS)
    os, lses = [], []
    for g in range(N_GROUPS):
        o, lse = _prompt_attention(qkv[3 * g], qkv[3 * g + 1], qkv[3 * g + 2], _prompt_bias_table(biases[g]))
        os.append(o)
        lses.append(lse)
    y_prompt = _out_proj(x_prompt, cm, sga, os, lses, mp[2], mp[3], mp[4], mp[5], g2n,
                         wao_b, wo_b, w1_b, w2_b, tm=TM_OUT)
    p_conv = utail[:, SUBLANES - (CONV_WIDTH - 1):, :]
    p_kv = []
    for g, (window, dil) in enumerate(DILATED_GROUPS):
        keep = min(window, ns)
        assert keep % dil == 0
        for part in (1, 2):
            tail = qkv[3 * g + part][:, :, (ns - keep) // dil:, :]
            p_kv.append(tail.transpose(0, 2, 1, 3).astype(F32).reshape(nbatch, keep, N_HEADS, HEAD_DIM))

    xs = x_sample.reshape(1, ntok, d)
    zero_row = jnp.zeros((db, 1, D_CONV), F32)
    hist1 = jnp.concatenate([state_conv[:, 1:2]] + [zero_row] * (dec_seq - 1), axis=1).reshape(1, ntok, D_CONV)
    hist2 = jnp.concatenate([state_conv[:, 0:1], state_conv[:, 1:2]] + [zero_row] * (dec_seq - 2),
                            axis=1).reshape(1, ntok, D_CONV)
    cm_s, sga_s, qkv_s, u_s = _in_proj(xs, msn[0], msn[1], g1, w_in_b, conv_w, wco_b, qg, kg, tm=ntok,
                                       dils=(1,) * N_GROUPS, hist1=hist1, hist2=hist2, dec_seq=dec_seq)
    s_conv = u_s.reshape(db, dec_seq, D_CONV)[:, dec_seq - (CONV_WIDTH - 1):]

    def pack_cols(a):
        return a.reshape(ntok // DEC_PACK, DEC_PACK, D_ATT).transpose(0, 2, 1)

    def unpack_cols(a):
        return a.transpose(0, 2, 1).reshape(1, 1, ntok, a.shape[1])

    os_s, lses_s, s_kv = [], [], []
    for g, (window, dil) in enumerate(DILATED_GROUPS):
        ck, cv = caches[2 * g], caches[2 * g + 1]
        width = ck.shape[1]
        assert width == window
        kt = ck.transpose(0, 2, 3, 1).reshape(db, D_ATT, width)
        vt = cv.transpose(0, 2, 3, 1).reshape(db, D_ATT, width)
        bdec, bnew = _decode_bias_tables(biases[g], width, dil, dec_seq)
        skt, svt, ot, lset = _decode_attention(pack_cols(qkv_s[3 * g]), pack_cols(qkv_s[3 * g + 1]),
                                               pack_cols(qkv_s[3 * g + 2]), kt, vt, bdec, bnew,
                                               dil=dil, dec_seq=dec_seq)
        s_kv.append(skt.reshape(db, N_HEADS, HEAD_DIM, width).transpose(0, 3, 1, 2))
        s_kv.append(svt.reshape(db, N_HEADS, HEAD_DIM, width).transpose(0, 3, 1, 2))
        os_s.append(unpack_cols(ot).astype(BF16))
        lses_s.append(jnp.pad(unpack_cols(lset), ((0, 0), (0, 0), (0, 0), (0, LANES - N_HEADS))))
    y_sample = _out_proj(xs, cm_s, sga_s, os_s, lses_s, msn[2], msn[3], msn[4], msn[5], g2n,
                         wao_b, wo_b, w1_b, w2_b, tm=ntok).reshape(db, dec_seq, d)
    return y_prompt, y_sample, [p_conv] + p_kv, [s_conv] + s_kv


def kernel(x_prompt, x_sample, c_prompt, c_sample, state_conv, cache_k1, cache_v1, cache_k2, cache_v2,
           cache_k3, cache_v3, rel_bias, norm1_g, norm2_g, w_ada, b_ada, w_in, conv_w, q_norm_g, k_norm_g,
           w_conv_out, w_attn_out, w_o, w_mlp_in, w_mlp_out):
    depth = w_in.shape[0]
    caches = (cache_k1, cache_v1, cache_k2, cache_v2, cache_k3, cache_v3)
    yp, ys = x_prompt, x_sample
    p_states = [[] for _ in range(1 + 2 * N_GROUPS)]
    s_states = [[] for _ in range(1 + 2 * N_GROUPS)]
    for l in range(depth):
        yp, ys, p_new, s_new = _layer(
            yp, ys, c_prompt, c_sample, state_conv[l], [c[l] for c in caches], rel_bias,
            norm1_g[l], norm2_g[l], w_ada[l], b_ada[l], w_in[l], conv_w[l], q_norm_g[l], k_norm_g[l],
            w_conv_out[l], w_attn_out[l], w_o[l], w_mlp_in[l], w_mlp_out[l])
        for lst, a in zip(p_states, p_new):
            lst.append(a)
        for lst, a in zip(s_states, s_new):
            lst.append(a)
    p_out = [jnp.stack(a) for a in p_states]
    s_out = [jnp.stack(a) for a in s_states]
    return (yp, ys, *p_out, *s_out)
```

```python
import functools
import math

import numpy as np
import jax
import jax.numpy as jnp
from jax import lax
from jax.experimental import pallas as pl
from jax.experimental.pallas import tpu as pltpu

F32 = jnp.float32
BF16 = jnp.bfloat16

D_MODEL = 1024
D_CONV = D_MODEL
CONV_WIDTH = 3
HEAD_DIM = 64
N_HEADS = 8
D_ATT = N_HEADS * HEAD_DIM
DILATED_GROUPS = ((128, 1), (512, 4), (2048, 16))
N_GROUPS = len(DILATED_GROUPS)
DILS = tuple(d for _, d in DILATED_GROUPS)
D_QKV = N_GROUPS * D_ATT
D_FF = 4 * D_MODEL
N_BUCKETS = 32
MAX_DISTANCE = 2048
ATT_BLOCK = 128
WK = 128
N_MOD = 6
RMS_EPS = 1e-6
ATT_SCALE = HEAD_DIM ** -0.5
NEG = -1e30

OFF_H, OFF_B, OFF_C = 0, D_CONV, 2 * D_CONV
OFF_Q = 3 * D_CONV
OFF_K = OFF_Q + D_QKV
OFF_V = OFF_K + D_QKV
OFF_GC = OFF_V + D_QKV
OFF_GA = OFF_GC + D_MODEL
D_PROJ = OFF_GA + D_MODEL

LANES = 128
SUBLANES = 8
MXU_DIM = 256
VMEM_LIMIT_BYTES = 56 * 1024 * 1024

TM_IN = 512
TM_OUT = 512
ATT_NB = 4
FF_CHUNK = 512
DEC_PACK = LANES
DEC_BLOCK_BYTES = 4 * 1024 * 1024

assert all(w // d == WK for w, d in DILATED_GROUPS)


def _mm(a, b):
    return jnp.dot(a, b, preferred_element_type=F32)


def _const_spec(shape):
    nd = len(shape)
    return pl.BlockSpec(shape, lambda *_: (0,) * nd, pipeline_mode=pl.Buffered(1))


def _ada_kernel(c_ref, w_ref, b_ref, o_ref):
    c = c_ref[...]
    s = c * jax.nn.sigmoid(c)
    o_ref[...] = _mm(s.astype(BF16), w_ref[...].astype(BF16)) + b_ref[...]


def _ada(c_all, w_ada, b_ada):
    n, d = c_all.shape
    nout = w_ada.shape[1]
    tn = 1024
    return pl.pallas_call(
        _ada_kernel,
        grid=(nout // tn,),
        in_specs=[pl.BlockSpec((n, d), lambda j: (0, 0)),
                  pl.BlockSpec((d, tn), lambda j: (0, j)),
                  pl.BlockSpec((1, tn), lambda j: (0, j))],
        out_specs=pl.BlockSpec((n, tn), lambda j: (0, j)),
        out_shape=jax.ShapeDtypeStruct((n, nout), F32),
        compiler_params=pltpu.CompilerParams(dimension_semantics=("arbitrary",),
                                             vmem_limit_bytes=VMEM_LIMIT_BYTES),
        name="ada",
    )(c_all, w_ada, b_ada.reshape(1, nout))


def _in_kernel(*refs, decode, tm, dec_seq, dils):
    n_in = 11 if decode else 9
    x_ref, sh_ref, sc_ref, g1_ref, win_ref, cw_ref, wco_ref, qg_ref, kg_ref = refs[:9]
    if decode:
        s1_ref, s2_ref = refs[9:11]
    cm_ref, sga_ref = refs[n_in:n_in + 2]
    qkv_refs = refs[n_in + 2:n_in + 2 + 3 * N_GROUPS]
    ustate_ref, uext, stg = refs[n_in + 2 + 3 * N_GROUPS:]

    x = x_ref[0]
    ms = jnp.mean(x * x, axis=-1, keepdims=True)
    xn = x * lax.rsqrt(ms + RMS_EPS) * g1_ref[...]
    xn = (xn * (1.0 + sc_ref[0]) + sh_ref[0]).astype(BF16)

    h = _mm(xn, win_ref[:, OFF_H:OFF_H + D_CONV])
    c = _mm(xn, win_ref[:, OFF_C:OFF_C + D_CONV])
    u = c * h
    hist = SUBLANES
    if decode:
        uext[0:hist, :] = jnp.zeros((hist, D_CONV), F32)
    else:
        @pl.when(pl.program_id(1) == 0)
        def _():
            uext[0:hist, :] = jnp.zeros((hist, D_CONV), F32)
    uext[hist:hist + tm, :] = u
    um1 = uext[hist - 1:hist - 1 + tm, :]
    um2 = uext[hist - 2:hist - 2 + tm, :]
    if decode:
        t = lax.broadcasted_iota(jnp.int32, (tm, 1), 0) % dec_seq
        um1 = jnp.where(t >= 1, um1, s1_ref[0])
        um2 = jnp.where(t >= 2, um2, s2_ref[0])
        ustate_ref[0] = u
    else:
        uext[0:hist, :] = uext[tm:tm + hist, :]
        ustate_ref[0] = u[tm - hist:tm, :]
    y = cw_ref[0:1, :] * um2 + cw_ref[1:2, :] * um1 + cw_ref[2:3, :] * u
    bg = _mm(xn, win_ref[:, OFF_B:OFF_B + D_CONV])
    conv_out = _mm((bg * y).astype(BF16), wco_ref[...])
    gc = _mm(xn, win_ref[:, OFF_GC:OFF_GC + D_MODEL])
    cm_ref[0] = (jax.nn.sigmoid(gc) * conv_out).astype(cm_ref.dtype)
    ga = _mm(xn, win_ref[:, OFF_GA:OFF_GA + D_MODEL])
    sga_ref[0] = jax.nn.sigmoid(ga).astype(sga_ref.dtype)

    seg_r = lax.broadcasted_iota(jnp.int32, (MXU_DIM, MXU_DIM), 0) // HEAD_DIM
    seg_c = lax.broadcasted_iota(jnp.int32, (MXU_DIM, MXU_DIM), 1) // HEAD_DIM
    seg = jnp.where(seg_r == seg_c, 1.0, 0.0).astype(BF16)

    def put(out_ref, val, dil, part):
        if dil == 1:
            out_ref[0, 0] = val.astype(out_ref.dtype)
            return
        for i in range(D_ATT // LANES):
            stg[part, i] = val[:, i * LANES:(i + 1) * LANES]
        for r in range(dil):
            for i in range(D_ATT // LANES):
                out_ref[0, r, :, i * LANES:(i + 1) * LANES] = (
                    stg[part, i, pl.ds(r, tm // dil, stride=dil), :].astype(out_ref.dtype))

    def head_norm(a, gain):
        sq = (a * a).astype(BF16)
        ss = jnp.concatenate([_mm(sq[:, i * MXU_DIM:(i + 1) * MXU_DIM], seg)
                              for i in range(D_ATT // MXU_DIM)], axis=1)
        return a * lax.rsqrt(ss * (1.0 / HEAD_DIM) + RMS_EPS) * gain

    for g in range(N_GROUPS):
        q_ref, k_ref, v_ref = qkv_refs[3 * g:3 * g + 3]
        lo = g * D_ATT
        q = _mm(xn, win_ref[:, OFF_Q + lo:OFF_Q + lo + D_ATT])
        put(q_ref, head_norm(q, qg_ref[...] * ATT_SCALE), dils[g], 0)
        k = _mm(xn, win_ref[:, OFF_K + lo:OFF_K + lo + D_ATT])
        put(k_ref, head_norm(k, kg_ref[...]), dils[g], 1)
        put(v_ref, _mm(xn, win_ref[:, OFF_V + lo:OFF_V + lo + D_ATT]), dils[g], 2)


def _in_proj(x, shift1, scale1, g1, w_in, conv_w, w_conv_out, qg, kg, *, tm, dils, hist1=None, hist2=None,
             dec_seq=1):
    nb, ns, d = x.shape
    decode = hist1 is not None
    assert ns % tm == 0 and all(tm % (dil * 2 * SUBLANES) == 0 for dil in dils)
    nmod = shift1.shape[1]
    tmod = 1 if nmod == 1 else tm
    row = lambda b, s: (b, s, 0)
    mod_map = (lambda b, s: (b, 0, 0)) if nmod == 1 else row
    act_dtype = F32 if decode else BF16
    in_specs = [pl.BlockSpec((1, tm, d), row),
                pl.BlockSpec((1, tmod, d), mod_map),
                pl.BlockSpec((1, tmod, d), mod_map),
                _const_spec((1, d)),
                _const_spec((d, D_PROJ)),
                _const_spec((CONV_WIDTH, D_CONV)),
                _const_spec((D_CONV, D_MODEL)),
                _const_spec((1, D_ATT)),
                _const_spec((1, D_ATT))]
    args = [x, shift1, scale1, g1, w_in, conv_w, w_conv_out, qg, kg]
    if decode:
        in_specs += [pl.BlockSpec((1, tm, D_CONV), row), pl.BlockSpec((1, tm, D_CONV), row)]
        args += [hist1, hist2]
    out_shape = [jax.ShapeDtypeStruct((nb, ns, D_MODEL), BF16),
                 jax.ShapeDtypeStruct((nb, ns, D_MODEL), BF16)]
    out_specs = [pl.BlockSpec((1, tm, D_MODEL), row),
                 pl.BlockSpec((1, tm, D_MODEL), row)]
    for dil in dils:
        for _ in range(3):
            out_shape.append(jax.ShapeDtypeStruct((nb, dil, ns // dil, D_ATT), act_dtype))
            out_specs.append(pl.BlockSpec((1, dil, tm // dil, D_ATT), lambda b, s: (b, 0, s, 0)))
    if decode:
        out_shape.append(jax.ShapeDtypeStruct((nb, ns, D_CONV), F32))
        out_specs.append(pl.BlockSpec((1, tm, D_CONV), row))
    else:
        out_shape.append(jax.ShapeDtypeStruct((nb, SUBLANES, D_CONV), F32))
        out_specs.append(pl.BlockSpec((1, SUBLANES, D_CONV), lambda b, s: (b, 0, 0)))
    outs = pl.pallas_call(
        functools.partial(_in_kernel, decode=decode, tm=tm, dec_seq=dec_seq, dils=tuple(dils)),
        grid=(nb, ns // tm),
        in_specs=in_specs,
        out_specs=out_specs,
        out_shape=out_shape,
        scratch_shapes=[pltpu.VMEM((tm + SUBLANES, D_CONV), F32),
                        pltpu.VMEM((3, D_ATT // LANES, tm, LANES), F32)],
        compiler_params=pltpu.CompilerParams(dimension_semantics=("arbitrary", "arbitrary"),
                                             vmem_limit_bytes=VMEM_LIMIT_BYTES),
        name="in_proj_decode" if decode else "in_proj",
    )(*args)
    return outs[0], outs[1], outs[2:2 + 3 * N_GROUPS], outs[-1]


HEADS_PER_PASS = MXU_DIM // HEAD_DIM


def _attn_kernel(q_ref, kc_ref, kp_ref, vc_ref, vp_ref, tb_ref, o_ref, lse_ref, *, nb):
    step = pl.program_id(2)
    blk = ATT_BLOCK
    lane = lax.broadcasted_iota(jnp.int32, (1, MXU_DIM), 1)
    hmask = [(lane >= HEAD_DIM * h) & (lane < HEAD_DIM * (h + 1)) for h in range(HEADS_PER_PASS)]
    col = lax.broadcasted_iota(jnp.int32, (1, 2 * blk), 1)
    lane_out = lax.broadcasted_iota(jnp.int32, (1, LANES), 1)
    for n in range(nb):
        rows = slice(n * blk, (n + 1) * blk)
        lse_acc = jnp.zeros((blk, LANES), F32)
        for hp in range(N_HEADS // HEADS_PER_PASS):
            cols = slice(hp * MXU_DIM, (hp + 1) * MXU_DIM)
            q4 = q_ref[0, 0, rows, cols]
            lhs = jnp.concatenate([jnp.where(hmask[h], q4, jnp.zeros_like(q4))
                                   for h in range(HEADS_PER_PASS)], axis=0)
            if n == 0:
                k_prev, v_prev = kp_ref[0, 0, :, cols], vp_ref[0, 0, :, cols]
            else:
                prev = slice((n - 1) * blk, n * blk)
                k_prev, v_prev = kc_ref[0, 0, prev, cols], vc_ref[0, 0, prev, cols]
            kk = jnp.concatenate([k_prev, kc_ref[0, 0, rows, cols]], axis=0)
            vv = jnp.concatenate([v_prev, vc_ref[0, 0, rows, cols]], axis=0)
            s = lax.dot_general(lhs, kk, (((1,), (1,)), ((), ())), preferred_element_type=F32)
            s = s + tb_ref[hp * HEADS_PER_PASS * blk:(hp + 1) * HEADS_PER_PASS * blk, :]
            if n == 0:
                s = jnp.where((step == 0) & (col < blk), NEG, s)
            m = jnp.max(s, axis=-1, keepdims=True)
            p = jnp.exp(s - m)
            l = jnp.sum(p, axis=-1, keepdims=True)
            pv = _mm(p.astype(BF16), vv) * (1.0 / l)
            lse = m + jnp.log(l)
            o4 = jnp.zeros((blk, MXU_DIM), F32)
            for h in range(HEADS_PER_PASS):
                hr = slice(h * blk, (h + 1) * blk)
                o4 = jnp.where(hmask[h], pv[hr, :], o4)
                lse_acc = jnp.where(lane_out == hp * HEADS_PER_PASS + h, lse[hr, :], lse_acc)
            o_ref[0, 0, rows, cols] = o4.astype(o_ref.dtype)
        lse_ref[0, 0, rows, :] = lse_acc


def _prompt_attention(q, k, v, tb):
    nbatch, dil, nl, _ = q.shape
    assert nl % ATT_BLOCK == 0
    nblk = nl // ATT_BLOCK
    nb = min(ATT_NB, nblk)
    assert nblk % nb == 0
    tq = nb * ATT_BLOCK
    cur = lambda b, r, i: (b, r, i, 0)
    prev = lambda b, r, i: (b, r, jnp.maximum(i * nb - 1, 0), 0)
    return pl.pallas_call(
        functools.partial(_attn_kernel, nb=nb),
        grid=(nbatch, dil, nblk // nb),
        in_specs=[pl.BlockSpec((1, 1, tq, D_ATT), cur),
                  pl.BlockSpec((1, 1, tq, D_ATT), cur),
                  pl.BlockSpec((1, 1, ATT_BLOCK, D_ATT), prev),
                  pl.BlockSpec((1, 1, tq, D_ATT), cur),
                  pl.BlockSpec((1, 1, ATT_BLOCK, D_ATT), prev),
                  pl.BlockSpec((N_HEADS * ATT_BLOCK, 2 * ATT_BLOCK), lambda b, r, i: (0, 0))],
        out_specs=[pl.BlockSpec((1, 1, tq, D_ATT), cur),
                   pl.BlockSpec((1, 1, tq, LANES), cur)],
        out_shape=[jax.ShapeDtypeStruct((nbatch, dil, nl, D_ATT), BF16),
                   jax.ShapeDtypeStruct((nbatch, dil, nl, LANES), F32)],
        compiler_params=pltpu.CompilerParams(dimension_semantics=("arbitrary",) * 3,
                                             vmem_limit_bytes=VMEM_LIMIT_BYTES),
        name=f"prompt_attn_d{dil}",
    )(q, k, k, v, v, tb)


def _decode_one(q, k_tail, v_tail, kt_ref, vt_ref, bias_ref, skt_ref, svt_ref, i, *, width, dec_seq):
    nch = width // LANES
    keep = LANES - dec_seq
    lane = lax.broadcasted_iota(jnp.int32, (1, LANES), 1)
    nrow = dec_seq * N_HEADS
    own = (lax.broadcasted_iota(jnp.int32, (nrow, D_ATT), 0) % N_HEADS
           == lax.broadcasted_iota(jnp.int32, (nrow, D_ATT), 1) // HEAD_DIM)
    q_rows = jnp.concatenate([jnp.broadcast_to(q[t:t + 1, :], (N_HEADS, D_ATT)) for t in range(dec_seq)], axis=0)
    q_bd = jnp.where(own, q_rows, 0.0).astype(BF16)

    kc = kt_ref[i]
    vc = vt_ref[i]
    s_c = _mm(q_bd, kc.astype(BF16)) + bias_ref[:, 0:width]
    s_t = _mm(q_bd, k_tail.astype(BF16)) + bias_ref[:, width:width + LANES]
    m = jnp.maximum(jnp.max(s_c, axis=-1, keepdims=True), jnp.max(s_t, axis=-1, keepdims=True))
    p_c = jnp.exp(s_c - m)
    p_t = jnp.exp(s_t - m)
    l = jnp.sum(p_c, axis=-1, keepdims=True) + jnp.sum(p_t, axis=-1, keepdims=True)
    nt = (((1,), (1,)), ((), ()))
    o = (lax.dot_general(p_c.astype(BF16), vc.astype(BF16), nt, preferred_element_type=F32)
         + lax.dot_general(p_t.astype(BF16), v_tail.astype(BF16), nt, preferred_element_type=F32))
    o = jnp.where(own, o, 0.0) * (1.0 / l)
    o_tok = jnp.sum(o.reshape(dec_seq, N_HEADS, D_ATT), axis=1)
    lse = m + jnp.log(l)
    head_lane = (lax.broadcasted_iota(jnp.int32, (nrow, LANES), 0) % N_HEADS
                 == lax.broadcasted_iota(jnp.int32, (nrow, LANES), 1))
    lse_tok = jnp.sum(jnp.where(head_lane, lse, 0.0).reshape(dec_seq, N_HEADS, LANES), axis=1)

    for src_ref, tail, dst_ref in ((kt_ref, k_tail, skt_ref), (vt_ref, v_tail, svt_ref)):
        cur = pltpu.roll(src_ref[i, :, 0:LANES], keep, 1)
        for c in range(nch):
            nxt = pltpu.roll(src_ref[i, :, (c + 1) * LANES:(c + 2) * LANES], keep, 1) if c + 1 < nch else tail
            dst_ref[i, :, c * LANES:(c + 1) * LANES] = jnp.where(lane < keep, cur, nxt)
            cur = nxt
    return o_tok, lse_tok


def _dec_kernel(q_ref, knp_ref, vnp_ref, kt_ref, vt_ref, bias_ref, skt_ref, svt_ref, o_ref, lse_ref,
                *, width, dec_seq, bb):
    step = pl.program_id(0)
    pack = DEC_PACK // dec_seq
    keep = LANES - dec_seq

    def one_batch(i, carry):
        slot = ((step * bb + i) % pack) * dec_seq
        to_tail = (keep + LANES - slot) % LANES
        k_tail = pltpu.roll(knp_ref[0], to_tail, 1)
        v_tail = pltpu.roll(vnp_ref[0], to_tail, 1)
        o_tok, lse_tok = _decode_one(q_ref[i], k_tail, v_tail, kt_ref, vt_ref, bias_ref, skt_ref, svt_ref, i,
                                     width=width, dec_seq=dec_seq)
        o_ref[i] = o_tok
        lse_ref[i] = lse_tok
        return carry

    lax.fori_loop(0, bb, one_batch, 0)


def _decode_attention(q, knp, vnp, kt, vt, bias, *, dec_seq):
    db, _, width = kt.shape
    pack = DEC_PACK // dec_seq
    assert db % pack == 0 and width % LANES == 0
    bb = max(1, min(pack, DEC_BLOCK_BYTES // (D_ATT * width * 4)))
    assert pack % bb == 0
    per_b = lambda s: (s, 0, 0)
    packed = lambda s: (s * bb // pack, 0, 0)
    return pl.pallas_call(
        functools.partial(_dec_kernel, width=width, dec_seq=dec_seq, bb=bb),
        grid=(db // bb,),
        in_specs=[pl.BlockSpec((bb, dec_seq, D_ATT), per_b),
                  pl.BlockSpec((1, D_ATT, DEC_PACK), packed),
                  pl.BlockSpec((1, D_ATT, DEC_PACK), packed),
                  pl.BlockSpec((bb, D_ATT, width), per_b),
                  pl.BlockSpec((bb, D_ATT, width), per_b),
                  pl.BlockSpec(bias.shape, lambda s: (0, 0))],
        out_specs=[pl.BlockSpec((bb, D_ATT, width), per_b),
                   pl.BlockSpec((bb, D_ATT, width), per_b),
                   pl.BlockSpec((bb, dec_seq, D_ATT), per_b),
                   pl.BlockSpec((bb, dec_seq, LANES), per_b)],
        out_shape=[jax.ShapeDtypeStruct((db, D_ATT, width), F32),
                   jax.ShapeDtypeStruct((db, D_ATT, width), F32),
                   jax.ShapeDtypeStruct((db, dec_seq, D_ATT), F32),
                   jax.ShapeDtypeStruct((db, dec_seq, LANES), F32)],
        compiler_params=pltpu.CompilerParams(dimension_semantics=("arbitrary",),
                                             vmem_limit_bytes=VMEM_LIMIT_BYTES),
        name=f"decode_attn_w{width}",
    )(q, knp, vnp, kt, vt, bias)


def _out_kernel(*refs, tm, dils):
    x_ref, cm_ref, sga_ref = refs[:3]
    o_refs = refs[3:3 + N_GROUPS]
    l_refs = refs[3 + N_GROUPS:3 + 2 * N_GROUPS]
    (g1_ref, sh2_ref, sc2_ref, g2_ref, n2_ref, wao_ref, wo_ref, w1_ref, w2_ref,
     y_ref, ostg, lstg) = refs[3 + 2 * N_GROUPS:]

    def token_order(ref, stg, dil):
        if dil == 1:
            return ref[0, 0].astype(F32)
        n_slab = ref.shape[-1] // LANES
        for r in range(dil):
            blk = ref[0, r].astype(F32)
            for i in range(n_slab):
                stg[i, pl.ds(r, tm // dil, stride=dil), :] = blk[:, i * LANES:(i + 1) * LANES]
        return jnp.concatenate([stg[i] for i in range(n_slab)], axis=1)

    lses = [token_order(l_refs[g], lstg, dils[g]) for g in range(N_GROUPS)]
    mx = jnp.maximum(jnp.maximum(lses[0], lses[1]), lses[2])
    es = [jnp.exp(l - mx) for l in lses]
    inv = 1.0 / (es[0] + es[1] + es[2])
    er = lax.broadcasted_iota(jnp.int32, (LANES, D_ATT), 0)
    ec = lax.broadcasted_iota(jnp.int32, (LANES, D_ATT), 1) // HEAD_DIM
    expand = jnp.where(er == ec, 1.0, 0.0).astype(BF16)
    o = None
    for g in range(N_GROUPS):
        w = es[g] * inv
        w_hi = w.astype(BF16)
        w_lo = (w - w_hi.astype(F32)).astype(BF16)
        wexp = _mm(w_hi, expand) + _mm(w_lo, expand)
        term = wexp * token_order(o_refs[g], ostg, dils[g])
        o = term if o is None else o + term
    attn_out = _mm(o.astype(BF16), wao_ref[...])
    mixed = cm_ref[0].astype(F32) + sga_ref[0].astype(F32) * attn_out
    x1 = x_ref[0] + g1_ref[0] * _mm(mixed.astype(BF16), wo_ref[...])
    ms = jnp.mean(x1 * x1, axis=-1, keepdims=True)
    xn2 = x1 * lax.rsqrt(ms + RMS_EPS) * n2_ref[...]
    xn2 = (xn2 * (1.0 + sc2_ref[0]) + sh2_ref[0]).astype(BF16)
    acc = None
    for f in range(D_FF // FF_CHUNK):
        cols = slice(f * FF_CHUNK, (f + 1) * FF_CHUNK)
        hid = jnp.maximum(_mm(xn2, w1_ref[:, cols]), 0.0)
        part = _mm((hid * hid).astype(BF16), w2_ref[cols, :])
        acc = part if acc is None else acc + part
    y_ref[0] = x1 + g2_ref[0] * acc


def _out_proj(x, cm, sga, os, lses, gate1, shift2, scale2, gate2, n2, wao, wo, w1, w2, *, tm):
    nb, ns, d = x.shape
    dils = tuple(o.shape[1] for o in os)
    assert ns % tm == 0 and all(tm % (dil * 2 * SUBLANES) == 0 for dil in dils)
    nmod = gate1.shape[1]
    tmod = 1 if nmod == 1 else tm
    row = lambda b, s: (b, s, 0)
    mod_map = (lambda b, s: (b, 0, 0)) if nmod == 1 else row
    mod_spec = pl.BlockSpec((1, tmod, d), mod_map)
    res_spec = lambda dil, n: pl.BlockSpec((1, dil, tm // dil, n), lambda b, s: (b, 0, s, 0))
    in_specs = ([pl.BlockSpec((1, tm, d), row)] * 3
                + [res_spec(dil, D_ATT) for dil in dils]
                + [res_spec(dil, LANES) for dil in dils]
                + [mod_spec] * 4
                + [_const_spec((1, d)), _const_spec((D_ATT, d)), _const_spec((d, d)),
                   _const_spec((d, D_FF)), _const_spec((D_FF, d))])
    return pl.pallas_call(
        functools.partial(_out_kernel, tm=tm, dils=dils),
        grid=(nb, ns // tm),
        in_specs=in_specs,
        out_specs=pl.BlockSpec((1, tm, d), row),
        out_shape=jax.ShapeDtypeStruct((nb, ns, d), F32),
        scratch_shapes=[pltpu.VMEM((D_ATT // LANES, tm, LANES), F32), pltpu.VMEM((1, tm, LANES), F32)],
        compiler_params=pltpu.CompilerParams(dimension_semantics=("arbitrary", "arbitrary"),
                                             vmem_limit_bytes=VMEM_LIMIT_BYTES),
        name="out_proj",
    )(x, cm, sga, *os, *lses, gate1, shift2, scale2, gate2, n2, wao, wo, w1, w2)


def _t5_causal_bucket(dist):
    max_exact = N_BUCKETS // 2
    ratio = jnp.maximum(dist, 1).astype(F32) / max_exact
    large = max_exact + (jnp.log(ratio) / math.log(MAX_DISTANCE / max_exact)
                         * (N_BUCKETS - max_exact)).astype(jnp.int32)
    large = jnp.minimum(large, N_BUCKETS - 1)
    return jnp.where(dist < max_exact, dist, large)


def _group_bias(rel_bias, g):
    steps = jnp.arange(WK + 1)
    bucket = _t5_causal_bucket(steps * DILS[g])
    onehot = (bucket[:, None] == jnp.arange(N_BUCKETS)[None, :]).astype(F32)
    b = jnp.dot(onehot, rel_bias[:, g * N_HEADS:(g + 1) * N_HEADS].astype(F32), precision=lax.Precision.HIGHEST)
    return b.T


def _prompt_bias_table(bias):
    blk, period = ATT_BLOCK, 4 * ATT_BLOCK
    rev = bias[:, ::-1]
    row = jnp.concatenate([rev, jnp.full((N_HEADS, period - (WK + 1)), NEG, F32)], axis=1)
    tiled = jnp.tile(row, (1, blk))[:, :blk * (period - 1)].reshape(N_HEADS, blk, period - 1)
    return tiled[:, :, :2 * blk].reshape(N_HEADS * blk, 2 * blk)


def _decode_bias_table(bias, width, dil, dec_seq):
    assert width == WK * dil and (dil == 1 or dil >= dec_seq)
    rev = bias[:, ::-1]
    neg_col = jnp.full((N_HEADS, 1), NEG, F32)
    rows = []
    for t in range(dec_seq):
        if dil == 1:
            cache = jnp.pad(rev[:, :width - t], ((0, 0), (t, 0)), constant_values=NEG)
        else:
            mine = (np.arange(dil) == t)[None, None, :]
            cache = jnp.where(mine, rev[:, :WK, None], NEG).reshape(N_HEADS, width)
        new = [bias[:, (t - tn) // dil:(t - tn) // dil + 1] if tn <= t and (t - tn) % dil == 0 else neg_col
               for tn in range(dec_seq)]
        rows.append(jnp.concatenate([cache, jnp.full((N_HEADS, LANES - dec_seq), NEG, F32)] + new, axis=1))
    return jnp.concatenate(rows, axis=0)


def _layer(x_prompt, x_sample, c_prompt, c_sample, state_conv, caches, rel_bias, norm1_g, norm2_g,
           w_ada, b_ada, w_in, conv_w, q_norm_g, k_norm_g, w_conv_out, w_attn_out, w_o, w_mlp_in,
           w_mlp_out):
    nbatch, ns, d = x_prompt.shape
    db, dec_seq, _ = x_sample.shape
    ntok = db * dec_seq

    w_in_b = w_in.astype(BF16)
    wco_b = w_conv_out.astype(BF16)
    wao_b = w_attn_out.astype(BF16)
    wo_b = w_o.astype(BF16)
    w1_b = w_mlp_in.astype(BF16)
    w2_b = w_mlp_out.astype(BF16)
    g1 = norm1_g.reshape(1, d)
    g2n = norm2_g.reshape(1, d)
    qg = jnp.tile(q_norm_g.reshape(1, HEAD_DIM), (1, N_HEADS))
    kg = jnp.tile(k_norm_g.reshape(1, HEAD_DIM), (1, N_HEADS))

    n_c = nbatch + db
    n_pad = -(-n_c // SUBLANES) * SUBLANES
    c_all = jnp.concatenate([c_prompt, c_sample, jnp.zeros((n_pad - n_c, d), F32)], axis=0)
    mod = _ada(c_all, w_ada, b_ada)
    mod_p = mod[:nbatch].reshape(nbatch, 1, N_MOD, d)
    mod_s = jnp.broadcast_to(mod[nbatch:n_c].reshape(db, 1, N_MOD, d), (db, dec_seq, N_MOD, d))
    mod_s = mod_s.reshape(1, ntok, N_MOD, d)
    mp = [mod_p[:, :, i] for i in range(N_MOD)]
    msn = [mod_s[:, :, i] for i in range(N_MOD)]

    biases = [_group_bias(rel_bias, g) for g in range(N_GROUPS)]

    cm, sga, qkv, utail = _in_proj(x_prompt, mp[0], mp[1], g1, w_in_b, conv_w, wco_b, qg, kg,
                                   tm=TM_IN, dils=DILS)
    os, lses = [], []
    for g in range(N_GROUPS):
        o, lse = _prompt_attention(qkv[3 * g], qkv[3 * g + 1], qkv[3 * g + 2], _prompt_bias_table(biases[g]))
        os.append(o)
        lses.append(lse)
    y_prompt = _out_proj(x_prompt, cm, sga, os, lses, mp[2], mp[3], mp[4], mp[5], g2n,
                         wao_b, wo_b, w1_b, w2_b, tm=TM_OUT)
    p_conv = utail[:, SUBLANES - (CONV_WIDTH - 1):, :]
    p_kv = []
    for g, (window, dil) in enumerate(DILATED_GROUPS):
        keep = min(window, ns)
        assert keep % dil == 0
        for part in (1, 2):
            tail = qkv[3 * g + part][:, :, (ns - keep) // dil:, :]
            p_kv.append(tail.transpose(0, 2, 1, 3).astype(F32).reshape(nbatch, keep, N_HEADS, HEAD_DIM))

    xs = x_sample.reshape(1, ntok, d)
    zero_row = jnp.zeros((db, 1, D_CONV), F32)
    hist1 = jnp.concatenate([state_conv[:, 1:2]] + [zero_row] * (dec_seq - 1), axis=1).reshape(1, ntok, D_CONV)
    hist2 = jnp.concatenate([state_conv[:, 0:1], state_conv[:, 1:2]] + [zero_row] * (dec_seq - 2),
                            axis=1).reshape(1, ntok, D_CONV)
    cm_s, sga_s, qkv_s, u_s = _in_proj(xs, msn[0], msn[1], g1, w_in_b, conv_w, wco_b, qg, kg, tm=ntok,
                                       dils=(1,) * N_GROUPS, hist1=hist1, hist2=hist2, dec_seq=dec_seq)
    s_conv = u_s.reshape(db, dec_seq, D_CONV)[:, dec_seq - (CONV_WIDTH - 1):]

    def pack_cols(a):
        return a.reshape(ntok // DEC_PACK, DEC_PACK, D_ATT).transpose(0, 2, 1)

    os_s, lses_s, s_kv = [], [], []
    for g, (window, dil) in enumerate(DILATED_GROUPS):
        ck, cv = caches[2 * g], caches[2 * g + 1]
        width = ck.shape[1]
        assert width == window
        kt = ck.transpose(0, 2, 3, 1).reshape(db, D_ATT, width)
        vt = cv.transpose(0, 2, 3, 1).reshape(db, D_ATT, width)
        bias_tab = _decode_bias_table(biases[g], width, dil, dec_seq)
        skt, svt, o_s, lse_s = _decode_attention(qkv_s[3 * g].reshape(db, dec_seq, D_ATT),
                                                 pack_cols(qkv_s[3 * g + 1]), pack_cols(qkv_s[3 * g + 2]),
                                                 kt, vt, bias_tab, dec_seq=dec_seq)
        s_kv.append(skt.reshape(db, N_HEADS, HEAD_DIM, width).transpose(0, 3, 1, 2))
        s_kv.append(svt.reshape(db, N_HEADS, HEAD_DIM, width).transpose(0, 3, 1, 2))
        os_s.append(o_s.reshape(1, 1, ntok, D_ATT).astype(BF16))
        lses_s.append(lse_s.reshape(1, 1, ntok, LANES))
    y_sample = _out_proj(xs, cm_s, sga_s, os_s, lses_s, msn[2], msn[3], msn[4], msn[5], g2n,
                         wao_b, wo_b, w1_b, w2_b, tm=ntok).reshape(db, dec_seq, d)
    return y_prompt, y_sample, [p_conv] + p_kv, [s_conv] + s_kv


def kernel(x_prompt, x_sample, c_prompt, c_sample, state_conv, cache_k1, cache_v1, cache_k2, cache_v2,
           cache_k3, cache_v3, rel_bias, norm1_g, norm2_g, w_ada, b_ada, w_in, conv_w, q_norm_g, k_norm_g,
           w_conv_out, w_attn_out, w_o, w_mlp_in, w_mlp_out):
    depth = w_in.shape[0]
    caches = (cache_k1, cache_v1, cache_k2, cache_v2, cache_k3, cache_v3)
    yp, ys = x_prompt, x_sample
    p_states = [[] for _ in range(1 + 2 * N_GROUPS)]
    s_states = [[] for _ in range(1 + 2 * N_GROUPS)]
    for l in range(depth):
        yp, ys, p_new, s_new = _layer(
            yp, ys, c_prompt, c_sample, state_conv[l], [c[l] for c in caches], rel_bias,
            norm1_g[l], norm2_g[l], w_ada[l], b_ada[l], w_in[l], conv_w[l], q_norm_g[l], k_norm_g[l],
            w_conv_out[l], w_attn_out[l], w_o[l], w_mlp_in[l], w_mlp_out[l])
        for lst, a in zip(p_states, p_new):
            lst.append(a)
        for lst, a in zip(s_states, s_new):
            lst.append(a)
    p_out = [jnp.stack(a) for a in p_states]
    s_out = [jnp.stack(a) for a in s_states]
    return (yp, ys, *p_out, *s_out)
```

```python
import functools
import math

import numpy as np
import jax
import jax.numpy as jnp
from jax import lax
from jax.experimental import pallas as pl
from jax.experimental.pallas import tpu as pltpu

F32 = jnp.float32
BF16 = jnp.bfloat16

D_MODEL = 1024
D_CONV = D_MODEL
CONV_WIDTH = 3
HEAD_DIM = 64
N_HEADS = 8
D_ATT = N_HEADS * HEAD_DIM
DILATED_GROUPS = ((128, 1), (512, 4), (2048, 16))
N_GROUPS = len(DILATED_GROUPS)
DILS = tuple(d for _, d in DILATED_GROUPS)
D_QKV = N_GROUPS * D_ATT
D_FF = 4 * D_MODEL
N_BUCKETS = 32
MAX_DISTANCE = 2048
ATT_BLOCK = 128
WK = 128
N_MOD = 6
RMS_EPS = 1e-6
ATT_SCALE = HEAD_DIM ** -0.5
NEG = -1e30

OFF_H, OFF_B, OFF_C = 0, D_CONV, 2 * D_CONV
OFF_Q = 3 * D_CONV
OFF_K = OFF_Q + D_QKV
OFF_V = OFF_K + D_QKV
OFF_GC = OFF_V + D_QKV
OFF_GA = OFF_GC + D_MODEL
D_PROJ = OFF_GA + D_MODEL

LANES = 128
SUBLANES = 8
MXU_DIM = 256
VMEM_LIMIT_BYTES = 56 * 1024 * 1024

TM_IN = 512
TM_OUT = 512
FF_CHUNK = 512
DEC_PACK = LANES
DEC_BLOCK_BYTES = 4 * 1024 * 1024

assert all(w // d == WK for w, d in DILATED_GROUPS)


def _mm(a, b):
    return jnp.dot(a, b, preferred_element_type=F32)


def _const_spec(shape):
    nd = len(shape)
    return pl.BlockSpec(shape, lambda *_: (0,) * nd, pipeline_mode=pl.Buffered(1))


def _ada_kernel(c_ref, w_ref, b_ref, o_ref):
    c = c_ref[...]
    s = c * jax.nn.sigmoid(c)
    o_ref[...] = _mm(s.astype(BF16), w_ref[...].astype(BF16)) + b_ref[...]


def _ada(c_all, w_ada, b_ada):
    n, d = c_all.shape
    nout = w_ada.shape[1]
    tn = 1024
    return pl.pallas_call(
        _ada_kernel,
        grid=(nout // tn,),
        in_specs=[pl.BlockSpec((n, d), lambda j: (0, 0)),
                  pl.BlockSpec((d, tn), lambda j: (0, j)),
                  pl.BlockSpec((1, tn), lambda j: (0, j))],
        out_specs=pl.BlockSpec((n, tn), lambda j: (0, j)),
        out_shape=jax.ShapeDtypeStruct((n, nout), F32),
        compiler_params=pltpu.CompilerParams(dimension_semantics=("arbitrary",),
                                             vmem_limit_bytes=VMEM_LIMIT_BYTES),
        name="ada",
    )(c_all, w_ada, b_ada.reshape(1, nout))


def _in_kernel(*refs, decode, tm, dec_seq, dils):
    n_in = 11 if decode else 9
    x_ref, sh_ref, sc_ref, g1_ref, win_ref, cw_ref, wco_ref, qg_ref, kg_ref = refs[:9]
    if decode:
        s1_ref, s2_ref = refs[9:11]
    cm_ref, sga_ref = refs[n_in:n_in + 2]
    qkv_refs = refs[n_in + 2:n_in + 2 + 3 * N_GROUPS]
    ustate_ref, uext, stg = refs[n_in + 2 + 3 * N_GROUPS:]

    x = x_ref[0]
    ms = jnp.mean(x * x, axis=-1, keepdims=True)
    xn = x * lax.rsqrt(ms + RMS_EPS) * g1_ref[...]
    xn = (xn * (1.0 + sc_ref[0]) + sh_ref[0]).astype(BF16)

    h = _mm(xn, win_ref[:, OFF_H:OFF_H + D_CONV])
    c = _mm(xn, win_ref[:, OFF_C:OFF_C + D_CONV])
    u = c * h
    hist = SUBLANES
    if decode:
        uext[0:hist, :] = jnp.zeros((hist, D_CONV), F32)
    else:
        @pl.when(pl.program_id(1) == 0)
        def _():
            uext[0:hist, :] = jnp.zeros((hist, D_CONV), F32)
    uext[hist:hist + tm, :] = u
    um1 = uext[hist - 1:hist - 1 + tm, :]
    um2 = uext[hist - 2:hist - 2 + tm, :]
    if decode:
        t = lax.broadcasted_iota(jnp.int32, (tm, 1), 0) % dec_seq
        um1 = jnp.where(t >= 1, um1, s1_ref[0])
        um2 = jnp.where(t >= 2, um2, s2_ref[0])
        ustate_ref[0] = u
    else:
        uext[0:hist, :] = uext[tm:tm + hist, :]
        ustate_ref[0] = u[tm - hist:tm, :]
    y = cw_ref[0:1, :] * um2 + cw_ref[1:2, :] * um1 + cw_ref[2:3, :] * u
    bg = _mm(xn, win_ref[:, OFF_B:OFF_B + D_CONV])
    conv_out = _mm((bg * y).astype(BF16), wco_ref[...])
    gc = _mm(xn, win_ref[:, OFF_GC:OFF_GC + D_MODEL])
    cm_ref[0] = (jax.nn.sigmoid(gc) * conv_out).astype(cm_ref.dtype)
    ga = _mm(xn, win_ref[:, OFF_GA:OFF_GA + D_MODEL])
    sga_ref[0] = jax.nn.sigmoid(ga).astype(sga_ref.dtype)

    seg_r = lax.broadcasted_iota(jnp.int32, (MXU_DIM, MXU_DIM), 0) // HEAD_DIM
    seg_c = lax.broadcasted_iota(jnp.int32, (MXU_DIM, MXU_DIM), 1) // HEAD_DIM
    seg = jnp.where(seg_r == seg_c, 1.0, 0.0).astype(BF16)

    def put(out_ref, val, dil, part):
        if dil == 1:
            out_ref[0, 0] = val.astype(out_ref.dtype)
            return
        for i in range(D_ATT // LANES):
            stg[part, i] = val[:, i * LANES:(i + 1) * LANES]
        for r in range(dil):
            for i in range(D_ATT // LANES):
                out_ref[0, r, :, i * LANES:(i + 1) * LANES] = (
                    stg[part, i, pl.ds(r, tm // dil, stride=dil), :].astype(out_ref.dtype))

    def head_norm(a, gain):
        sq = (a * a).astype(BF16)
        ss = jnp.concatenate([_mm(sq[:, i * MXU_DIM:(i + 1) * MXU_DIM], seg)
                              for i in range(D_ATT // MXU_DIM)], axis=1)
        return a * lax.rsqrt(ss * (1.0 / HEAD_DIM) + RMS_EPS) * gain

    for g in range(N_GROUPS):
        q_ref, k_ref, v_ref = qkv_refs[3 * g:3 * g + 3]
        lo = g * D_ATT
        q = _mm(xn, win_ref[:, OFF_Q + lo:OFF_Q + lo + D_ATT])
        put(q_ref, head_norm(q, qg_ref[...] * ATT_SCALE), dils[g], 0)
        k = _mm(xn, win_ref[:, OFF_K + lo:OFF_K + lo + D_ATT])
        put(k_ref, head_norm(k, kg_ref[...]), dils[g], 1)
        put(v_ref, _mm(xn, win_ref[:, OFF_V + lo:OFF_V + lo + D_ATT]), dils[g], 2)


def _in_proj(x, shift1, scale1, g1, w_in, conv_w, w_conv_out, qg, kg, *, tm, dils, hist1=None, hist2=None,
             dec_seq=1):
    nb, ns, d = x.shape
    decode = hist1 is not None
    assert ns % tm == 0 and all(tm % (dil * 2 * SUBLANES) == 0 for dil in dils)
    nmod = shift1.shape[1]
    tmod = 1 if nmod == 1 else tm
    row = lambda b, s: (b, s, 0)
    mod_map = (lambda b, s: (b, 0, 0)) if nmod == 1 else row
    act_dtype = F32 if decode else BF16
    in_specs = [pl.BlockSpec((1, tm, d), row),
                pl.BlockSpec((1, tmod, d), mod_map),
                pl.BlockSpec((1, tmod, d), mod_map),
                _const_spec((1, d)),
                _const_spec((d, D_PROJ)),
                _const_spec((CONV_WIDTH, D_CONV)),
                _const_spec((D_CONV, D_MODEL)),
                _const_spec((1, D_ATT)),
                _const_spec((1, D_ATT))]
    args = [x, shift1, scale1, g1, w_in, conv_w, w_conv_out, qg, kg]
    if decode:
        in_specs += [pl.BlockSpec((1, tm, D_CONV), row), pl.BlockSpec((1, tm, D_CONV), row)]
        args += [hist1, hist2]
    out_shape = [jax.ShapeDtypeStruct((nb, ns, D_MODEL), BF16),
                 jax.ShapeDtypeStruct((nb, ns, D_MODEL), BF16)]
    out_specs = [pl.BlockSpec((1, tm, D_MODEL), row),
                 pl.BlockSpec((1, tm, D_MODEL), row)]
    for dil in dils:
        for _ in range(3):
            out_shape.append(jax.ShapeDtypeStruct((nb, dil, ns // dil, D_ATT), act_dtype))
            out_specs.append(pl.BlockSpec((1, dil, tm // dil, D_ATT), lambda b, s: (b, 0, s, 0)))
    if decode:
        out_shape.append(jax.ShapeDtypeStruct((nb, ns, D_CONV), F32))
        out_specs.append(pl.BlockSpec((1, tm, D_CONV), row))
    else:
        out_shape.append(jax.ShapeDtypeStruct((nb, SUBLANES, D_CONV), F32))
        out_specs.append(pl.BlockSpec((1, SUBLANES, D_CONV), lambda b, s: (b, 0, 0)))
    outs = pl.pallas_call(
        functools.partial(_in_kernel, decode=decode, tm=tm, dec_seq=dec_seq, dils=tuple(dils)),
        grid=(nb, ns // tm),
        in_specs=in_specs,
        out_specs=out_specs,
        out_shape=out_shape,
        scratch_shapes=[pltpu.VMEM((tm + SUBLANES, D_CONV), F32),
                        pltpu.VMEM((3, D_ATT // LANES, tm, LANES), F32)],
        compiler_params=pltpu.CompilerParams(dimension_semantics=("arbitrary", "arbitrary"),
                                             vmem_limit_bytes=VMEM_LIMIT_BYTES),
        name="in_proj_decode" if decode else "in_proj",
    )(*args)
    return outs[0], outs[1], outs[2:2 + 3 * N_GROUPS], outs[-1]


HEADS_PER_PASS = MXU_DIM // HEAD_DIM


def _attn_blocks(q_ref, kc_ref, kp_ref, vc_ref, vp_ref, tb_ref, o_ref, lse_ref, *, nb, first):
    blk = ATT_BLOCK
    lane = lax.broadcasted_iota(jnp.int32, (1, MXU_DIM), 1)
    hmask = [(lane >= HEAD_DIM * h) & (lane < HEAD_DIM * (h + 1)) for h in range(HEADS_PER_PASS)]
    col = lax.broadcasted_iota(jnp.int32, (1, 2 * blk), 1)
    lane_out = lax.broadcasted_iota(jnp.int32, (1, LANES), 1)
    for n in range(nb):
        rows = slice(n * blk, (n + 1) * blk)
        lse_acc = jnp.zeros((blk, LANES), F32)
        for hp in range(N_HEADS // HEADS_PER_PASS):
            cols = slice(hp * MXU_DIM, (hp + 1) * MXU_DIM)
            q4 = q_ref[0, 0, rows, cols]
            lhs = jnp.concatenate([jnp.where(hmask[h], q4, jnp.zeros_like(q4))
                                   for h in range(HEADS_PER_PASS)], axis=0)
            if n == 0:
                k_prev, v_prev = kp_ref[0, 0, :, cols], vp_ref[0, 0, :, cols]
            else:
                prev = slice((n - 1) * blk, n * blk)
                k_prev, v_prev = kc_ref[0, 0, prev, cols], vc_ref[0, 0, prev, cols]
            kk = jnp.concatenate([k_prev, kc_ref[0, 0, rows, cols]], axis=0)
            vv = jnp.concatenate([v_prev, vc_ref[0, 0, rows, cols]], axis=0)
            s = lax.dot_general(lhs, kk, (((1,), (1,)), ((), ())), preferred_element_type=F32)
            s = s + tb_ref[hp * HEADS_PER_PASS * blk:(hp + 1) * HEADS_PER_PASS * blk, :]
            if n == 0:
                s = jnp.where(first & (col < blk), NEG, s)
            m = jnp.max(s, axis=-1, keepdims=True)
            p = jnp.exp(s - m)
            l = jnp.sum(p, axis=-1, keepdims=True)
            pv = _mm(p.astype(BF16), vv) * (1.0 / l)
            lse = m + jnp.log(l)
            o4 = jnp.zeros((blk, MXU_DIM), F32)
            for h in range(HEADS_PER_PASS):
                hr = slice(h * blk, (h + 1) * blk)
                o4 = jnp.where(hmask[h], pv[hr, :], o4)
                lse_acc = jnp.where(lane_out == hp * HEADS_PER_PASS + h, lse[hr, :], lse_acc)
            o_ref[0, 0, rows, cols] = o4.astype(o_ref.dtype)
        lse_ref[0, 0, rows, :] = lse_acc


def _decode_one(q, k_tail, v_tail, kt_ref, vt_ref, bias_ref, skt_ref, svt_ref, i, *, width, dec_seq):
    nch = width // LANES
    keep = LANES - dec_seq
    lane = lax.broadcasted_iota(jnp.int32, (1, LANES), 1)
    nrow = dec_seq * N_HEADS
    own = (lax.broadcasted_iota(jnp.int32, (nrow, D_ATT), 0) % N_HEADS
           == lax.broadcasted_iota(jnp.int32, (nrow, D_ATT), 1) // HEAD_DIM)
    q_rows = jnp.concatenate([jnp.broadcast_to(q[t:t + 1, :], (N_HEADS, D_ATT)) for t in range(dec_seq)], axis=0)
    q_bd = jnp.where(own, q_rows, 0.0).astype(BF16)

    kc = kt_ref[i]
    vc = vt_ref[i]
    s_c = _mm(q_bd, kc.astype(BF16)) + bias_ref[:, 0:width]
    s_t = _mm(q_bd, k_tail.astype(BF16)) + bias_ref[:, width:width + LANES]
    m = jnp.maximum(jnp.max(s_c, axis=-1, keepdims=True), jnp.max(s_t, axis=-1, keepdims=True))
    p_c = jnp.exp(s_c - m)
    p_t = jnp.exp(s_t - m)
    l = jnp.sum(p_c, axis=-1, keepdims=True) + jnp.sum(p_t, axis=-1, keepdims=True)
    nt = (((1,), (1,)), ((), ()))
    o = (lax.dot_general(p_c.astype(BF16), vc.astype(BF16), nt, preferred_element_type=F32)
         + lax.dot_general(p_t.astype(BF16), v_tail.astype(BF16), nt, preferred_element_type=F32))
    o = jnp.where(own, o, 0.0) * (1.0 / l)
    o_tok = jnp.sum(o.reshape(dec_seq, N_HEADS, D_ATT), axis=1)
    lse = m + jnp.log(l)
    head_lane = (lax.broadcasted_iota(jnp.int32, (nrow, LANES), 0) % N_HEADS
                 == lax.broadcasted_iota(jnp.int32, (nrow, LANES), 1))
    lse_tok = jnp.sum(jnp.where(head_lane, lse, 0.0).reshape(dec_seq, N_HEADS, LANES), axis=1)

    for src_ref, tail, dst_ref in ((kt_ref, k_tail, skt_ref), (vt_ref, v_tail, svt_ref)):
        cur = pltpu.roll(src_ref[i, :, 0:LANES], keep, 1)
        for c in range(nch):
            nxt = pltpu.roll(src_ref[i, :, (c + 1) * LANES:(c + 2) * LANES], keep, 1) if c + 1 < nch else tail
            dst_ref[i, :, c * LANES:(c + 1) * LANES] = jnp.where(lane < keep, cur, nxt)
            cur = nxt
    return o_tok, lse_tok


def _dec_kernel(q_ref, knp_ref, vnp_ref, kt_ref, vt_ref, bias_ref, skt_ref, svt_ref, o_ref, lse_ref,
                *, width, dec_seq, bb):
    step = pl.program_id(0)
    pack = DEC_PACK // dec_seq
    keep = LANES - dec_seq

    def one_batch(i, carry):
        slot = ((step * bb + i) % pack) * dec_seq
        to_tail = (keep + LANES - slot) % LANES
        k_tail = pltpu.roll(knp_ref[0], to_tail, 1)
        v_tail = pltpu.roll(vnp_ref[0], to_tail, 1)
        o_tok, lse_tok = _decode_one(q_ref[i], k_tail, v_tail, kt_ref, vt_ref, bias_ref, skt_ref, svt_ref, i,
                                     width=width, dec_seq=dec_seq)
        o_ref[i] = o_tok
        lse_ref[i] = lse_tok
        return carry

    lax.fori_loop(0, bb, one_batch, 0)


def _decode_attention(q, knp, vnp, kt, vt, bias, *, dec_seq):
    db, _, width = kt.shape
    pack = DEC_PACK // dec_seq
    assert db % pack == 0 and width % LANES == 0
    bb = max(1, min(pack, DEC_BLOCK_BYTES // (D_ATT * width * 4)))
    assert pack % bb == 0
    per_b = lambda s: (s, 0, 0)
    packed = lambda s: (s * bb // pack, 0, 0)
    return pl.pallas_call(
        functools.partial(_dec_kernel, width=width, dec_seq=dec_seq, bb=bb),
        grid=(db // bb,),
        in_specs=[pl.BlockSpec((bb, dec_seq, D_ATT), per_b),
                  pl.BlockSpec((1, D_ATT, DEC_PACK), packed),
                  pl.BlockSpec((1, D_ATT, DEC_PACK), packed),
                  pl.BlockSpec((bb, D_ATT, width), per_b),
                  pl.BlockSpec((bb, D_ATT, width), per_b),
                  pl.BlockSpec(bias.shape, lambda s: (0, 0))],
        out_specs=[pl.BlockSpec((bb, D_ATT, width), per_b),
                   pl.BlockSpec((bb, D_ATT, width), per_b),
                   pl.BlockSpec((bb, dec_seq, D_ATT), per_b),
                   pl.BlockSpec((bb, dec_seq, LANES), per_b)],
        out_shape=[jax.ShapeDtypeStruct((db, D_ATT, width), F32),
                   jax.ShapeDtypeStruct((db, D_ATT, width), F32),
                   jax.ShapeDtypeStruct((db, dec_seq, D_ATT), F32),
                   jax.ShapeDtypeStruct((db, dec_seq, LANES), F32)],
        compiler_params=pltpu.CompilerParams(dimension_semantics=("arbitrary",),
                                             vmem_limit_bytes=VMEM_LIMIT_BYTES),
        name=f"decode_attn_w{width}",
    )(q, knp, vnp, kt, vt, bias)


def _attn_cache_kernel(*refs, nb, rows_blocks, width, dec_seq):
    dq_ref, knp_ref, vnp_ref, kt_ref, vt_ref, bias_ref = refs[:6]
    att_in = refs[6:6 + 6 * N_GROUPS]
    n_in = 6 + 6 * N_GROUPS
    skt_ref, svt_ref, do_ref, dlse_ref = refs[n_in:n_in + 4]
    att_out = refs[n_in + 4:]
    step = pl.program_id(0)
    pack = DEC_PACK // dec_seq
    slot = (step % pack) * dec_seq
    to_tail = (2 * LANES - dec_seq - slot) % LANES
    k_tail = pltpu.roll(knp_ref[0], to_tail, 1)
    v_tail = pltpu.roll(vnp_ref[0], to_tail, 1)
    o_tok, lse_tok = _decode_one(dq_ref[0], k_tail, v_tail, kt_ref, vt_ref, bias_ref, skt_ref, svt_ref, 0,
                                 width=width, dec_seq=dec_seq)
    do_ref[0] = o_tok
    dlse_ref[0] = lse_tok
    for g in range(N_GROUPS):
        first = (step * nb) % rows_blocks[g] == 0
        _attn_blocks(*att_in[6 * g:6 * g + 6], *att_out[2 * g:2 * g + 2], nb=nb, first=first)


def _attention_and_cache(qkv, tbs, dq, knp, vnp, kt, vt, bias, *, dec_seq):
    db, _, width = kt.shape
    pack = DEC_PACK // dec_seq
    nbatch = qkv[0].shape[0]
    total_blocks = nbatch * qkv[0].shape[1] * qkv[0].shape[2] // ATT_BLOCK
    assert db % pack == 0 and total_blocks % db == 0
    nb = total_blocks // db
    tq = nb * ATT_BLOCK
    per_b = lambda s: (s, 0, 0)
    packed = lambda s: (s // pack, 0, 0)
    in_specs = [pl.BlockSpec((1, dec_seq, D_ATT), per_b),
                pl.BlockSpec((1, D_ATT, DEC_PACK), packed),
                pl.BlockSpec((1, D_ATT, DEC_PACK), packed),
                pl.BlockSpec((1, D_ATT, width), per_b),
                pl.BlockSpec((1, D_ATT, width), per_b),
                pl.BlockSpec(bias.shape, lambda s: (0, 0))]
    args = [dq, knp, vnp, kt, vt, bias]
    out_specs = [pl.BlockSpec((1, D_ATT, width), per_b),
                 pl.BlockSpec((1, D_ATT, width), per_b),
                 pl.BlockSpec((1, dec_seq, D_ATT), per_b),
                 pl.BlockSpec((1, dec_seq, LANES), per_b)]
    out_shape = [jax.ShapeDtypeStruct((db, D_ATT, width), F32),
                 jax.ShapeDtypeStruct((db, D_ATT, width), F32),
                 jax.ShapeDtypeStruct((db, dec_seq, D_ATT), F32),
                 jax.ShapeDtypeStruct((db, dec_seq, LANES), F32)]
    rows_blocks = []
    for g in range(N_GROUPS):
        q, k, v = qkv[3 * g:3 * g + 3]
        _, dil, nl, _ = q.shape
        rb = nl // ATT_BLOCK
        assert nl % ATT_BLOCK == 0 and rb % nb == 0
        rows_blocks.append(rb)

        def cur(s, rb=rb, dil=dil):
            blk = s * nb
            return (blk // (dil * rb), (blk // rb) % dil, (blk % rb) // nb, 0)

        def prev(s, rb=rb, dil=dil):
            blk = s * nb
            return (blk // (dil * rb), (blk // rb) % dil, jnp.maximum(blk % rb - 1, 0), 0)

        in_specs += [pl.BlockSpec((1, 1, tq, D_ATT), cur),
                     pl.BlockSpec((1, 1, tq, D_ATT), cur),
                     pl.BlockSpec((1, 1, ATT_BLOCK, D_ATT), prev),
                     pl.BlockSpec((1, 1, tq, D_ATT), cur),
                     pl.BlockSpec((1, 1, ATT_BLOCK, D_ATT), prev),
                     pl.BlockSpec((N_HEADS * ATT_BLOCK, 2 * ATT_BLOCK), lambda s: (0, 0))]
        args += [q, k, k, v, v, tbs[g]]
        out_specs += [pl.BlockSpec((1, 1, tq, D_ATT), cur), pl.BlockSpec((1, 1, tq, LANES), cur)]
        out_shape += [jax.ShapeDtypeStruct(q.shape, BF16),
                      jax.ShapeDtypeStruct(q.shape[:3] + (LANES,), F32)]
    outs = pl.pallas_call(
        functools.partial(_attn_cache_kernel, nb=nb, rows_blocks=tuple(rows_blocks), width=width,
                          dec_seq=dec_seq),
        grid=(db,),
        in_specs=in_specs,
        out_specs=out_specs,
        out_shape=out_shape,
        compiler_params=pltpu.CompilerParams(dimension_semantics=("arbitrary",),
                                             vmem_limit_bytes=VMEM_LIMIT_BYTES),
        name="attn_and_cache",
    )(*args)
    return outs[4:], outs[:4]


def _out_kernel(*refs, tm, dils):
    x_ref, cm_ref, sga_ref = refs[:3]
    o_refs = refs[3:3 + N_GROUPS]
    l_refs = refs[3 + N_GROUPS:3 + 2 * N_GROUPS]
    (g1_ref, sh2_ref, sc2_ref, g2_ref, n2_ref, wao_ref, wo_ref, w1_ref, w2_ref,
     y_ref, ostg, lstg) = refs[3 + 2 * N_GROUPS:]

    def token_order(ref, stg, dil):
        if dil == 1:
            return ref[0, 0].astype(F32)
        n_slab = ref.shape[-1] // LANES
        for r in range(dil):
            blk = ref[0, r].astype(F32)
            for i in range(n_slab):
                stg[i, pl.ds(r, tm // dil, stride=dil), :] = blk[:, i * LANES:(i + 1) * LANES]
        return jnp.concatenate([stg[i] for i in range(n_slab)], axis=1)

    lses = [token_order(l_refs[g], lstg, dils[g]) for g in range(N_GROUPS)]
    mx = jnp.maximum(jnp.maximum(lses[0], lses[1]), lses[2])
    es = [jnp.exp(l - mx) for l in lses]
    inv = 1.0 / (es[0] + es[1] + es[2])
    er = lax.broadcasted_iota(jnp.int32, (LANES, D_ATT), 0)
    ec = lax.broadcasted_iota(jnp.int32, (LANES, D_ATT), 1) // HEAD_DIM
    expand = jnp.where(er == ec, 1.0, 0.0).astype(BF16)
    o = None
    for g in range(N_GROUPS):
        w = es[g] * inv
        w_hi = w.astype(BF16)
        w_lo = (w - w_hi.astype(F32)).astype(BF16)
        wexp = _mm(w_hi, expand) + _mm(w_lo, expand)
        term = wexp * token_order(o_refs[g], ostg, dils[g])
        o = term if o is None else o + term
    attn_out = _mm(o.astype(BF16), wao_ref[...])
    mixed = cm_ref[0].astype(F32) + sga_ref[0].astype(F32) * attn_out
    x1 = x_ref[0] + g1_ref[0] * _mm(mixed.astype(BF16), wo_ref[...])
    ms = jnp.mean(x1 * x1, axis=-1, keepdims=True)
    xn2 = x1 * lax.rsqrt(ms + RMS_EPS) * n2_ref[...]
    xn2 = (xn2 * (1.0 + sc2_ref[0]) + sh2_ref[0]).astype(BF16)
    acc = None
    for f in range(D_FF // FF_CHUNK):
        cols = slice(f * FF_CHUNK, (f + 1) * FF_CHUNK)
        hid = jnp.maximum(_mm(xn2, w1_ref[:, cols]), 0.0)
        part = _mm((hid * hid).astype(BF16), w2_ref[cols, :])
        acc = part if acc is None else acc + part
    y_ref[0] = x1 + g2_ref[0] * acc


def _out_proj(x, cm, sga, os, lses, gate1, shift2, scale2, gate2, n2, wao, wo, w1, w2, *, tm):
    nb, ns, d = x.shape
    dils = tuple(o.shape[1] for o in os)
    assert ns % tm == 0 and all(tm % (dil * 2 * SUBLANES) == 0 for dil in dils)
    nmod = gate1.shape[1]
    tmod = 1 if nmod == 1 else tm
    row = lambda b, s: (b, s, 0)
    mod_map = (lambda b, s: (b, 0, 0)) if nmod == 1 else row
    mod_spec = pl.BlockSpec((1, tmod, d), mod_map)
    res_spec = lambda dil, n: pl.BlockSpec((1, dil, tm // dil, n), lambda b, s: (b, 0, s, 0))
    in_specs = ([pl.BlockSpec((1, tm, d), row)] * 3
                + [res_spec(dil, D_ATT) for dil in dils]
                + [res_spec(dil, LANES) for dil in dils]
                + [mod_spec] * 4
                + [_const_spec((1, d)), _const_spec((D_ATT, d)), _const_spec((d, d)),
                   _const_spec((d, D_FF)), _const_spec((D_FF, d))])
    return pl.pallas_call(
        functools.partial(_out_kernel, tm=tm, dils=dils),
        grid=(nb, ns // tm),
        in_specs=in_specs,
        out_specs=pl.BlockSpec((1, tm, d), row),
        out_shape=jax.ShapeDtypeStruct((nb, ns, d), F32),
        scratch_shapes=[pltpu.VMEM((D_ATT // LANES, tm, LANES), F32), pltpu.VMEM((1, tm, LANES), F32)],
        compiler_params=pltpu.CompilerParams(dimension_semantics=("arbitrary", "arbitrary"),
                                             vmem_limit_bytes=VMEM_LIMIT_BYTES),
        name="out_proj",
    )(x, cm, sga, *os, *lses, gate1, shift2, scale2, gate2, n2, wao, wo, w1, w2)


def _t5_causal_bucket(dist):
    max_exact = N_BUCKETS // 2
    ratio = jnp.maximum(dist, 1).astype(F32) / max_exact
    large = max_exact + (jnp.log(ratio) / math.log(MAX_DISTANCE / max_exact)
                         * (N_BUCKETS - max_exact)).astype(jnp.int32)
    large = jnp.minimum(large, N_BUCKETS - 1)
    return jnp.where(dist < max_exact, dist, large)


def _group_bias(rel_bias, g):
    steps = jnp.arange(WK + 1)
    bucket = _t5_causal_bucket(steps * DILS[g])
    onehot = (bucket[:, None] == jnp.arange(N_BUCKETS)[None, :]).astype(F32)
    b = jnp.dot(onehot, rel_bias[:, g * N_HEADS:(g + 1) * N_HEADS].astype(F32), precision=lax.Precision.HIGHEST)
    return b.T


def _prompt_bias_table(bias):
    blk, period = ATT_BLOCK, 4 * ATT_BLOCK
    rev = bias[:, ::-1]
    row = jnp.concatenate([rev, jnp.full((N_HEADS, period - (WK + 1)), NEG, F32)], axis=1)
    tiled = jnp.tile(row, (1, blk))[:, :blk * (period - 1)].reshape(N_HEADS, blk, period - 1)
    return tiled[:, :, :2 * blk].reshape(N_HEADS * blk, 2 * blk)


def _decode_bias_table(bias, width, dil, dec_seq):
    assert width == WK * dil and (dil == 1 or dil >= dec_seq)
    rev = bias[:, ::-1]
    neg_col = jnp.full((N_HEADS, 1), NEG, F32)
    rows = []
    for t in range(dec_seq):
        if dil == 1:
            cache = jnp.pad(rev[:, :width - t], ((0, 0), (t, 0)), constant_values=NEG)
        else:
            mine = (np.arange(dil) == t)[None, None, :]
            cache = jnp.where(mine, rev[:, :WK, None], NEG).reshape(N_HEADS, width)
        new = [bias[:, (t - tn) // dil:(t - tn) // dil + 1] if tn <= t and (t - tn) % dil == 0 else neg_col
               for tn in range(dec_seq)]
        rows.append(jnp.concatenate([cache, jnp.full((N_HEADS, LANES - dec_seq), NEG, F32)] + new, axis=1))
    return jnp.concatenate(rows, axis=0)


def _layer(x_prompt, x_sample, c_prompt, c_sample, state_conv, caches, rel_bias, norm1_g, norm2_g,
           w_ada, b_ada, w_in, conv_w, q_norm_g, k_norm_g, w_conv_out, w_attn_out, w_o, w_mlp_in,
           w_mlp_out):
    nbatch, ns, d = x_prompt.shape
    db, dec_seq, _ = x_sample.shape
    ntok = db * dec_seq

    w_in_b = w_in.astype(BF16)
    wco_b = w_conv_out.astype(BF16)
    wao_b = w_attn_out.astype(BF16)
    wo_b = w_o.astype(BF16)
    w1_b = w_mlp_in.astype(BF16)
    w2_b = w_mlp_out.astype(BF16)
    g1 = norm1_g.reshape(1, d)
    g2n = norm2_g.reshape(1, d)
    qg = jnp.tile(q_norm_g.reshape(1, HEAD_DIM), (1, N_HEADS))
    kg = jnp.tile(k_norm_g.reshape(1, HEAD_DIM), (1, N_HEADS))

    n_c = nbatch + db
    n_pad = -(-n_c // SUBLANES) * SUBLANES
    c_all = jnp.concatenate([c_prompt, c_sample, jnp.zeros((n_pad - n_c, d), F32)], axis=0)
    mod = _ada(c_all, w_ada, b_ada)
    mod_p = mod[:nbatch].reshape(nbatch, 1, N_MOD, d)
    mod_s = jnp.broadcast_to(mod[nbatch:n_c].reshape(db, 1, N_MOD, d), (db, dec_seq, N_MOD, d))
    mod_s = mod_s.reshape(1, ntok, N_MOD, d)
    mp = [mod_p[:, :, i] for i in range(N_MOD)]
    msn = [mod_s[:, :, i] for i in range(N_MOD)]

    biases = [_group_bias(rel_bias, g) for g in range(N_GROUPS)]

    cm, sga, qkv, utail = _in_proj(x_prompt, mp[0], mp[1], g1, w_in_b, conv_w, wco_b, qg, kg,
                                   tm=TM_IN, dils=DILS)
    p_conv = utail[:, SUBLANES - (CONV_WIDTH - 1):, :]
    p_kv = []
    for g, (window, dil) in enumerate(DILATED_GROUPS):
        keep = min(window, ns)
        assert keep % dil == 0
        for part in (1, 2):
            tail = qkv[3 * g + part][:, :, (ns - keep) // dil:, :]
            p_kv.append(tail.transpose(0, 2, 1, 3).astype(F32).reshape(nbatch, keep, N_HEADS, HEAD_DIM))
    xs = x_sample.reshape(1, ntok, d)
    zero_row = jnp.zeros((db, 1, D_CONV), F32)
    hist1 = jnp.concatenate([state_conv[:, 1:2]] + [zero_row] * (dec_seq - 1), axis=1).reshape(1, ntok, D_CONV)
    hist2 = jnp.concatenate([state_conv[:, 0:1], state_conv[:, 1:2]] + [zero_row] * (dec_seq - 2),
                            axis=1).reshape(1, ntok, D_CONV)
    cm_s, sga_s, qkv_s, u_s = _in_proj(xs, msn[0], msn[1], g1, w_in_b, conv_w, wco_b, qg, kg, tm=ntok,
                                       dils=(1,) * N_GROUPS, hist1=hist1, hist2=hist2, dec_seq=dec_seq)
    s_conv = u_s.reshape(db, dec_seq, D_CONV)[:, dec_seq - (CONV_WIDTH - 1):]

    def pack_cols(a):
        return a.reshape(ntok // DEC_PACK, DEC_PACK, D_ATT).transpose(0, 2, 1)

    fused = N_GROUPS - 1
    os_s, lses_s, s_kv = [], [], []
    for g, (window, dil) in enumerate(DILATED_GROUPS):
        ck, cv = caches[2 * g], caches[2 * g + 1]
        width = ck.shape[1]
        assert width == window
        kt = ck.transpose(0, 2, 3, 1).reshape(db, D_ATT, width)
        vt = cv.transpose(0, 2, 3, 1).reshape(db, D_ATT, width)
        dec_args = (qkv_s[3 * g].reshape(db, dec_seq, D_ATT), pack_cols(qkv_s[3 * g + 1]),
                    pack_cols(qkv_s[3 * g + 2]), kt, vt, _decode_bias_table(biases[g], width, dil, dec_seq))
        if g == fused:
            prompt_att, (skt, svt, o_s, lse_s) = _attention_and_cache(
                qkv, [_prompt_bias_table(b) for b in biases], *dec_args, dec_seq=dec_seq)
        else:
            skt, svt, o_s, lse_s = _decode_attention(*dec_args, dec_seq=dec_seq)
        s_kv.append(skt.reshape(db, N_HEADS, HEAD_DIM, width).transpose(0, 3, 1, 2))
        s_kv.append(svt.reshape(db, N_HEADS, HEAD_DIM, width).transpose(0, 3, 1, 2))
        os_s.append(o_s.reshape(1, 1, ntok, D_ATT).astype(BF16))
        lses_s.append(lse_s.reshape(1, 1, ntok, LANES))

    y_prompt = _out_proj(x_prompt, cm, sga, prompt_att[0::2], prompt_att[1::2], mp[2], mp[3], mp[4], mp[5], g2n,
                         wao_b, wo_b, w1_b, w2_b, tm=TM_OUT)
    y_sample = _out_proj(xs, cm_s, sga_s, os_s, lses_s, msn[2], msn[3], msn[4], msn[5], g2n,
                         wao_b, wo_b, w1_b, w2_b, tm=ntok).reshape(db, dec_seq, d)
    return y_prompt, y_sample, [p_conv] + p_kv, [s_conv] + s_kv


def kernel(x_prompt, x_sample, c_prompt, c_sample, state_conv, cache_k1, cache_v1, cache_k2, cache_v2,
           cache_k3, cache_v3, rel_bias, norm1_g, norm2_g, w_ada, b_ada, w_in, conv_w, q_norm_g, k_norm_g,
           w_conv_out, w_attn_out, w_o, w_mlp_in, w_mlp_out):
    depth = w_in.shape[0]
    caches = (cache_k1, cache_v1, cache_k2, cache_v2, cache_k3, cache_v3)
    yp, ys = x_prompt, x_sample
    p_states = [[] for _ in range(1 + 2 * N_GROUPS)]
    s_states = [[] for _ in range(1 + 2 * N_GROUPS)]
    for l in range(depth):
        yp, ys, p_new, s_new = _layer(
            yp, ys, c_prompt, c_sample, state_conv[l], [c[l] for c in caches], rel_bias,
            norm1_g[l], norm2_g[l], w_ada[l], b_ada[l], w_in[l], conv_w[l], q_norm_g[l], k_norm_g[l],
            w_conv_out[l], w_attn_out[l], w_o[l], w_mlp_in[l], w_mlp_out[l])
        for lst, a in zip(p_states, p_new):
            lst.append(a)
        for lst, a in zip(s_states, s_new):
            lst.append(a)
    p_out = [jnp.stack(a) for a in p_states]
    s_out = [jnp.stack(a) for a in s_states]
    return (yp, ys, *p_out, *s_out)
```

```python
import functools
import math

import numpy as np
import jax
import jax.numpy as jnp
from jax import lax
from jax.experimental import pallas as pl
from jax.experimental.pallas import tpu as pltpu

F32 = jnp.float32
BF16 = jnp.bfloat16

D_MODEL = 1024
D_CONV = D_MODEL
CONV_WIDTH = 3
HEAD_DIM = 64
N_HEADS = 8
D_ATT = N_HEADS * HEAD_DIM
DILATED_GROUPS = ((128, 1), (512, 4), (2048, 16))
N_GROUPS = len(DILATED_GROUPS)
DILS = tuple(d for _, d in DILATED_GROUPS)
D_QKV = N_GROUPS * D_ATT
D_FF = 4 * D_MODEL
N_BUCKETS = 32
MAX_DISTANCE = 2048
ATT_BLOCK = 128
WK = 128
N_MOD = 6
RMS_EPS = 1e-6
ATT_SCALE = HEAD_DIM ** -0.5
NEG = -1e30

OFF_H, OFF_B, OFF_C = 0, D_CONV, 2 * D_CONV
OFF_Q = 3 * D_CONV
OFF_K = OFF_Q + D_QKV
OFF_V = OFF_K + D_QKV
OFF_GC = OFF_V + D_QKV
OFF_GA = OFF_GC + D_MODEL
D_PROJ = OFF_GA + D_MODEL

LANES = 128
SUBLANES = 8
MXU_DIM = 256
VMEM_LIMIT_BYTES = 56 * 1024 * 1024

TM_IN = 512
TM_OUT = 512
FF_CHUNK = 512
DEC_PACK = LANES
DEC_BLOCK_BYTES = 4 * 1024 * 1024

assert all(w // d == WK for w, d in DILATED_GROUPS)


def _mm(a, b):
    return jnp.dot(a, b, preferred_element_type=F32)


def _const_spec(shape):
    nd = len(shape)
    return pl.BlockSpec(shape, lambda *_: (0,) * nd, pipeline_mode=pl.Buffered(1))


def _ada_kernel(c_ref, w_ref, b_ref, o_ref):
    c = c_ref[...]
    s = c * jax.nn.sigmoid(c)
    o_ref[...] = _mm(s.astype(BF16), w_ref[...].astype(BF16)) + b_ref[...]


def _ada(c_all, w_ada, b_ada):
    n, d = c_all.shape
    nout = w_ada.shape[1]
    tn = 1024
    return pl.pallas_call(
        _ada_kernel,
        grid=(nout // tn,),
        in_specs=[pl.BlockSpec((n, d), lambda j: (0, 0)),
                  pl.BlockSpec((d, tn), lambda j: (0, j)),
                  pl.BlockSpec((1, tn), lambda j: (0, j))],
        out_specs=pl.BlockSpec((n, tn), lambda j: (0, j)),
        out_shape=jax.ShapeDtypeStruct((n, nout), F32),
        compiler_params=pltpu.CompilerParams(dimension_semantics=("arbitrary",),
                                             vmem_limit_bytes=VMEM_LIMIT_BYTES),
        name="ada",
    )(c_all, w_ada, b_ada.reshape(1, nout))


def _in_kernel(*refs, decode, tm, dec_seq, dils):
    n_in = 11 if decode else 9
    x_ref, sh_ref, sc_ref, g1_ref, win_ref, cw_ref, wco_ref, qg_ref, kg_ref = refs[:9]
    if decode:
        s1_ref, s2_ref = refs[9:11]
    cm_ref, sga_ref = refs[n_in:n_in + 2]
    qkv_refs = refs[n_in + 2:n_in + 2 + 3 * N_GROUPS]
    ustate_ref, uext, stg = refs[n_in + 2 + 3 * N_GROUPS:]

    x = x_ref[0]
    ms = jnp.mean(x * x, axis=-1, keepdims=True)
    xn = x * lax.rsqrt(ms + RMS_EPS) * g1_ref[...]
    xn = (xn * (1.0 + sc_ref[0]) + sh_ref[0]).astype(BF16)

    h = _mm(xn, win_ref[:, OFF_H:OFF_H + D_CONV])
    c = _mm(xn, win_ref[:, OFF_C:OFF_C + D_CONV])
    u = c * h
    hist = SUBLANES
    if decode:
        uext[0:hist, :] = jnp.zeros((hist, D_CONV), F32)
    else:
        @pl.when(pl.program_id(1) == 0)
        def _():
            uext[0:hist, :] = jnp.zeros((hist, D_CONV), F32)
    uext[hist:hist + tm, :] = u
    um1 = uext[hist - 1:hist - 1 + tm, :]
    um2 = uext[hist - 2:hist - 2 + tm, :]
    if decode:
        t = lax.broadcasted_iota(jnp.int32, (tm, 1), 0) % dec_seq
        um1 = jnp.where(t >= 1, um1, s1_ref[0])
        um2 = jnp.where(t >= 2, um2, s2_ref[0])
        ustate_ref[0] = u
    else:
        uext[0:hist, :] = uext[tm:tm + hist, :]
        ustate_ref[0] = u[tm - hist:tm, :]
    y = cw_ref[0:1, :] * um2 + cw_ref[1:2, :] * um1 + cw_ref[2:3, :] * u
    bg = _mm(xn, win_ref[:, OFF_B:OFF_B + D_CONV])
    conv_out = _mm((bg * y).astype(BF16), wco_ref[...])
    gc = _mm(xn, win_ref[:, OFF_GC:OFF_GC + D_MODEL])
    cm_ref[0] = (jax.nn.sigmoid(gc) * conv_out).astype(cm_ref.dtype)
    ga = _mm(xn, win_ref[:, OFF_GA:OFF_GA + D_MODEL])
    sga_ref[0] = jax.nn.sigmoid(ga).astype(sga_ref.dtype)

    seg_r = lax.broadcasted_iota(jnp.int32, (MXU_DIM, MXU_DIM), 0) // HEAD_DIM
    seg_c = lax.broadcasted_iota(jnp.int32, (MXU_DIM, MXU_DIM), 1) // HEAD_DIM
    seg = jnp.where(seg_r == seg_c, 1.0, 0.0).astype(BF16)

    def put(out_ref, val, dil, part):
        if dil == 1:
            out_ref[0, 0] = val.astype(out_ref.dtype)
            return
        for i in range(D_ATT // LANES):
            stg[part, i] = val[:, i * LANES:(i + 1) * LANES]
        for r in range(dil):
            for i in range(D_ATT // LANES):
                out_ref[0, r, :, i * LANES:(i + 1) * LANES] = (
                    stg[part, i, pl.ds(r, tm // dil, stride=dil), :].astype(out_ref.dtype))

    def head_norm(a, gain):
        sq = (a * a).astype(BF16)
        ss = jnp.concatenate([_mm(sq[:, i * MXU_DIM:(i + 1) * MXU_DIM], seg)
                              for i in range(D_ATT // MXU_DIM)], axis=1)
        return a * lax.rsqrt(ss * (1.0 / HEAD_DIM) + RMS_EPS) * gain

    for g in range(N_GROUPS):
        q_ref, k_ref, v_ref = qkv_refs[3 * g:3 * g + 3]
        lo = g * D_ATT
        q = _mm(xn, win_ref[:, OFF_Q + lo:OFF_Q + lo + D_ATT])
        put(q_ref, head_norm(q, qg_ref[...] * ATT_SCALE), dils[g], 0)
        k = _mm(xn, win_ref[:, OFF_K + lo:OFF_K + lo + D_ATT])
        put(k_ref, head_norm(k, kg_ref[...]), dils[g], 1)
        put(v_ref, _mm(xn, win_ref[:, OFF_V + lo:OFF_V + lo + D_ATT]), dils[g], 2)


def _in_proj(x, shift1, scale1, g1, w_in, conv_w, w_conv_out, qg, kg, *, tm, dils, hist1=None, hist2=None,
             dec_seq=1):
    nb, ns, d = x.shape
    decode = hist1 is not None
    assert ns % tm == 0 and all(tm % (dil * 2 * SUBLANES) == 0 for dil in dils)
    nmod = shift1.shape[1]
    tmod = 1 if nmod == 1 else tm
    row = lambda b, s: (b, s, 0)
    mod_map = (lambda b, s: (b, 0, 0)) if nmod == 1 else row
    act_dtype = F32 if decode else BF16
    in_specs = [pl.BlockSpec((1, tm, d), row),
                pl.BlockSpec((1, tmod, d), mod_map),
                pl.BlockSpec((1, tmod, d), mod_map),
                _const_spec((1, d)),
                _const_spec((d, D_PROJ)),
                _const_spec((CONV_WIDTH, D_CONV)),
                _const_spec((D_CONV, D_MODEL)),
                _const_spec((1, D_ATT)),
                _const_spec((1, D_ATT))]
    args = [x, shift1, scale1, g1, w_in, conv_w, w_conv_out, qg, kg]
    if decode:
        in_specs += [pl.BlockSpec((1, tm, D_CONV), row), pl.BlockSpec((1, tm, D_CONV), row)]
        args += [hist1, hist2]
    out_shape = [jax.ShapeDtypeStruct((nb, ns, D_MODEL), BF16),
                 jax.ShapeDtypeStruct((nb, ns, D_MODEL), BF16)]
    out_specs = [pl.BlockSpec((1, tm, D_MODEL), row),
                 pl.BlockSpec((1, tm, D_MODEL), row)]
    for dil in dils:
        for _ in range(3):
            out_shape.append(jax.ShapeDtypeStruct((nb, dil, ns // dil, D_ATT), act_dtype))
            out_specs.append(pl.BlockSpec((1, dil, tm // dil, D_ATT), lambda b, s: (b, 0, s, 0)))
    if decode:
        out_shape.append(jax.ShapeDtypeStruct((nb, ns, D_CONV), F32))
        out_specs.append(pl.BlockSpec((1, tm, D_CONV), row))
    else:
        out_shape.append(jax.ShapeDtypeStruct((nb, SUBLANES, D_CONV), F32))
        out_specs.append(pl.BlockSpec((1, SUBLANES, D_CONV), lambda b, s: (b, 0, 0)))
    outs = pl.pallas_call(
        functools.partial(_in_kernel, decode=decode, tm=tm, dec_seq=dec_seq, dils=tuple(dils)),
        grid=(nb, ns // tm),
        in_specs=in_specs,
        out_specs=out_specs,
        out_shape=out_shape,
        scratch_shapes=[pltpu.VMEM((tm + SUBLANES, D_CONV), F32),
                        pltpu.VMEM((3, D_ATT // LANES, tm, LANES), F32)],
        compiler_params=pltpu.CompilerParams(dimension_semantics=("arbitrary", "arbitrary"),
                                             vmem_limit_bytes=VMEM_LIMIT_BYTES),
        name="in_proj_decode" if decode else "in_proj",
    )(*args)
    return outs[0], outs[1], outs[2:2 + 3 * N_GROUPS], outs[-1]


HEADS_PER_PASS = MXU_DIM // HEAD_DIM


def _attn_blocks(q_ref, kc_ref, kp_ref, vc_ref, vp_ref, tb_ref, o_ref, lse_ref, *, nb, first):
    blk = ATT_BLOCK
    lane = lax.broadcasted_iota(jnp.int32, (1, MXU_DIM), 1)
    hmask = [(lane >= HEAD_DIM * h) & (lane < HEAD_DIM * (h + 1)) for h in range(HEADS_PER_PASS)]
    col = lax.broadcasted_iota(jnp.int32, (1, 2 * blk), 1)
    lane_out = lax.broadcasted_iota(jnp.int32, (1, LANES), 1)
    for n in range(nb):
        rows = slice(n * blk, (n + 1) * blk)
        lse_acc = jnp.zeros((blk, LANES), F32)
        for hp in range(N_HEADS // HEADS_PER_PASS):
            cols = slice(hp * MXU_DIM, (hp + 1) * MXU_DIM)
            q4 = q_ref[0, 0, rows, cols]
            lhs = jnp.concatenate([jnp.where(hmask[h], q4, jnp.zeros_like(q4))
                                   for h in range(HEADS_PER_PASS)], axis=0)
            if n == 0:
                k_prev, v_prev = kp_ref[0, 0, :, cols], vp_ref[0, 0, :, cols]
            else:
                prev = slice((n - 1) * blk, n * blk)
                k_prev, v_prev = kc_ref[0, 0, prev, cols], vc_ref[0, 0, prev, cols]
            kk = jnp.concatenate([k_prev, kc_ref[0, 0, rows, cols]], axis=0)
            vv = jnp.concatenate([v_prev, vc_ref[0, 0, rows, cols]], axis=0)
            s = lax.dot_general(lhs, kk, (((1,), (1,)), ((), ())), preferred_element_type=F32)
            s = s + tb_ref[hp * HEADS_PER_PASS * blk:(hp + 1) * HEADS_PER_PASS * blk, :]
            if n == 0:
                s = jnp.where(first & (col < blk), NEG, s)
            m = jnp.max(s, axis=-1, keepdims=True)
            p = jnp.exp(s - m)
            l = jnp.sum(p, axis=-1, keepdims=True)
            pv = _mm(p.astype(BF16), vv) * (1.0 / l)
            lse = m + jnp.log(l)
            o4 = jnp.zeros((blk, MXU_DIM), F32)
            for h in range(HEADS_PER_PASS):
                hr = slice(h * blk, (h + 1) * blk)
                o4 = jnp.where(hmask[h], pv[hr, :], o4)
                lse_acc = jnp.where(lane_out == hp * HEADS_PER_PASS + h, lse[hr, :], lse_acc)
            o_ref[0, 0, rows, cols] = o4.astype(o_ref.dtype)
        lse_ref[0, 0, rows, :] = lse_acc


def _decode_one(q, k_tail, v_tail, kt_ref, vt_ref, bias_ref, skt_ref, svt_ref, i, *, width, dec_seq):
    nch = width // LANES
    keep = LANES - dec_seq
    lane = lax.broadcasted_iota(jnp.int32, (1, LANES), 1)
    nrow = dec_seq * N_HEADS
    own = (lax.broadcasted_iota(jnp.int32, (nrow, D_ATT), 0) % N_HEADS
           == lax.broadcasted_iota(jnp.int32, (nrow, D_ATT), 1) // HEAD_DIM)
    q_rows = jnp.concatenate([jnp.broadcast_to(q[t:t + 1, :], (N_HEADS, D_ATT)) for t in range(dec_seq)], axis=0)
    q_bd = jnp.where(own, q_rows, 0.0).astype(BF16)

    kc = kt_ref[i]
    vc = vt_ref[i]
    s_c = _mm(q_bd, kc.astype(BF16)) + bias_ref[:, 0:width]
    s_t = _mm(q_bd, k_tail.astype(BF16)) + bias_ref[:, width:width + LANES]
    m = jnp.maximum(jnp.max(s_c, axis=-1, keepdims=True), jnp.max(s_t, axis=-1, keepdims=True))
    p_c = jnp.exp(s_c - m)
    p_t = jnp.exp(s_t - m)
    l = jnp.sum(p_c, axis=-1, keepdims=True) + jnp.sum(p_t, axis=-1, keepdims=True)
    nt = (((1,), (1,)), ((), ()))
    o = (lax.dot_general(p_c.astype(BF16), vc.astype(BF16), nt, preferred_element_type=F32)
         + lax.dot_general(p_t.astype(BF16), v_tail.astype(BF16), nt, preferred_element_type=F32))
    o = jnp.where(own, o, 0.0) * (1.0 / l)
    o_tok = jnp.sum(o.reshape(dec_seq, N_HEADS, D_ATT), axis=1)
    lse = m + jnp.log(l)
    head_lane = (lax.broadcasted_iota(jnp.int32, (nrow, LANES), 0) % N_HEADS
                 == lax.broadcasted_iota(jnp.int32, (nrow, LANES), 1))
    lse_tok = jnp.sum(jnp.where(head_lane, lse, 0.0).reshape(dec_seq, N_HEADS, LANES), axis=1)

    for src_ref, tail, dst_ref in ((kt_ref, k_tail, skt_ref), (vt_ref, v_tail, svt_ref)):
        cur = pltpu.roll(src_ref[i, :, 0:LANES], keep, 1)
        for c in range(nch):
            nxt = pltpu.roll(src_ref[i, :, (c + 1) * LANES:(c + 2) * LANES], keep, 1) if c + 1 < nch else tail
            dst_ref[i, :, c * LANES:(c + 1) * LANES] = jnp.where(lane < keep, cur, nxt)
            cur = nxt
    return o_tok, lse_tok


def _dec_kernel(q_ref, knp_ref, vnp_ref, kt_ref, vt_ref, bias_ref, skt_ref, svt_ref, o_ref, lse_ref,
                *, width, dec_seq, bb):
    step = pl.program_id(0)
    pack = DEC_PACK // dec_seq
    keep = LANES - dec_seq

    def one_batch(i, carry):
        slot = ((step * bb + i) % pack) * dec_seq
        to_tail = (keep + LANES - slot) % LANES
        k_tail = pltpu.roll(knp_ref[0], to_tail, 1)
        v_tail = pltpu.roll(vnp_ref[0], to_tail, 1)
        o_tok, lse_tok = _decode_one(q_ref[i], k_tail, v_tail, kt_ref, vt_ref, bias_ref, skt_ref, svt_ref, i,
                                     width=width, dec_seq=dec_seq)
        o_ref[i] = o_tok
        lse_ref[i] = lse_tok
        return carry

    lax.fori_loop(0, bb, one_batch, 0)


def _decode_attention(q, knp, vnp, kt, vt, bias, *, dec_seq):
    db, _, width = kt.shape
    pack = DEC_PACK // dec_seq
    assert db % pack == 0 and width % LANES == 0
    bb = max(1, min(pack, DEC_BLOCK_BYTES // (D_ATT * width * 4)))
    assert pack % bb == 0
    per_b = lambda s: (s, 0, 0)
    packed = lambda s: (s * bb // pack, 0, 0)
    return pl.pallas_call(
        functools.partial(_dec_kernel, width=width, dec_seq=dec_seq, bb=bb),
        grid=(db // bb,),
        in_specs=[pl.BlockSpec((bb, dec_seq, D_ATT), per_b),
                  pl.BlockSpec((1, D_ATT, DEC_PACK), packed),
                  pl.BlockSpec((1, D_ATT, DEC_PACK), packed),
                  pl.BlockSpec((bb, D_ATT, width), per_b),
                  pl.BlockSpec((bb, D_ATT, width), per_b),
                  pl.BlockSpec(bias.shape, lambda s: (0, 0))],
        out_specs=[pl.BlockSpec((bb, D_ATT, width), per_b),
                   pl.BlockSpec((bb, D_ATT, width), per_b),
                   pl.BlockSpec((bb, dec_seq, D_ATT), per_b),
                   pl.BlockSpec((bb, dec_seq, LANES), per_b)],
        out_shape=[jax.ShapeDtypeStruct((db, D_ATT, width), F32),
                   jax.ShapeDtypeStruct((db, D_ATT, width), F32),
                   jax.ShapeDtypeStruct((db, dec_seq, D_ATT), F32),
                   jax.ShapeDtypeStruct((db, dec_seq, LANES), F32)],
        compiler_params=pltpu.CompilerParams(dimension_semantics=("arbitrary",),
                                             vmem_limit_bytes=VMEM_LIMIT_BYTES),
        name=f"decode_attn_w{width}",
    )(q, knp, vnp, kt, vt, bias)


def _attn_cache_kernel(*refs, nb, rows_blocks, width, dec_seq):
    dq_ref, knp_ref, vnp_ref, kt_ref, vt_ref, bias_ref = refs[:6]
    att_in = refs[6:6 + 6 * N_GROUPS]
    n_in = 6 + 6 * N_GROUPS
    skt_ref, svt_ref, do_ref, dlse_ref = refs[n_in:n_in + 4]
    att_out = refs[n_in + 4:]
    step = pl.program_id(0)
    pack = DEC_PACK // dec_seq
    slot = (step % pack) * dec_seq
    to_tail = (2 * LANES - dec_seq - slot) % LANES
    k_tail = pltpu.roll(knp_ref[0], to_tail, 1)
    v_tail = pltpu.roll(vnp_ref[0], to_tail, 1)
    o_tok, lse_tok = _decode_one(dq_ref[0], k_tail, v_tail, kt_ref, vt_ref, bias_ref, skt_ref, svt_ref, 0,
                                 width=width, dec_seq=dec_seq)
    do_ref[0] = o_tok
    dlse_ref[0] = lse_tok
    for g in range(N_GROUPS):
        first = (step * nb) % rows_blocks[g] == 0
        _attn_blocks(*att_in[6 * g:6 * g + 6], *att_out[2 * g:2 * g + 2], nb=nb, first=first)


def _attention_and_cache(qkv, tbs, dq, knp, vnp, kt, vt, bias, *, dec_seq):
    db, _, width = kt.shape
    pack = DEC_PACK // dec_seq
    nbatch = qkv[0].shape[0]
    total_blocks = nbatch * qkv[0].shape[1] * qkv[0].shape[2] // ATT_BLOCK
    assert db % pack == 0 and total_blocks % db == 0
    nb = total_blocks // db
    tq = nb * ATT_BLOCK
    per_b = lambda s: (s, 0, 0)
    packed = lambda s: (s // pack, 0, 0)
    in_specs = [pl.BlockSpec((1, dec_seq, D_ATT), per_b),
                pl.BlockSpec((1, D_ATT, DEC_PACK), packed),
                pl.BlockSpec((1, D_ATT, DEC_PACK), packed),
                pl.BlockSpec((1, D_ATT, width), per_b),
                pl.BlockSpec((1, D_ATT, width), per_b),
                pl.BlockSpec(bias.shape, lambda s: (0, 0))]
    args = [dq, knp, vnp, kt, vt, bias]
    out_specs = [pl.BlockSpec((1, D_ATT, width), per_b),
                 pl.BlockSpec((1, D_ATT, width), per_b),
                 pl.BlockSpec((1, dec_seq, D_ATT), per_b),
                 pl.BlockSpec((1, dec_seq, LANES), per_b)]
    out_shape = [jax.ShapeDtypeStruct((db, D_ATT, width), F32),
                 jax.ShapeDtypeStruct((db, D_ATT, width), F32),
                 jax.ShapeDtypeStruct((db, dec_seq, D_ATT), F32),
                 jax.ShapeDtypeStruct((db, dec_seq, LANES), F32)]
    rows_blocks = []
    for g in range(N_GROUPS):
        q, k, v = qkv[3 * g:3 * g + 3]
        _, dil, nl, _ = q.shape
        rb = nl // ATT_BLOCK
        assert nl % ATT_BLOCK == 0 and rb % nb == 0
        rows_blocks.append(rb)

        def cur(s, rb=rb, dil=dil):
            blk = s * nb
            return (blk // (dil * rb), (blk // rb) % dil, (blk % rb) // nb, 0)

        def prev(s, rb=rb, dil=dil):
            blk = s * nb
            return (blk // (dil * rb), (blk // rb) % dil, jnp.maximum(blk % rb - 1, 0), 0)

        in_specs += [pl.BlockSpec((1, 1, tq, D_ATT), cur),
                     pl.BlockSpec((1, 1, tq, D_ATT), cur),
                     pl.BlockSpec((1, 1, ATT_BLOCK, D_ATT), prev),
                     pl.BlockSpec((1, 1, tq, D_ATT), cur),
                     pl.BlockSpec((1, 1, ATT_BLOCK, D_ATT), prev),
                     pl.BlockSpec((N_HEADS * ATT_BLOCK, 2 * ATT_BLOCK), lambda s: (0, 0))]
        args += [q, k, k, v, v, tbs[g]]
        out_specs += [pl.BlockSpec((1, 1, tq, D_ATT), cur), pl.BlockSpec((1, 1, tq, LANES), cur)]
        out_shape += [jax.ShapeDtypeStruct(q.shape, BF16),
                      jax.ShapeDtypeStruct(q.shape[:3] + (LANES,), F32)]
    outs = pl.pallas_call(
        functools.partial(_attn_cache_kernel, nb=nb, rows_blocks=tuple(rows_blocks), width=width,
                          dec_seq=dec_seq),
        grid=(db,),
        in_specs=in_specs,
        out_specs=out_specs,
        out_shape=out_shape,
        compiler_params=pltpu.CompilerParams(dimension_semantics=("arbitrary",),
                                             vmem_limit_bytes=VMEM_LIMIT_BYTES),
        name="attn_and_cache",
    )(*args)
    return outs[4:], outs[:4]


def _out_kernel(*refs, tm, dils):
    x_ref, cm_ref, sga_ref = refs[:3]
    o_refs = refs[3:3 + N_GROUPS]
    l_refs = refs[3 + N_GROUPS:3 + 2 * N_GROUPS]
    (g1_ref, sh2_ref, sc2_ref, g2_ref, n2_ref, wao_ref, wo_ref, w1_ref, w2_ref,
     y_ref, ostg, lstg) = refs[3 + 2 * N_GROUPS:]

    def token_order(ref, stg, dil):
        if dil == 1:
            return ref[0, 0].astype(F32)
        n_slab = ref.shape[-1] // LANES
        for r in range(dil):
            blk = ref[0, r].astype(F32)
            for i in range(n_slab):
                stg[i, pl.ds(r, tm // dil, stride=dil), :] = blk[:, i * LANES:(i + 1) * LANES]
        return jnp.concatenate([stg[i] for i in range(n_slab)], axis=1)

    lses = [token_order(l_refs[g], lstg, dils[g]) for g in range(N_GROUPS)]
    mx = jnp.maximum(jnp.maximum(lses[0], lses[1]), lses[2])
    es = [jnp.exp(l - mx) for l in lses]
    inv = 1.0 / (es[0] + es[1] + es[2])
    er = lax.broadcasted_iota(jnp.int32, (LANES, D_ATT), 0)
    ec = lax.broadcasted_iota(jnp.int32, (LANES, D_ATT), 1) // HEAD_DIM
    expand = jnp.where(er == ec, 1.0, 0.0).astype(BF16)
    o = None
    for g in range(N_GROUPS):
        w = es[g] * inv
        w_hi = w.astype(BF16)
        w_lo = (w - w_hi.astype(F32)).astype(BF16)
        wexp = _mm(w_hi, expand) + _mm(w_lo, expand)
        term = wexp * token_order(o_refs[g], ostg, dils[g])
        o = term if o is None else o + term
    attn_out = _mm(o.astype(BF16), wao_ref[...])
    mixed = cm_ref[0].astype(F32) + sga_ref[0].astype(F32) * attn_out
    x1 = x_ref[0] + g1_ref[0] * _mm(mixed.astype(BF16), wo_ref[...])
    ms = jnp.mean(x1 * x1, axis=-1, keepdims=True)
    xn2 = x1 * lax.rsqrt(ms + RMS_EPS) * n2_ref[...]
    xn2 = (xn2 * (1.0 + sc2_ref[0]) + sh2_ref[0]).astype(BF16)
    acc = None
    for f in range(D_FF // FF_CHUNK):
        cols = slice(f * FF_CHUNK, (f + 1) * FF_CHUNK)
        hid = jnp.maximum(_mm(xn2, w1_ref[:, cols]), 0.0)
        part = _mm((hid * hid).astype(BF16), w2_ref[cols, :])
        acc = part if acc is None else acc + part
    y_ref[0] = x1 + g2_ref[0] * acc


def _out_proj(x, cm, sga, os, lses, gate1, shift2, scale2, gate2, n2, wao, wo, w1, w2, *, tm):
    nb, ns, d = x.shape
    dils = tuple(o.shape[1] for o in os)
    assert ns % tm == 0 and all(tm % (dil * 2 * SUBLANES) == 0 for dil in dils)
    nmod = gate1.shape[1]
    tmod = 1 if nmod == 1 else tm
    row = lambda b, s: (b, s, 0)
    mod_map = (lambda b, s: (b, 0, 0)) if nmod == 1 else row
    mod_spec = pl.BlockSpec((1, tmod, d), mod_map)
    res_spec = lambda dil, n: pl.BlockSpec((1, dil, tm // dil, n), lambda b, s: (b, 0, s, 0))
    in_specs = ([pl.BlockSpec((1, tm, d), row)] * 3
                + [res_spec(dil, D_ATT) for dil in dils]
                + [res_spec(dil, LANES) for dil in dils]
                + [mod_spec] * 4
                + [_const_spec((1, d)), _const_spec((D_ATT, d)), _const_spec((d, d)),
                   _const_spec((d, D_FF)), _const_spec((D_FF, d))])
    return pl.pallas_call(
        functools.partial(_out_kernel, tm=tm, dils=dils),
        grid=(nb, ns // tm),
        in_specs=in_specs,
        out_specs=pl.BlockSpec((1, tm, d), row),
        out_shape=jax.ShapeDtypeStruct((nb, ns, d), F32),
        scratch_shapes=[pltpu.VMEM((D_ATT // LANES, tm, LANES), F32), pltpu.VMEM((1, tm, LANES), F32)],
        compiler_params=pltpu.CompilerParams(dimension_semantics=("arbitrary", "arbitrary"),
                                             vmem_limit_bytes=VMEM_LIMIT_BYTES),
        name="out_proj",
    )(x, cm, sga, *os, *lses, gate1, shift2, scale2, gate2, n2, wao, wo, w1, w2)


def _tail_kernel(k_ref, v_ref, ko_ref, vo_ref, stg, *, dil, rows):
    for src, dst in ((k_ref, ko_ref), (v_ref, vo_ref)):
        if dil == 1:
            x = src[0, 0].astype(F32)
        else:
            for r in range(dil):
                blk = src[0, r].astype(F32)
                for i in range(D_ATT // LANES):
                    stg[i, pl.ds(r, rows // dil, stride=dil), :] = blk[:, i * LANES:(i + 1) * LANES]
            x = jnp.concatenate([stg[i] for i in range(D_ATT // LANES)], axis=1)
        dst[0] = x.T


def _state_tail(k, v, keep):
    nbatch, dil, nl, _ = k.shape
    rows = min(keep, TM_OUT)
    assert keep % rows == 0 and (nl * dil - keep) % rows == 0 and rows % (dil * 2 * SUBLANES) == 0
    first = (nl * dil - keep) // rows
    src = pl.BlockSpec((1, dil, rows // dil, D_ATT), lambda b, j: (b, 0, first + j, 0))
    dst = pl.BlockSpec((1, D_ATT, rows), lambda b, j: (b, 0, j))
    return pl.pallas_call(
        functools.partial(_tail_kernel, dil=dil, rows=rows),
        grid=(nbatch, keep // rows),
        in_specs=[src, src],
        out_specs=[dst, dst],
        out_shape=[jax.ShapeDtypeStruct((nbatch, D_ATT, keep), F32)] * 2,
        scratch_shapes=[pltpu.VMEM((D_ATT // LANES, rows, LANES), F32)],
        compiler_params=pltpu.CompilerParams(dimension_semantics=("arbitrary", "arbitrary"),
                                             vmem_limit_bytes=VMEM_LIMIT_BYTES),
        name=f"state_tail_d{dil}",
    )(k, v)


def _t5_causal_bucket(dist):
    max_exact = N_BUCKETS // 2
    ratio = jnp.maximum(dist, 1).astype(F32) / max_exact
    large = max_exact + (jnp.log(ratio) / math.log(MAX_DISTANCE / max_exact)
                         * (N_BUCKETS - max_exact)).astype(jnp.int32)
    large = jnp.minimum(large, N_BUCKETS - 1)
    return jnp.where(dist < max_exact, dist, large)


def _group_bias(rel_bias, g):
    steps = jnp.arange(WK + 1)
    bucket = _t5_causal_bucket(steps * DILS[g])
    onehot = (bucket[:, None] == jnp.arange(N_BUCKETS)[None, :]).astype(F32)
    b = jnp.dot(onehot, rel_bias[:, g * N_HEADS:(g + 1) * N_HEADS].astype(F32), precision=lax.Precision.HIGHEST)
    return b.T


def _prompt_bias_table(bias):
    blk, period = ATT_BLOCK, 4 * ATT_BLOCK
    rev = bias[:, ::-1]
    row = jnp.concatenate([rev, jnp.full((N_HEADS, period - (WK + 1)), NEG, F32)], axis=1)
    tiled = jnp.tile(row, (1, blk))[:, :blk * (period - 1)].reshape(N_HEADS, blk, period - 1)
    return tiled[:, :, :2 * blk].reshape(N_HEADS * blk, 2 * blk)


def _decode_bias_table(bias, width, dil, dec_seq):
    assert width == WK * dil and (dil == 1 or dil >= dec_seq)
    rev = bias[:, ::-1]
    neg_col = jnp.full((N_HEADS, 1), NEG, F32)
    rows = []
    for t in range(dec_seq):
        if dil == 1:
            cache = jnp.pad(rev[:, :width - t], ((0, 0), (t, 0)), constant_values=NEG)
        else:
            mine = (np.arange(dil) == t)[None, None, :]
            cache = jnp.where(mine, rev[:, :WK, None], NEG).reshape(N_HEADS, width)
        new = [bias[:, (t - tn) // dil:(t - tn) // dil + 1] if tn <= t and (t - tn) % dil == 0 else neg_col
               for tn in range(dec_seq)]
        rows.append(jnp.concatenate([cache, jnp.full((N_HEADS, LANES - dec_seq), NEG, F32)] + new, axis=1))
    return jnp.concatenate(rows, axis=0)


def _layer(x_prompt, x_sample, c_prompt, c_sample, state_conv, caches, rel_bias, norm1_g, norm2_g,
           w_ada, b_ada, w_in, conv_w, q_norm_g, k_norm_g, w_conv_out, w_attn_out, w_o, w_mlp_in,
           w_mlp_out):
    nbatch, ns, d = x_prompt.shape
    db, dec_seq, _ = x_sample.shape
    ntok = db * dec_seq

    w_in_b = w_in.astype(BF16)
    wco_b = w_conv_out.astype(BF16)
    wao_b = w_attn_out.astype(BF16)
    wo_b = w_o.astype(BF16)
    w1_b = w_mlp_in.astype(BF16)
    w2_b = w_mlp_out.astype(BF16)
    g1 = norm1_g.reshape(1, d)
    g2n = norm2_g.reshape(1, d)
    qg = jnp.tile(q_norm_g.reshape(1, HEAD_DIM), (1, N_HEADS))
    kg = jnp.tile(k_norm_g.reshape(1, HEAD_DIM), (1, N_HEADS))

    n_c = nbatch + db
    n_pad = -(-n_c // SUBLANES) * SUBLANES
    c_all = jnp.concatenate([c_prompt, c_sample, jnp.zeros((n_pad - n_c, d), F32)], axis=0)
    mod = _ada(c_all, w_ada, b_ada)
    mod_p = mod[:nbatch].reshape(nbatch, 1, N_MOD, d)
    mod_s = jnp.broadcast_to(mod[nbatch:n_c].reshape(db, 1, N_MOD, d), (db, dec_seq, N_MOD, d))
    mod_s = mod_s.reshape(1, ntok, N_MOD, d)
    mp = [mod_p[:, :, i] for i in range(N_MOD)]
    msn = [mod_s[:, :, i] for i in range(N_MOD)]

    biases = [_group_bias(rel_bias, g) for g in range(N_GROUPS)]

    cm, sga, qkv, utail = _in_proj(x_prompt, mp[0], mp[1], g1, w_in_b, conv_w, wco_b, qg, kg,
                                   tm=TM_IN, dils=DILS)
    p_conv = utail[:, SUBLANES - (CONV_WIDTH - 1):, :]
    p_kv = []
    for g, (window, dil) in enumerate(DILATED_GROUPS):
        keep = min(window, ns)
        for tail in _state_tail(qkv[3 * g + 1], qkv[3 * g + 2], keep):
            p_kv.append(tail.reshape(nbatch, N_HEADS, HEAD_DIM, keep).transpose(0, 3, 1, 2))
    xs = x_sample.reshape(1, ntok, d)
    zero_row = jnp.zeros((db, 1, D_CONV), F32)
    hist1 = jnp.concatenate([state_conv[:, 1:2]] + [zero_row] * (dec_seq - 1), axis=1).reshape(1, ntok, D_CONV)
    hist2 = jnp.concatenate([state_conv[:, 0:1], state_conv[:, 1:2]] + [zero_row] * (dec_seq - 2),
                            axis=1).reshape(1, ntok, D_CONV)
    cm_s, sga_s, qkv_s, u_s = _in_proj(xs, msn[0], msn[1], g1, w_in_b, conv_w, wco_b, qg, kg, tm=ntok,
                                       dils=(1,) * N_GROUPS, hist1=hist1, hist2=hist2, dec_seq=dec_seq)
    s_conv = u_s.reshape(db, dec_seq, D_CONV)[:, dec_seq - (CONV_WIDTH - 1):]

    def pack_cols(a):
        return a.reshape(ntok // DEC_PACK, DEC_PACK, D_ATT).transpose(0, 2, 1)

    fused = N_GROUPS - 1
    os_s, lses_s, s_kv = [], [], []
    for g, (window, dil) in enumerate(DILATED_GROUPS):
        ck, cv = caches[2 * g], caches[2 * g + 1]
        width = ck.shape[1]
        assert width == window
        kt = ck.transpose(0, 2, 3, 1).reshape(db, D_ATT, width)
        vt = cv.transpose(0, 2, 3, 1).reshape(db, D_ATT, width)
        dec_args = (qkv_s[3 * g].reshape(db, dec_seq, D_ATT), pack_cols(qkv_s[3 * g + 1]),
                    pack_cols(qkv_s[3 * g + 2]), kt, vt, _decode_bias_table(biases[g], width, dil, dec_seq))
        if g == fused:
            prompt_att, (skt, svt, o_s, lse_s) = _attention_and_cache(
                qkv, [_prompt_bias_table(b) for b in biases], *dec_args, dec_seq=dec_seq)
        else:
            skt, svt, o_s, lse_s = _decode_attention(*dec_args, dec_seq=dec_seq)
        s_kv.append(skt.reshape(db, N_HEADS, HEAD_DIM, width).transpose(0, 3, 1, 2))
        s_kv.append(svt.reshape(db, N_HEADS, HEAD_DIM, width).transpose(0, 3, 1, 2))
        os_s.append(o_s.reshape(1, 1, ntok, D_ATT).astype(BF16))
        lses_s.append(lse_s.reshape(1, 1, ntok, LANES))

    y_prompt = _out_proj(x_prompt, cm, sga, prompt_att[0::2], prompt_att[1::2], mp[2], mp[3], mp[4], mp[5], g2n,
                         wao_b, wo_b, w1_b, w2_b, tm=TM_OUT)
    y_sample = _out_proj(xs, cm_s, sga_s, os_s, lses_s, msn[2], msn[3], msn[4], msn[5], g2n,
                         wao_b, wo_b, w1_b, w2_b, tm=ntok).reshape(db, dec_seq, d)
    return y_prompt, y_sample, [p_conv] + p_kv, [s_conv] + s_kv


def kernel(x_prompt, x_sample, c_prompt, c_sample, state_conv, cache_k1, cache_v1, cache_k2, cache_v2,
           cache_k3, cache_v3, rel_bias, norm1_g, norm2_g, w_ada, b_ada, w_in, conv_w, q_norm_g, k_norm_g,
           w_conv_out, w_attn_out, w_o, w_mlp_in, w_mlp_out):
    depth = w_in.shape[0]
    caches = (cache_k1, cache_v1, cache_k2, cache_v2, cache_k3, cache_v3)
    yp, ys = x_prompt, x_sample
    p_states = [[] for _ in range(1 + 2 * N_GROUPS)]
    s_states = [[] for _ in range(1 + 2 * N_GROUPS)]
    for l in range(depth):
        yp, ys, p_new, s_new = _layer(
            yp, ys, c_prompt, c_sample, state_conv[l], [c[l] for c in caches], rel_bias,
            norm1_g[l], norm2_g[l], w_ada[l], b_ada[l], w_in[l], conv_w[l], q_norm_g[l], k_norm_g[l],
            w_conv_out[l], w_attn_out[l], w_o[l], w_mlp_in[l], w_mlp_out[l])
        for lst, a in zip(p_states, p_new):
            lst.append(a)
        for lst, a in zip(s_states, s_new):
            lst.append(a)
    p_out = [jnp.stack(a) for a in p_states]
    s_out = [jnp.stack(a) for a in s_states]
    return (yp, ys, *p_out, *s_out)
```

```python
import functools
import math

import numpy as np
import jax
import jax.numpy as jnp
from jax import lax
from jax.experimental import pallas as pl
from jax.experimental.pallas import tpu as pltpu

F32 = jnp.float32
BF16 = jnp.bfloat16

D_MODEL = 1024
D_CONV = D_MODEL
CONV_WIDTH = 3
HEAD_DIM = 64
N_HEADS = 8
D_ATT = N_HEADS * HEAD_DIM
DILATED_GROUPS = ((128, 1), (512, 4), (2048, 16))
N_GROUPS = len(DILATED_GROUPS)
DILS = tuple(d for _, d in DILATED_GROUPS)
D_QKV = N_GROUPS * D_ATT
D_FF = 4 * D_MODEL
N_BUCKETS = 32
MAX_DISTANCE = 2048
ATT_BLOCK = 128
WK = 128
N_MOD = 6
RMS_EPS = 1e-6
ATT_SCALE = HEAD_DIM ** -0.5
NEG = -1e30

OFF_H, OFF_B, OFF_C = 0, D_CONV, 2 * D_CONV
OFF_Q = 3 * D_CONV
OFF_K = OFF_Q + D_QKV
OFF_V = OFF_K + D_QKV
OFF_GC = OFF_V + D_QKV
OFF_GA = OFF_GC + D_MODEL
D_PROJ = OFF_GA + D_MODEL

LANES = 128
SUBLANES = 8
MXU_DIM = 256
VMEM_LIMIT_BYTES = 60 * 1024 * 1024

TM_IN = 512
TM_OUT = 512
FF_CHUNK = 512
DEC_PACK = LANES

assert all(w // d == WK for w, d in DILATED_GROUPS)


def _mm(a, b):
    return jnp.dot(a, b, preferred_element_type=F32)


def _const_spec(shape):
    nd = len(shape)
    return pl.BlockSpec(shape, lambda *_: (0,) * nd, pipeline_mode=pl.Buffered(1))


def _ada_kernel(c_ref, w_ref, b_ref, o_ref):
    c = c_ref[...]
    s = c * jax.nn.sigmoid(c)
    o_ref[...] = _mm(s.astype(BF16), w_ref[...].astype(BF16)) + b_ref[...]


def _ada(c_all, w_ada, b_ada):
    n, d = c_all.shape
    nout = w_ada.shape[1]
    tn = 1024
    return pl.pallas_call(
        _ada_kernel,
        grid=(nout // tn,),
        in_specs=[pl.BlockSpec((n, d), lambda j: (0, 0)),
                  pl.BlockSpec((d, tn), lambda j: (0, j)),
                  pl.BlockSpec((1, tn), lambda j: (0, j))],
        out_specs=pl.BlockSpec((n, tn), lambda j: (0, j)),
        out_shape=jax.ShapeDtypeStruct((n, nout), F32),
        compiler_params=pltpu.CompilerParams(dimension_semantics=("arbitrary",),
                                             vmem_limit_bytes=VMEM_LIMIT_BYTES),
        name="ada",
    )(c_all, w_ada, b_ada.reshape(1, nout))


def _in_kernel(*refs, decode, tm, dec_seq, dils, cache_kw):
    n_in = 11 if decode else 9
    x_ref, sh_ref, sc_ref, g1_ref, win_ref, cw_ref, wco_ref, qg_ref, kg_ref = refs[:9]
    if decode:
        s1_ref, s2_ref = refs[9:11]
    n_out = 3 + 3 * N_GROUPS
    if cache_kw is not None:
        step = pl.program_id(0) * pl.num_programs(1) + pl.program_id(1)
        _cache_step(step, refs[n_in:n_in + N_CACHE_IN],
                    refs[n_in + N_CACHE_IN + n_out:n_in + N_CACHE_IN + n_out + N_CACHE_OUT], **cache_kw)
        n_in += N_CACHE_IN
    cm_ref, sga_ref = refs[n_in:n_in + 2]
    qkv_refs = refs[n_in + 2:n_in + 2 + 3 * N_GROUPS]
    ustate_ref = refs[n_in + n_out - 1]
    uext, stg = refs[-2:]

    x = x_ref[0]
    ms = jnp.mean(x * x, axis=-1, keepdims=True)
    xn = x * lax.rsqrt(ms + RMS_EPS) * g1_ref[...]
    xn = (xn * (1.0 + sc_ref[0]) + sh_ref[0]).astype(BF16)

    h = _mm(xn, win_ref[:, OFF_H:OFF_H + D_CONV])
    c = _mm(xn, win_ref[:, OFF_C:OFF_C + D_CONV])
    u = c * h
    hist = SUBLANES
    if decode:
        uext[0:hist, :] = jnp.zeros((hist, D_CONV), F32)
    else:
        @pl.when(pl.program_id(1) == 0)
        def _():
            uext[0:hist, :] = jnp.zeros((hist, D_CONV), F32)
    uext[hist:hist + tm, :] = u
    um1 = uext[hist - 1:hist - 1 + tm, :]
    um2 = uext[hist - 2:hist - 2 + tm, :]
    if decode:
        t = lax.broadcasted_iota(jnp.int32, (tm, 1), 0) % dec_seq
        um1 = jnp.where(t >= 1, um1, s1_ref[0])
        um2 = jnp.where(t >= 2, um2, s2_ref[0])
        ustate_ref[0] = u
    else:
        uext[0:hist, :] = uext[tm:tm + hist, :]
        ustate_ref[0] = u[tm - hist:tm, :]
    y = cw_ref[0:1, :] * um2 + cw_ref[1:2, :] * um1 + cw_ref[2:3, :] * u
    bg = _mm(xn, win_ref[:, OFF_B:OFF_B + D_CONV])
    conv_out = _mm((bg * y).astype(BF16), wco_ref[...])
    gc = _mm(xn, win_ref[:, OFF_GC:OFF_GC + D_MODEL])
    cm_ref[0] = (jax.nn.sigmoid(gc) * conv_out).astype(cm_ref.dtype)
    ga = _mm(xn, win_ref[:, OFF_GA:OFF_GA + D_MODEL])
    sga_ref[0] = jax.nn.sigmoid(ga).astype(sga_ref.dtype)

    seg_r = lax.broadcasted_iota(jnp.int32, (MXU_DIM, MXU_DIM), 0) // HEAD_DIM
    seg_c = lax.broadcasted_iota(jnp.int32, (MXU_DIM, MXU_DIM), 1) // HEAD_DIM
    seg = jnp.where(seg_r == seg_c, 1.0, 0.0).astype(BF16)

    def put(out_ref, val, dil, part):
        if dil == 1:
            out_ref[0, 0] = val.astype(out_ref.dtype)
            return
        for i in range(D_ATT // LANES):
            stg[part, i] = val[:, i * LANES:(i + 1) * LANES]
        for r in range(dil):
            for i in range(D_ATT // LANES):
                out_ref[0, r, :, i * LANES:(i + 1) * LANES] = (
                    stg[part, i, pl.ds(r, tm // dil, stride=dil), :].astype(out_ref.dtype))

    def head_norm(a, gain):
        sq = (a * a).astype(BF16)
        ss = jnp.concatenate([_mm(sq[:, i * MXU_DIM:(i + 1) * MXU_DIM], seg)
                              for i in range(D_ATT // MXU_DIM)], axis=1)
        return a * lax.rsqrt(ss * (1.0 / HEAD_DIM) + RMS_EPS) * gain

    for g in range(N_GROUPS):
        q_ref, k_ref, v_ref = qkv_refs[3 * g:3 * g + 3]
        lo = g * D_ATT
        q = _mm(xn, win_ref[:, OFF_Q + lo:OFF_Q + lo + D_ATT])
        put(q_ref, head_norm(q, qg_ref[...] * ATT_SCALE), dils[g], 0)
        k = _mm(xn, win_ref[:, OFF_K + lo:OFF_K + lo + D_ATT])
        put(k_ref, head_norm(k, kg_ref[...]), dils[g], 1)
        put(v_ref, _mm(xn, win_ref[:, OFF_V + lo:OFF_V + lo + D_ATT]), dils[g], 2)


def _in_proj(x, shift1, scale1, g1, w_in, conv_w, w_conv_out, qg, kg, *, tm, dils, hist1=None, hist2=None,
             dec_seq=1, cache_args=None):
    nb, ns, d = x.shape
    decode = hist1 is not None
    assert ns % tm == 0 and all(tm % (dil * 2 * SUBLANES) == 0 for dil in dils)
    nmod = shift1.shape[1]
    tmod = 1 if nmod == 1 else tm
    row = lambda b, s: (b, s, 0)
    mod_map = (lambda b, s: (b, 0, 0)) if nmod == 1 else row
    act_dtype = F32 if decode else BF16
    in_specs = [pl.BlockSpec((1, tm, d), row),
                pl.BlockSpec((1, tmod, d), mod_map),
                pl.BlockSpec((1, tmod, d), mod_map),
                _const_spec((1, d)),
                _const_spec((d, D_PROJ)),
                _const_spec((CONV_WIDTH, D_CONV)),
                _const_spec((D_CONV, D_MODEL)),
                _const_spec((1, D_ATT)),
                _const_spec((1, D_ATT))]
    args = [x, shift1, scale1, g1, w_in, conv_w, w_conv_out, qg, kg]
    if decode:
        in_specs += [pl.BlockSpec((1, tm, D_CONV), row), pl.BlockSpec((1, tm, D_CONV), row)]
        args += [hist1, hist2]
    out_shape = [jax.ShapeDtypeStruct((nb, ns, D_MODEL), BF16),
                 jax.ShapeDtypeStruct((nb, ns, D_MODEL), BF16)]
    out_specs = [pl.BlockSpec((1, tm, D_MODEL), row),
                 pl.BlockSpec((1, tm, D_MODEL), row)]
    for dil in dils:
        for _ in range(3):
            out_shape.append(jax.ShapeDtypeStruct((nb, dil, ns // dil, D_ATT), act_dtype))
            out_specs.append(pl.BlockSpec((1, dil, tm // dil, D_ATT), lambda b, s: (b, 0, s, 0)))
    if decode:
        out_shape.append(jax.ShapeDtypeStruct((nb, ns, D_CONV), F32))
        out_specs.append(pl.BlockSpec((1, tm, D_CONV), row))
    else:
        out_shape.append(jax.ShapeDtypeStruct((nb, SUBLANES, D_CONV), F32))
        out_specs.append(pl.BlockSpec((1, SUBLANES, D_CONV), lambda b, s: (b, 0, 0)))
    n_out = len(out_specs)
    cache_kw = None
    if cache_args is not None:
        nst = ns // tm
        c_in, c_out, c_shape, cache_kw = _cache_specs(cache_args, nb * nst, lambda b, s: b * nst + s)
        in_specs += c_in
        args += list(cache_args)
        out_specs += c_out
        out_shape += c_shape
    outs = pl.pallas_call(
        functools.partial(_in_kernel, decode=decode, tm=tm, dec_seq=dec_seq, dils=tuple(dils),
                          cache_kw=cache_kw),
        grid=(nb, ns // tm),
        in_specs=in_specs,
        out_specs=out_specs,
        out_shape=out_shape,
        scratch_shapes=[pltpu.VMEM((tm + SUBLANES, D_CONV), F32),
                        pltpu.VMEM((3, D_ATT // LANES, tm, LANES), F32)],
        compiler_params=pltpu.CompilerParams(dimension_semantics=("arbitrary", "arbitrary"),
                                             vmem_limit_bytes=VMEM_LIMIT_BYTES),
        name="in_proj_decode" if decode else "in_proj",
    )(*args)
    return outs[0], outs[1], outs[2:2 + 3 * N_GROUPS], outs[n_out - 1], outs[n_out:]


HEADS_PER_PASS = MXU_DIM // HEAD_DIM


def _attn_blocks(q_ref, kc_ref, kp_ref, vc_ref, vp_ref, tb_ref, o_ref, lse_ref, *, nb, first):
    blk = ATT_BLOCK
    lane = lax.broadcasted_iota(jnp.int32, (1, MXU_DIM), 1)
    hmask = [(lane >= HEAD_DIM * h) & (lane < HEAD_DIM * (h + 1)) for h in range(HEADS_PER_PASS)]
    col = lax.broadcasted_iota(jnp.int32, (1, 2 * blk), 1)
    lane_out = lax.broadcasted_iota(jnp.int32, (1, LANES), 1)
    for n in range(nb):
        rows = slice(n * blk, (n + 1) * blk)
        lse_acc = jnp.zeros((blk, LANES), F32)
        for hp in range(N_HEADS // HEADS_PER_PASS):
            cols = slice(hp * MXU_DIM, (hp + 1) * MXU_DIM)
            q4 = q_ref[0, 0, rows, cols]
            lhs = jnp.concatenate([jnp.where(hmask[h], q4, jnp.zeros_like(q4))
                                   for h in range(HEADS_PER_PASS)], axis=0)
            if n == 0:
                k_prev, v_prev = kp_ref[0, 0, :, cols], vp_ref[0, 0, :, cols]
            else:
                prev = slice((n - 1) * blk, n * blk)
                k_prev, v_prev = kc_ref[0, 0, prev, cols], vc_ref[0, 0, prev, cols]
            kk = jnp.concatenate([k_prev, kc_ref[0, 0, rows, cols]], axis=0)
            vv = jnp.concatenate([v_prev, vc_ref[0, 0, rows, cols]], axis=0)
            s = lax.dot_general(lhs, kk, (((1,), (1,)), ((), ())), preferred_element_type=F32)
            s = s + tb_ref[hp * HEADS_PER_PASS * blk:(hp + 1) * HEADS_PER_PASS * blk, :]
            if n == 0:
                s = jnp.where(first & (col < blk), NEG, s)
            m = jnp.max(s, axis=-1, keepdims=True)
            p = jnp.exp(s - m)
            l = jnp.sum(p, axis=-1, keepdims=True)
            pv = _mm(p.astype(BF16), vv) * (1.0 / l)
            lse = m + jnp.log(l)
            o4 = jnp.zeros((blk, MXU_DIM), F32)
            for h in range(HEADS_PER_PASS):
                hr = slice(h * blk, (h + 1) * blk)
                o4 = jnp.where(hmask[h], pv[hr, :], o4)
                lse_acc = jnp.where(lane_out == hp * HEADS_PER_PASS + h, lse[hr, :], lse_acc)
            o_ref[0, 0, rows, cols] = o4.astype(o_ref.dtype)
        lse_ref[0, 0, rows, :] = lse_acc


def _decode_one(q, k_tail, v_tail, kt_ref, vt_ref, bias_ref, skt_ref, svt_ref, i, *, width, dec_seq):
    nch = width // LANES
    keep = LANES - dec_seq
    lane = lax.broadcasted_iota(jnp.int32, (1, LANES), 1)
    nrow = dec_seq * N_HEADS
    own = (lax.broadcasted_iota(jnp.int32, (nrow, D_ATT), 0) % N_HEADS
           == lax.broadcasted_iota(jnp.int32, (nrow, D_ATT), 1) // HEAD_DIM)
    q_rows = jnp.concatenate([jnp.broadcast_to(q[t:t + 1, :], (N_HEADS, D_ATT)) for t in range(dec_seq)], axis=0)
    q_bd = jnp.where(own, q_rows, 0.0).astype(BF16)

    kc = kt_ref[i]
    vc = vt_ref[i]
    s_c = _mm(q_bd, kc.astype(BF16)) + bias_ref[:, 0:width]
    s_t = _mm(q_bd, k_tail.astype(BF16)) + bias_ref[:, width:width + LANES]
    m = jnp.maximum(jnp.max(s_c, axis=-1, keepdims=True), jnp.max(s_t, axis=-1, keepdims=True))
    p_c = jnp.exp(s_c - m)
    p_t = jnp.exp(s_t - m)
    l = jnp.sum(p_c, axis=-1, keepdims=True) + jnp.sum(p_t, axis=-1, keepdims=True)
    nt = (((1,), (1,)), ((), ()))
    o = (lax.dot_general(p_c.astype(BF16), vc.astype(BF16), nt, preferred_element_type=F32)
         + lax.dot_general(p_t.astype(BF16), v_tail.astype(BF16), nt, preferred_element_type=F32))
    o = jnp.where(own, o, 0.0) * (1.0 / l)
    o_tok = jnp.sum(o.reshape(dec_seq, N_HEADS, D_ATT), axis=1)
    lse = m + jnp.log(l)
    head_lane = (lax.broadcasted_iota(jnp.int32, (nrow, LANES), 0) % N_HEADS
                 == lax.broadcasted_iota(jnp.int32, (nrow, LANES), 1))
    lse_tok = jnp.sum(jnp.where(head_lane, lse, 0.0).reshape(dec_seq, N_HEADS, LANES), axis=1)

    for src_ref, tail, dst_ref in ((kt_ref, k_tail, skt_ref), (vt_ref, v_tail, svt_ref)):
        cur = pltpu.roll(src_ref[i, :, 0:LANES], keep, 1)
        for c in range(nch):
            nxt = pltpu.roll(src_ref[i, :, (c + 1) * LANES:(c + 2) * LANES], keep, 1) if c + 1 < nch else tail
            dst_ref[i, :, c * LANES:(c + 1) * LANES] = jnp.where(lane < keep, cur, nxt)
            cur = nxt
    return o_tok, lse_tok


N_CACHE_IN, N_CACHE_OUT = 6, 4


def _cache_specs(cache_args, n_steps, step_of):
    q, _, _, kt, _, bias = cache_args
    db, dec_seq, _ = q.shape
    width = kt.shape[-1]
    pack = DEC_PACK // dec_seq
    assert db % n_steps == 0 and db % pack == 0 and width % LANES == 0
    bb = db // n_steps
    assert pack % bb == 0
    per_b = lambda *i: (step_of(*i), 0, 0)
    packed = lambda *i: (step_of(*i) * bb // pack, 0, 0)
    in_specs = [pl.BlockSpec((bb, dec_seq, D_ATT), per_b),
                pl.BlockSpec((1, D_ATT, DEC_PACK), packed),
                pl.BlockSpec((1, D_ATT, DEC_PACK), packed),
                pl.BlockSpec((bb, D_ATT, width), per_b),
                pl.BlockSpec((bb, D_ATT, width), per_b),
                pl.BlockSpec(bias.shape, lambda *i: (0, 0))]
    out_specs = [pl.BlockSpec((bb, D_ATT, width), per_b),
                 pl.BlockSpec((bb, D_ATT, width), per_b),
                 pl.BlockSpec((bb, dec_seq, D_ATT), per_b),
                 pl.BlockSpec((bb, dec_seq, LANES), per_b)]
    out_shape = [jax.ShapeDtypeStruct((db, D_ATT, width), F32),
                 jax.ShapeDtypeStruct((db, D_ATT, width), F32),
                 jax.ShapeDtypeStruct((db, dec_seq, D_ATT), F32),
                 jax.ShapeDtypeStruct((db, dec_seq, LANES), F32)]
    return in_specs, out_specs, out_shape, dict(width=width, dec_seq=dec_seq, bb=bb)


def _cache_step(step, in_refs, out_refs, *, width, dec_seq, bb):
    q_ref, knp_ref, vnp_ref, kt_ref, vt_ref, bias_ref = in_refs
    skt_ref, svt_ref, o_ref, lse_ref = out_refs
    pack = DEC_PACK // dec_seq

    def one_batch(i, carry):
        slot = ((step * bb + i) % pack) * dec_seq
        to_tail = (2 * LANES - dec_seq - slot) % LANES
        k_tail = pltpu.roll(knp_ref[0], to_tail, 1)
        v_tail = pltpu.roll(vnp_ref[0], to_tail, 1)
        o_tok, lse_tok = _decode_one(q_ref[i], k_tail, v_tail, kt_ref, vt_ref, bias_ref, skt_ref, svt_ref, i,
                                     width=width, dec_seq=dec_seq)
        o_ref[i] = o_tok
        lse_ref[i] = lse_tok
        return carry

    if bb <= 2:
        for i in range(bb):
            one_batch(i, 0)
    else:
        lax.fori_loop(0, bb, one_batch, 0)


def _attn_cache_kernel(*refs, nb, rows_blocks, cache_kw):
    n_in = N_CACHE_IN + 6 * N_GROUPS
    att_in = refs[N_CACHE_IN:n_in]
    att_out = refs[n_in + N_CACHE_OUT:]
    step = pl.program_id(0)
    _cache_step(step, refs[:N_CACHE_IN], refs[n_in:n_in + N_CACHE_OUT], **cache_kw)
    for g in range(N_GROUPS):
        first = (step * nb) % rows_blocks[g] == 0
        _attn_blocks(*att_in[6 * g:6 * g + 6], *att_out[2 * g:2 * g + 2], nb=nb, first=first)


def _attention_and_cache(qkv, tbs, cache_args):
    db = cache_args[0].shape[0]
    nbatch = qkv[0].shape[0]
    total_blocks = nbatch * qkv[0].shape[1] * qkv[0].shape[2] // ATT_BLOCK
    assert total_blocks % db == 0
    nb = total_blocks // db
    tq = nb * ATT_BLOCK
    in_specs, out_specs, out_shape, cache_kw = _cache_specs(cache_args, db, lambda s: s)
    args = list(cache_args)
    rows_blocks = []
    for g in range(N_GROUPS):
        q, k, v = qkv[3 * g:3 * g + 3]
        _, dil, nl, _ = q.shape
        rb = nl // ATT_BLOCK
        assert nl % ATT_BLOCK == 0 and rb % nb == 0
        rows_blocks.append(rb)

        def cur(s, rb=rb, dil=dil):
            blk = s * nb
            return (blk // (dil * rb), (blk // rb) % dil, (blk % rb) // nb, 0)

        def prev(s, rb=rb, dil=dil):
            blk = s * nb
            return (blk // (dil * rb), (blk // rb) % dil, jnp.maximum(blk % rb - 1, 0), 0)

        in_specs += [pl.BlockSpec((1, 1, tq, D_ATT), cur),
                     pl.BlockSpec((1, 1, tq, D_ATT), cur),
                     pl.BlockSpec((1, 1, ATT_BLOCK, D_ATT), prev),
                     pl.BlockSpec((1, 1, tq, D_ATT), cur),
                     pl.BlockSpec((1, 1, ATT_BLOCK, D_ATT), prev),
                     pl.BlockSpec((N_HEADS * ATT_BLOCK, 2 * ATT_BLOCK), lambda s: (0, 0))]
        args += [q, k, k, v, v, tbs[g]]
        out_specs += [pl.BlockSpec((1, 1, tq, D_ATT), cur), pl.BlockSpec((1, 1, tq, LANES), cur)]
        out_shape += [jax.ShapeDtypeStruct(q.shape, BF16),
                      jax.ShapeDtypeStruct(q.shape[:3] + (LANES,), F32)]
    outs = pl.pallas_call(
        functools.partial(_attn_cache_kernel, nb=nb, rows_blocks=tuple(rows_blocks), cache_kw=cache_kw),
        grid=(db,),
        in_specs=in_specs,
        out_specs=out_specs,
        out_shape=out_shape,
        compiler_params=pltpu.CompilerParams(dimension_semantics=("arbitrary",),
                                             vmem_limit_bytes=VMEM_LIMIT_BYTES),
        name="attn_and_cache",
    )(*args)
    return outs[N_CACHE_OUT:], outs[:N_CACHE_OUT]


def _out_kernel(*refs, tm, dils, cache_kw):
    x_ref, cm_ref, sga_ref = refs[:3]
    o_refs = refs[3:3 + N_GROUPS]
    l_refs = refs[3 + N_GROUPS:3 + 2 * N_GROUPS]
    n_in = 3 + 2 * N_GROUPS + 9
    g1_ref, sh2_ref, sc2_ref, g2_ref, n2_ref, wao_ref, wo_ref, w1_ref, w2_ref = refs[3 + 2 * N_GROUPS:n_in]
    if cache_kw is not None:
        step = pl.program_id(0) * pl.num_programs(1) + pl.program_id(1)
        _cache_step(step, refs[n_in:n_in + N_CACHE_IN],
                    refs[n_in + N_CACHE_IN + 1:n_in + N_CACHE_IN + 1 + N_CACHE_OUT], **cache_kw)
        n_in += N_CACHE_IN
    y_ref = refs[n_in]
    ostg, lstg = refs[-2:]

    def token_order(ref, stg, dil):
        if dil == 1:
            return ref[0, 0].astype(F32)
        n_slab = ref.shape[-1] // LANES
        for r in range(dil):
            blk = ref[0, r].astype(F32)
            for i in range(n_slab):
                stg[i, pl.ds(r, tm // dil, stride=dil), :] = blk[:, i * LANES:(i + 1) * LANES]
        return jnp.concatenate([stg[i] for i in range(n_slab)], axis=1)

    lses = [token_order(l_refs[g], lstg, dils[g]) for g in range(N_GROUPS)]
    mx = jnp.maximum(jnp.maximum(lses[0], lses[1]), lses[2])
    es = [jnp.exp(l - mx) for l in lses]
    inv = 1.0 / (es[0] + es[1] + es[2])
    er = lax.broadcasted_iota(jnp.int32, (LANES, D_ATT), 0)
    ec = lax.broadcasted_iota(jnp.int32, (LANES, D_ATT), 1) // HEAD_DIM
    expand = jnp.where(er == ec, 1.0, 0.0).astype(BF16)
    o = None
    for g in range(N_GROUPS):
        w = es[g] * inv
        w_hi = w.astype(BF16)
        w_lo = (w - w_hi.astype(F32)).astype(BF16)
        wexp = _mm(w_hi, expand) + _mm(w_lo, expand)
        term = wexp * token_order(o_refs[g], ostg, dils[g])
        o = term if o is None else o + term
    attn_out = _mm(o.astype(BF16), wao_ref[...])
    mixed = cm_ref[0].astype(F32) + sga_ref[0].astype(F32) * attn_out
    x1 = x_ref[0] + g1_ref[0] * _mm(mixed.astype(BF16), wo_ref[...])
    ms = jnp.mean(x1 * x1, axis=-1, keepdims=True)
    xn2 = x1 * lax.rsqrt(ms + RMS_EPS) * n2_ref[...]
    xn2 = (xn2 * (1.0 + sc2_ref[0]) + sh2_ref[0]).astype(BF16)
    acc = None
    for f in range(D_FF // FF_CHUNK):
        cols = slice(f * FF_CHUNK, (f + 1) * FF_CHUNK)
        hid = jnp.maximum(_mm(xn2, w1_ref[:, cols]), 0.0)
        part = _mm((hid * hid).astype(BF16), w2_ref[cols, :])
        acc = part if acc is None else acc + part
    y_ref[0] = x1 + g2_ref[0] * acc


def _out_proj(x, cm, sga, os, lses, gate1, shift2, scale2, gate2, n2, wao, wo, w1, w2, *, tm, cache_args=None):
    nb, ns, d = x.shape
    dils = tuple(o.shape[1] for o in os)
    assert ns % tm == 0 and all(tm % (dil * 2 * SUBLANES) == 0 for dil in dils)
    nmod = gate1.shape[1]
    tmod = 1 if nmod == 1 else tm
    row = lambda b, s: (b, s, 0)
    mod_map = (lambda b, s: (b, 0, 0)) if nmod == 1 else row
    mod_spec = pl.BlockSpec((1, tmod, d), mod_map)
    res_spec = lambda dil, n: pl.BlockSpec((1, dil, tm // dil, n), lambda b, s: (b, 0, s, 0))
    in_specs = ([pl.BlockSpec((1, tm, d), row)] * 3
                + [res_spec(dil, D_ATT) for dil in dils]
                + [res_spec(dil, LANES) for dil in dils]
                + [mod_spec] * 4
                + [_const_spec((1, d)), _const_spec((D_ATT, d)), _const_spec((d, d)),
                   _const_spec((d, D_FF)), _const_spec((D_FF, d))])
    args = [x, cm, sga, *os, *lses, gate1, shift2, scale2, gate2, n2, wao, wo, w1, w2]
    out_specs = [pl.BlockSpec((1, tm, d), row)]
    out_shape = [jax.ShapeDtypeStruct((nb, ns, d), F32)]
    cache_kw = None
    if cache_args is not None:
        nst = ns // tm
        c_in, c_out, c_shape, cache_kw = _cache_specs(cache_args, nb * nst, lambda b, s: b * nst + s)
        in_specs += c_in
        args += list(cache_args)
        out_specs += c_out
        out_shape += c_shape
    outs = pl.pallas_call(
        functools.partial(_out_kernel, tm=tm, dils=dils, cache_kw=cache_kw),
        grid=(nb, ns // tm),
        in_specs=in_specs,
        out_specs=out_specs,
        out_shape=out_shape,
        scratch_shapes=[pltpu.VMEM((D_ATT // LANES, tm, LANES), F32), pltpu.VMEM((1, tm, LANES), F32)],
        compiler_params=pltpu.CompilerParams(dimension_semantics=("arbitrary", "arbitrary"),
                                             vmem_limit_bytes=VMEM_LIMIT_BYTES),
        name="out_proj",
    )(*args)
    return outs[0] if cache_args is None else outs


def _tail_kernel(k_ref, v_ref, ko_ref, vo_ref, stg, *, dil, rows):
    for src, dst in ((k_ref, ko_ref), (v_ref, vo_ref)):
        if dil == 1:
            x = src[0, 0].astype(F32)
        else:
            for r in range(dil):
                blk = src[0, r].astype(F32)
                for i in range(D_ATT // LANES):
                    stg[i, pl.ds(r, rows // dil, stride=dil), :] = blk[:, i * LANES:(i + 1) * LANES]
            x = jnp.concatenate([stg[i] for i in range(D_ATT // LANES)], axis=1)
        dst[0] = x.T


def _state_tail(k, v, keep):
    nbatch, dil, nl, _ = k.shape
    rows = min(keep, TM_OUT)
    assert keep % rows == 0 and (nl * dil - keep) % rows == 0 and rows % (dil * 2 * SUBLANES) == 0
    first = (nl * dil - keep) // rows
    src = pl.BlockSpec((1, dil, rows // dil, D_ATT), lambda b, j: (b, 0, first + j, 0))
    dst = pl.BlockSpec((1, D_ATT, rows), lambda b, j: (b, 0, j))
    return pl.pallas_call(
        functools.partial(_tail_kernel, dil=dil, rows=rows),
        grid=(nbatch, keep // rows),
        in_specs=[src, src],
        out_specs=[dst, dst],
        out_shape=[jax.ShapeDtypeStruct((nbatch, D_ATT, keep), F32)] * 2,
        scratch_shapes=[pltpu.VMEM((D_ATT // LANES, rows, LANES), F32)],
        compiler_params=pltpu.CompilerParams(dimension_semantics=("arbitrary", "arbitrary"),
                                             vmem_limit_bytes=VMEM_LIMIT_BYTES),
        name=f"state_tail_d{dil}",
    )(k, v)


def _t5_causal_bucket(dist):
    max_exact = N_BUCKETS // 2
    ratio = jnp.maximum(dist, 1).astype(F32) / max_exact
    large = max_exact + (jnp.log(ratio) / math.log(MAX_DISTANCE / max_exact)
                         * (N_BUCKETS - max_exact)).astype(jnp.int32)
    large = jnp.minimum(large, N_BUCKETS - 1)
    return jnp.where(dist < max_exact, dist, large)


def _group_bias(rel_bias, g):
    steps = jnp.arange(WK + 1)
    bucket = _t5_causal_bucket(steps * DILS[g])
    onehot = (bucket[:, None] == jnp.arange(N_BUCKETS)[None, :]).astype(F32)
    b = jnp.dot(onehot, rel_bias[:, g * N_HEADS:(g + 1) * N_HEADS].astype(F32), precision=lax.Precision.HIGHEST)
    return b.T


def _prompt_bias_table(bias):
    blk, period = ATT_BLOCK, 4 * ATT_BLOCK
    rev = bias[:, ::-1]
    row = jnp.concatenate([rev, jnp.full((N_HEADS, period - (WK + 1)), NEG, F32)], axis=1)
    tiled = jnp.tile(row, (1, blk))[:, :blk * (period - 1)].reshape(N_HEADS, blk, period - 1)
    return tiled[:, :, :2 * blk].reshape(N_HEADS * blk, 2 * blk)


def _decode_bias_table(bias, width, dil, dec_seq):
    assert width == WK * dil and (dil == 1 or dil >= dec_seq)
    rev = bias[:, ::-1]
    neg_col = jnp.full((N_HEADS, 1), NEG, F32)
    rows = []
    for t in range(dec_seq):
        if dil == 1:
            cache = jnp.pad(rev[:, :width - t], ((0, 0), (t, 0)), constant_values=NEG)
        else:
            mine = (np.arange(dil) == t)[None, None, :]
            cache = jnp.where(mine, rev[:, :WK, None], NEG).reshape(N_HEADS, width)
        new = [bias[:, (t - tn) // dil:(t - tn) // dil + 1] if tn <= t and (t - tn) % dil == 0 else neg_col
               for tn in range(dec_seq)]
        rows.append(jnp.concatenate([cache, jnp.full((N_HEADS, LANES - dec_seq), NEG, F32)] + new, axis=1))
    return jnp.concatenate(rows, axis=0)


def _layer(x_prompt, x_sample, c_prompt, c_sample, state_conv, caches, rel_bias, norm1_g, norm2_g,
           w_ada, b_ada, w_in, conv_w, q_norm_g, k_norm_g, w_conv_out, w_attn_out, w_o, w_mlp_in,
           w_mlp_out):
    nbatch, ns, d = x_prompt.shape
    db, dec_seq, _ = x_sample.shape
    ntok = db * dec_seq

    w_in_b = w_in.astype(BF16)
    wco_b = w_conv_out.astype(BF16)
    wao_b = w_attn_out.astype(BF16)
    wo_b = w_o.astype(BF16)
    w1_b = w_mlp_in.astype(BF16)
    w2_b = w_mlp_out.astype(BF16)
    g1 = norm1_g.reshape(1, d)
    g2n = norm2_g.reshape(1, d)
    qg = jnp.tile(q_norm_g.reshape(1, HEAD_DIM), (1, N_HEADS))
    kg = jnp.tile(k_norm_g.reshape(1, HEAD_DIM), (1, N_HEADS))

    n_c = nbatch + db
    n_pad = -(-n_c // SUBLANES) * SUBLANES
    c_all = jnp.concatenate([c_prompt, c_sample, jnp.zeros((n_pad - n_c, d), F32)], axis=0)
    mod = _ada(c_all, w_ada, b_ada)
    mod_p = mod[:nbatch].reshape(nbatch, 1, N_MOD, d)
    mod_s = jnp.broadcast_to(mod[nbatch:n_c].reshape(db, 1, N_MOD, d), (db, dec_seq, N_MOD, d))
    mod_s = mod_s.reshape(1, ntok, N_MOD, d)
    mp = [mod_p[:, :, i] for i in range(N_MOD)]
    msn = [mod_s[:, :, i] for i in range(N_MOD)]

    biases = [_group_bias(rel_bias, g) for g in range(N_GROUPS)]

    xs = x_sample.reshape(1, ntok, d)
    zero_row = jnp.zeros((db, 1, D_CONV), F32)
    hist1 = jnp.concatenate([state_conv[:, 1:2]] + [zero_row] * (dec_seq - 1), axis=1).reshape(1, ntok, D_CONV)
    hist2 = jnp.concatenate([state_conv[:, 0:1], state_conv[:, 1:2]] + [zero_row] * (dec_seq - 2),
                            axis=1).reshape(1, ntok, D_CONV)
    cm_s, sga_s, qkv_s, u_s, _ = _in_proj(xs, msn[0], msn[1], g1, w_in_b, conv_w, wco_b, qg, kg, tm=ntok,
                                          dils=(1,) * N_GROUPS, hist1=hist1, hist2=hist2, dec_seq=dec_seq)
    s_conv = u_s.reshape(db, dec_seq, D_CONV)[:, dec_seq - (CONV_WIDTH - 1):]

    def pack_cols(a):
        return a.reshape(ntok // DEC_PACK, DEC_PACK, D_ATT).transpose(0, 2, 1)

    cache_args = []
    for g, (window, dil) in enumerate(DILATED_GROUPS):
        ck, cv = caches[2 * g], caches[2 * g + 1]
        width = ck.shape[1]
        assert width == window
        kt = ck.transpose(0, 2, 3, 1).reshape(db, D_ATT, width)
        vt = cv.transpose(0, 2, 3, 1).reshape(db, D_ATT, width)
        cache_args.append((qkv_s[3 * g].reshape(db, dec_seq, D_ATT), pack_cols(qkv_s[3 * g + 1]),
                           pack_cols(qkv_s[3 * g + 2]), kt, vt,
                           _decode_bias_table(biases[g], width, dil, dec_seq)))
    by_width = sorted(range(N_GROUPS), key=lambda g: DILATED_GROUPS[g][0])
    on_in_proj, on_out_proj, on_attention = by_width[0], by_width[1], by_width[2]
    cache_out = [None] * N_GROUPS

    cm, sga, qkv, utail, cache_out[on_in_proj] = _in_proj(
        x_prompt, mp[0], mp[1], g1, w_in_b, conv_w, wco_b, qg, kg, tm=TM_IN, dils=DILS,
        cache_args=cache_args[on_in_proj])
    p_conv = utail[:, SUBLANES - (CONV_WIDTH - 1):, :]
    p_kv = []
    for g, (window, dil) in enumerate(DILATED_GROUPS):
        keep = min(window, ns)
        for tail in _state_tail(qkv[3 * g + 1], qkv[3 * g + 2], keep):
            p_kv.append(tail.reshape(nbatch, N_HEADS, HEAD_DIM, keep).transpose(0, 3, 1, 2))
    prompt_att, cache_out[on_attention] = _attention_and_cache(
        qkv, [_prompt_bias_table(b) for b in biases], cache_args[on_attention])
    y_prompt, *cache_out[on_out_proj] = _out_proj(
        x_prompt, cm, sga, prompt_att[0::2], prompt_att[1::2], mp[2], mp[3], mp[4], mp[5], g2n,
        wao_b, wo_b, w1_b, w2_b, tm=TM_OUT, cache_args=cache_args[on_out_proj])

    os_s, lses_s, s_kv = [], [], []
    for g, (window, _) in enumerate(DILATED_GROUPS):
        skt, svt, o_s, lse_s = cache_out[g]
        s_kv.append(skt.reshape(db, N_HEADS, HEAD_DIM, window).transpose(0, 3, 1, 2))
        s_kv.append(svt.reshape(db, N_HEADS, HEAD_DIM, window).transpose(0, 3, 1, 2))
        os_s.append(o_s.reshape(1, 1, ntok, D_ATT).astype(BF16))
        lses_s.append(lse_s.reshape(1, 1, ntok, LANES))
    y_sample = _out_proj(xs, cm_s, sga_s, os_s, lses_s, msn[2], msn[3], msn[4], msn[5], g2n,
                         wao_b, wo_b, w1_b, w2_b, tm=ntok).reshape(db, dec_seq, d)
    return y_prompt, y_sample, [p_conv] + p_kv, [s_conv] + s_kv


def kernel(x_prompt, x_sample, c_prompt, c_sample, state_conv, cache_k1, cache_v1, cache_k2, cache_v2,
           cache_k3, cache_v3, rel_bias, norm1_g, norm2_g, w_ada, b_ada, w_in, conv_w, q_norm_g, k_norm_g,
           w_conv_out, w_attn_out, w_o, w_mlp_in, w_mlp_out):
    depth = w_in.shape[0]
    caches = (cache_k1, cache_v1, cache_k2, cache_v2, cache_k3, cache_v3)
    yp, ys = x_prompt, x_sample
    p_states = [[] for _ in range(1 + 2 * N_GROUPS)]
    s_states = [[] for _ in range(1 + 2 * N_GROUPS)]
    for l in range(depth):
        yp, ys, p_new, s_new = _layer(
            yp, ys, c_prompt, c_sample, state_conv[l], [c[l] for c in caches], rel_bias,
            norm1_g[l], norm2_g[l], w_ada[l], b_ada[l], w_in[l], conv_w[l], q_norm_g[l], k_norm_g[l],
            w_conv_out[l], w_attn_out[l], w_o[l], w_mlp_in[l], w_mlp_out[l])
        for lst, a in zip(p_states, p_new):
            lst.append(a)
        for lst, a in zip(s_states, s_new):
            lst.append(a)
    p_out = [jnp.stack(a) for a in p_states]
    s_out = [jnp.stack(a) for a in s_states]
    return (yp, ys, *p_out, *s_out)
```

```python
import functools
import math

import numpy as np
import jax
import jax.numpy as jnp
from jax import lax
from jax.experimental import pallas as pl
from jax.experimental.pallas import tpu as pltpu

F32 = jnp.float32
BF16 = jnp.bfloat16

D_MODEL = 1024
D_CONV = D_MODEL
CONV_WIDTH = 3
HEAD_DIM = 64
N_HEADS = 8
D_ATT = N_HEADS * HEAD_DIM
DILATED_GROUPS = ((128, 1), (512, 4), (2048, 16))
N_GROUPS = len(DILATED_GROUPS)
DILS = tuple(d for _, d in DILATED_GROUPS)
D_QKV = N_GROUPS * D_ATT
D_FF = 4 * D_MODEL
N_BUCKETS = 32
MAX_DISTANCE = 2048
ATT_BLOCK = 128
WK = 128
N_MOD = 6
RMS_EPS = 1e-6
ATT_SCALE = HEAD_DIM ** -0.5
NEG = -1e30

OFF_H, OFF_B, OFF_C = 0, D_CONV, 2 * D_CONV
OFF_Q = 3 * D_CONV
OFF_K = OFF_Q + D_QKV
OFF_V = OFF_K + D_QKV
OFF_GC = OFF_V + D_QKV
OFF_GA = OFF_GC + D_MODEL
D_PROJ = OFF_GA + D_MODEL

LANES = 128
SUBLANES = 8
MXU_DIM = 256
VMEM_LIMIT_BYTES = 60 * 1024 * 1024

TM_IN = 512
TM_OUT = 512
FF_CHUNK = 512
DEC_PACK = LANES

assert all(w // d == WK for w, d in DILATED_GROUPS)


def _mm(a, b):
    return jnp.dot(a, b, preferred_element_type=F32)


def _mod_rows(ref, tm):
    m = ref[0]
    n = m.shape[0]
    if n == 1 or n == tm:
        return m
    covered = lax.broadcasted_iota(jnp.int32, (tm, n), 0) // (tm // n)
    sel = jnp.where(covered == lax.broadcasted_iota(jnp.int32, (tm, n), 1), 1.0, 0.0).astype(BF16)
    hi = m.astype(BF16)
    rest = m - hi.astype(F32)
    mid = rest.astype(BF16)
    lo = (rest - mid.astype(F32)).astype(BF16)
    return (_mm(sel, hi) + _mm(sel, mid)) + _mm(sel, lo)


def _const_spec(shape):
    nd = len(shape)
    return pl.BlockSpec(shape, lambda *_: (0,) * nd, pipeline_mode=pl.Buffered(1))


def _ada_kernel(c_ref, w_ref, b_ref, o_ref):
    c = c_ref[...]
    s = c * jax.nn.sigmoid(c)
    o_ref[...] = _mm(s.astype(BF16), w_ref[...].astype(BF16)) + b_ref[...]


def _ada(c_all, w_ada, b_ada):
    n, d = c_all.shape
    nout = w_ada.shape[1]
    tn = 1024
    return pl.pallas_call(
        _ada_kernel,
        grid=(nout // tn,),
        in_specs=[pl.BlockSpec((n, d), lambda j: (0, 0)),
                  pl.BlockSpec((d, tn), lambda j: (0, j)),
                  pl.BlockSpec((1, tn), lambda j: (0, j))],
        out_specs=pl.BlockSpec((n, tn), lambda j: (0, j)),
        out_shape=jax.ShapeDtypeStruct((n, nout), F32),
        compiler_params=pltpu.CompilerParams(dimension_semantics=("arbitrary",),
                                             vmem_limit_bytes=VMEM_LIMIT_BYTES),
        name="ada",
    )(c_all, w_ada, b_ada.reshape(1, nout))


def _in_kernel(*refs, decode, tm, dec_seq, dils, cache_kw):
    n_in = 11 if decode else 9
    x_ref, sh_ref, sc_ref, g1_ref, win_ref, cw_ref, wco_ref, qg_ref, kg_ref = refs[:9]
    if decode:
        s1_ref, s2_ref = refs[9:11]
    n_out = 3 + 3 * N_GROUPS
    cache_in = refs[n_in:n_in + N_CACHE_IN]
    if cache_kw is not None:
        n_in += N_CACHE_IN
    cm_ref, sga_ref = refs[n_in:n_in + 2]
    qkv_refs = refs[n_in + 2:n_in + 2 + 3 * N_GROUPS]
    ustate_ref = refs[n_in + n_out - 1]
    cache_out = refs[n_in + n_out:n_in + n_out + N_CACHE_OUT]
    uext, stg = refs[-2:]
    step = pl.program_id(0) * pl.num_programs(1) + pl.program_id(1)

    x = x_ref[0]
    ms = jnp.mean(x * x, axis=-1, keepdims=True)
    xn = x * lax.rsqrt(ms + RMS_EPS) * g1_ref[...]
    xn = (xn * (1.0 + _mod_rows(sc_ref, tm)) + _mod_rows(sh_ref, tm)).astype(BF16)

    h = _mm(xn, win_ref[:, OFF_H:OFF_H + D_CONV])
    c = _mm(xn, win_ref[:, OFF_C:OFF_C + D_CONV])
    u = c * h
    hist = SUBLANES
    if decode:
        uext[0:hist, :] = jnp.zeros((hist, D_CONV), F32)
    else:
        @pl.when(pl.program_id(1) == 0)
        def _():
            uext[0:hist, :] = jnp.zeros((hist, D_CONV), F32)
    uext[hist:hist + tm, :] = u
    um1 = uext[hist - 1:hist - 1 + tm, :]
    um2 = uext[hist - 2:hist - 2 + tm, :]
    if decode:
        t = lax.broadcasted_iota(jnp.int32, (tm, 1), 0) % dec_seq
        um1 = jnp.where(t >= 1, um1, s1_ref[0])
        um2 = jnp.where(t >= 2, um2, s2_ref[0])
        ustate_ref[0] = u
    else:
        uext[0:hist, :] = uext[tm:tm + hist, :]
        ustate_ref[0] = u[tm - hist:tm, :]
    y = cw_ref[0:1, :] * um2 + cw_ref[1:2, :] * um1 + cw_ref[2:3, :] * u
    if cache_kw is not None:
        scores = _cache_scores(step, cache_in, cache_out, **cache_kw)
    bg = _mm(xn, win_ref[:, OFF_B:OFF_B + D_CONV])
    conv_out = _mm((bg * y).astype(BF16), wco_ref[...])
    gc = _mm(xn, win_ref[:, OFF_GC:OFF_GC + D_MODEL])
    cm_ref[0] = (jax.nn.sigmoid(gc) * conv_out).astype(cm_ref.dtype)
    if cache_kw is not None:
        _cache_finish(scores, step, cache_in, cache_out, **cache_kw)
    ga = _mm(xn, win_ref[:, OFF_GA:OFF_GA + D_MODEL])
    sga_ref[0] = jax.nn.sigmoid(ga).astype(sga_ref.dtype)

    seg_r = lax.broadcasted_iota(jnp.int32, (MXU_DIM, MXU_DIM), 0) // HEAD_DIM
    seg_c = lax.broadcasted_iota(jnp.int32, (MXU_DIM, MXU_DIM), 1) // HEAD_DIM
    seg = jnp.where(seg_r == seg_c, 1.0, 0.0).astype(BF16)

    def put(out_ref, val, dil, part):
        if dil == 1:
            out_ref[0, 0] = val.astype(out_ref.dtype)
            return
        for i in range(D_ATT // LANES):
            stg[part, i] = val[:, i * LANES:(i + 1) * LANES]
        for r in range(dil):
            for i in range(D_ATT // LANES):
                out_ref[0, r, :, i * LANES:(i + 1) * LANES] = (
                    stg[part, i, pl.ds(r, tm // dil, stride=dil), :].astype(out_ref.dtype))

    def head_norm(a, gain):
        sq = (a * a).astype(BF16)
        ss = jnp.concatenate([_mm(sq[:, i * MXU_DIM:(i + 1) * MXU_DIM], seg)
                              for i in range(D_ATT // MXU_DIM)], axis=1)
        return a * lax.rsqrt(ss * (1.0 / HEAD_DIM) + RMS_EPS) * gain

    for g in range(N_GROUPS):
        q_ref, k_ref, v_ref = qkv_refs[3 * g:3 * g + 3]
        lo = g * D_ATT
        q = _mm(xn, win_ref[:, OFF_Q + lo:OFF_Q + lo + D_ATT])
        put(q_ref, head_norm(q, qg_ref[...] * ATT_SCALE), dils[g], 0)
        k = _mm(xn, win_ref[:, OFF_K + lo:OFF_K + lo + D_ATT])
        put(k_ref, head_norm(k, kg_ref[...]), dils[g], 1)
        put(v_ref, _mm(xn, win_ref[:, OFF_V + lo:OFF_V + lo + D_ATT]), dils[g], 2)


def _in_proj(x, shift1, scale1, g1, w_in, conv_w, w_conv_out, qg, kg, *, tm, dils, hist1=None, hist2=None,
             dec_seq=1, cache_args=None):
    nb, ns, d = x.shape
    decode = hist1 is not None
    assert ns % tm == 0 and all(tm % (dil * 2 * SUBLANES) == 0 for dil in dils)
    nmod = shift1.shape[1]
    assert nmod == 1 or (tm * nmod) % ns == 0
    tmod = 1 if nmod == 1 else tm * nmod // ns
    row = lambda b, s: (b, s, 0)
    mod_map = (lambda b, s: (b, 0, 0)) if nmod == 1 else row
    act_dtype = F32 if decode else BF16
    in_specs = [pl.BlockSpec((1, tm, d), row),
                pl.BlockSpec((1, tmod, d), mod_map),
                pl.BlockSpec((1, tmod, d), mod_map),
                _const_spec((1, d)),
                _const_spec((d, D_PROJ)),
                _const_spec((CONV_WIDTH, D_CONV)),
                _const_spec((D_CONV, D_MODEL)),
                _const_spec((1, D_ATT)),
                _const_spec((1, D_ATT))]
    args = [x, shift1, scale1, g1, w_in, conv_w, w_conv_out, qg, kg]
    if decode:
        in_specs += [pl.BlockSpec((1, tm, D_CONV), row), pl.BlockSpec((1, tm, D_CONV), row)]
        args += [hist1, hist2]
    out_shape = [jax.ShapeDtypeStruct((nb, ns, D_MODEL), BF16),
                 jax.ShapeDtypeStruct((nb, ns, D_MODEL), BF16)]
    out_specs = [pl.BlockSpec((1, tm, D_MODEL), row),
                 pl.BlockSpec((1, tm, D_MODEL), row)]
    for dil in dils:
        for _ in range(3):
            out_shape.append(jax.ShapeDtypeStruct((nb, dil, ns // dil, D_ATT), act_dtype))
            out_specs.append(pl.BlockSpec((1, dil, tm // dil, D_ATT), lambda b, s: (b, 0, s, 0)))
    if decode:
        out_shape.append(jax.ShapeDtypeStruct((nb, ns, D_CONV), F32))
        out_specs.append(pl.BlockSpec((1, tm, D_CONV), row))
    else:
        out_shape.append(jax.ShapeDtypeStruct((nb, SUBLANES, D_CONV), F32))
        out_specs.append(pl.BlockSpec((1, SUBLANES, D_CONV), lambda b, s: (b, 0, 0)))
    n_out = len(out_specs)
    cache_kw = None
    if cache_args is not None:
        nst = ns // tm
        c_in, c_out, c_shape, cache_kw = _cache_specs(cache_args, nb * nst, lambda b, s: b * nst + s)
        in_specs += c_in
        args += list(cache_args)
        out_specs += c_out
        out_shape += c_shape
    outs = pl.pallas_call(
        functools.partial(_in_kernel, decode=decode, tm=tm, dec_seq=dec_seq, dils=tuple(dils),
                          cache_kw=cache_kw),
        grid=(nb, ns // tm),
        in_specs=in_specs,
        out_specs=out_specs,
        out_shape=out_shape,
        scratch_shapes=[pltpu.VMEM((tm + SUBLANES, D_CONV), F32),
                        pltpu.VMEM((3, D_ATT // LANES, tm, LANES), F32)],
        compiler_params=pltpu.CompilerParams(dimension_semantics=("arbitrary", "arbitrary"),
                                             vmem_limit_bytes=VMEM_LIMIT_BYTES),
        name="in_proj_decode" if decode else "in_proj",
    )(*args)
    return outs[0], outs[1], outs[2:2 + 3 * N_GROUPS], outs[n_out - 1], outs[n_out:]


HEADS_PER_PASS = MXU_DIM // HEAD_DIM


def _attn_blocks(q_ref, kc_ref, kp_ref, vc_ref, vp_ref, tb_ref, o_ref, lse_ref, *, nb, first):
    blk = ATT_BLOCK
    lane = lax.broadcasted_iota(jnp.int32, (1, MXU_DIM), 1)
    hmask = [(lane >= HEAD_DIM * h) & (lane < HEAD_DIM * (h + 1)) for h in range(HEADS_PER_PASS)]
    col = lax.broadcasted_iota(jnp.int32, (1, 2 * blk), 1)
    lane_out = lax.broadcasted_iota(jnp.int32, (1, LANES), 1)
    for n in range(nb):
        rows = slice(n * blk, (n + 1) * blk)
        lse_acc = jnp.zeros((blk, LANES), F32)
        for hp in range(N_HEADS // HEADS_PER_PASS):
            cols = slice(hp * MXU_DIM, (hp + 1) * MXU_DIM)
            q4 = q_ref[0, 0, rows, cols]
            lhs = jnp.concatenate([jnp.where(hmask[h], q4, jnp.zeros_like(q4))
                                   for h in range(HEADS_PER_PASS)], axis=0)
            if n == 0:
                k_prev, v_prev = kp_ref[0, 0, :, cols], vp_ref[0, 0, :, cols]
            else:
                prev = slice((n - 1) * blk, n * blk)
                k_prev, v_prev = kc_ref[0, 0, prev, cols], vc_ref[0, 0, prev, cols]
            kk = jnp.concatenate([k_prev, kc_ref[0, 0, rows, cols]], axis=0)
            vv = jnp.concatenate([v_prev, vc_ref[0, 0, rows, cols]], axis=0)
            s = lax.dot_general(lhs, kk, (((1,), (1,)), ((), ())), preferred_element_type=F32)
            s = s + tb_ref[hp * HEADS_PER_PASS * blk:(hp + 1) * HEADS_PER_PASS * blk, :]
            if n == 0:
                s = jnp.where(first & (col < blk), NEG, s)
            m = jnp.max(s, axis=-1, keepdims=True)
            p = jnp.exp(s - m)
            l = jnp.sum(p, axis=-1, keepdims=True)
            pv = _mm(p.astype(BF16), vv) * (1.0 / l)
            lse = m + jnp.log(l)
            o4 = jnp.zeros((blk, MXU_DIM), F32)
            for h in range(HEADS_PER_PASS):
                hr = slice(h * blk, (h + 1) * blk)
                o4 = jnp.where(hmask[h], pv[hr, :], o4)
                lse_acc = jnp.where(lane_out == hp * HEADS_PER_PASS + h, lse[hr, :], lse_acc)
            o_ref[0, 0, rows, cols] = o4.astype(o_ref.dtype)
        lse_ref[0, 0, rows, :] = lse_acc


def _shifted_cache(src_ref, tail, dst_ref, i, *, width, dec_seq):
    keep = LANES - dec_seq
    lane = lax.broadcasted_iota(jnp.int32, (1, LANES), 1)
    nch = width // LANES
    cur = pltpu.roll(src_ref[i, :, 0:LANES], keep, 1)
    for c in range(nch):
        nxt = pltpu.roll(src_ref[i, :, (c + 1) * LANES:(c + 2) * LANES], keep, 1) if c + 1 < nch else tail
        dst_ref[i, :, c * LANES:(c + 1) * LANES] = jnp.where(lane < keep, cur, nxt)
        cur = nxt


def _decode_scores(q, k_tail, kt_ref, bias_ref, skt_ref, i, *, width, dec_seq):
    nrow = dec_seq * N_HEADS
    own = (lax.broadcasted_iota(jnp.int32, (nrow, D_ATT), 0) % N_HEADS
           == lax.broadcasted_iota(jnp.int32, (nrow, D_ATT), 1) // HEAD_DIM)
    q_rows = jnp.concatenate([jnp.broadcast_to(q[t:t + 1, :], (N_HEADS, D_ATT)) for t in range(dec_seq)], axis=0)
    q_bd = jnp.where(own, q_rows, 0.0).astype(BF16)
    s_c = _mm(q_bd, kt_ref[i].astype(BF16)) + bias_ref[:, 0:width]
    s_t = _mm(q_bd, k_tail.astype(BF16)) + bias_ref[:, width:width + LANES]
    _shifted_cache(kt_ref, k_tail, skt_ref, i, width=width, dec_seq=dec_seq)
    return s_c, s_t


def _decode_finish(s_c, s_t, v_tail, vt_ref, svt_ref, i, *, width, dec_seq):
    nrow = dec_seq * N_HEADS
    own = (lax.broadcasted_iota(jnp.int32, (nrow, D_ATT), 0) % N_HEADS
           == lax.broadcasted_iota(jnp.int32, (nrow, D_ATT), 1) // HEAD_DIM)
    m = jnp.maximum(jnp.max(s_c, axis=-1, keepdims=True), jnp.max(s_t, axis=-1, keepdims=True))
    p_c = jnp.exp(s_c - m)
    p_t = jnp.exp(s_t - m)
    l = jnp.sum(p_c, axis=-1, keepdims=True) + jnp.sum(p_t, axis=-1, keepdims=True)
    nt = (((1,), (1,)), ((), ()))
    o = (lax.dot_general(p_c.astype(BF16), vt_ref[i].astype(BF16), nt, preferred_element_type=F32)
         + lax.dot_general(p_t.astype(BF16), v_tail.astype(BF16), nt, preferred_element_type=F32))
    o = jnp.where(own, o, 0.0) * (1.0 / l)
    o_tok = jnp.sum(o.reshape(dec_seq, N_HEADS, D_ATT), axis=1)
    lse = m + jnp.log(l)
    head_lane = (lax.broadcasted_iota(jnp.int32, (nrow, LANES), 0) % N_HEADS
                 == lax.broadcasted_iota(jnp.int32, (nrow, LANES), 1))
    lse_tok = jnp.sum(jnp.where(head_lane, lse, 0.0).reshape(dec_seq, N_HEADS, LANES), axis=1)
    _shifted_cache(vt_ref, v_tail, svt_ref, i, width=width, dec_seq=dec_seq)
    return o_tok, lse_tok


N_CACHE_IN, N_CACHE_OUT = 6, 4


def _cache_specs(cache_args, n_steps, step_of):
    q, _, _, kt, _, bias = cache_args
    db, dec_seq, _ = q.shape
    width = kt.shape[-1]
    pack = DEC_PACK // dec_seq
    assert db % n_steps == 0 and db % pack == 0 and width % LANES == 0
    bb = db // n_steps
    assert pack % bb == 0 and bb <= 4
    per_b = lambda *i: (step_of(*i), 0, 0)
    packed = lambda *i: (step_of(*i) * bb // pack, 0, 0)
    in_specs = [pl.BlockSpec((bb, dec_seq, D_ATT), per_b),
                pl.BlockSpec((1, D_ATT, DEC_PACK), packed),
                pl.BlockSpec((1, D_ATT, DEC_PACK), packed),
                pl.BlockSpec((bb, D_ATT, width), per_b),
                pl.BlockSpec((bb, D_ATT, width), per_b),
                pl.BlockSpec(bias.shape, lambda *i: (0, 0))]
    out_specs = [pl.BlockSpec((bb, D_ATT, width), per_b),
                 pl.BlockSpec((bb, D_ATT, width), per_b),
                 pl.BlockSpec((bb, dec_seq, D_ATT), per_b),
                 pl.BlockSpec((bb, dec_seq, LANES), per_b)]
    out_shape = [jax.ShapeDtypeStruct((db, D_ATT, width), F32),
                 jax.ShapeDtypeStruct((db, D_ATT, width), F32),
                 jax.ShapeDtypeStruct((db, dec_seq, D_ATT), F32),
                 jax.ShapeDtypeStruct((db, dec_seq, LANES), F32)]
    return in_specs, out_specs, out_shape, dict(width=width, dec_seq=dec_seq, bb=bb)


def _cache_scores(step, in_refs, out_refs, *, width, dec_seq, bb):
    q_ref, knp_ref, _, kt_ref, _, bias_ref = in_refs
    pack = DEC_PACK // dec_seq
    scores = []
    for i in range(bb):
        slot = ((step * bb + i) % pack) * dec_seq
        to_tail = (2 * LANES - dec_seq - slot) % LANES
        k_tail = pltpu.roll(knp_ref[0], to_tail, 1)
        scores.append(_decode_scores(q_ref[i], k_tail, kt_ref, bias_ref, out_refs[0], i,
                                     width=width, dec_seq=dec_seq))
    return scores


def _cache_finish(scores, step, in_refs, out_refs, *, width, dec_seq, bb):
    vnp_ref, vt_ref = in_refs[2], in_refs[4]
    _, svt_ref, o_ref, lse_ref = out_refs
    pack = DEC_PACK // dec_seq
    for i in range(bb):
        slot = ((step * bb + i) % pack) * dec_seq
        to_tail = (2 * LANES - dec_seq - slot) % LANES
        v_tail = pltpu.roll(vnp_ref[0], to_tail, 1)
        o_ref[i], lse_ref[i] = _decode_finish(*scores[i], v_tail, vt_ref, svt_ref, i, width=width, dec_seq=dec_seq)


def _attn_cache_kernel(*refs, nb, rows_blocks, cache_kw):
    n_in = N_CACHE_IN + 6 * N_GROUPS
    att_in = refs[N_CACHE_IN:n_in]
    att_out = refs[n_in + N_CACHE_OUT:]
    step = pl.program_id(0)
    cache_in, cache_out = refs[:N_CACHE_IN], refs[n_in:n_in + N_CACHE_OUT]
    scores = _cache_scores(step, cache_in, cache_out, **cache_kw)
    _cache_finish(scores, step, cache_in, cache_out, **cache_kw)
    for g in range(N_GROUPS):
        first = (step * nb) % rows_blocks[g] == 0
        _attn_blocks(*att_in[6 * g:6 * g + 6], *att_out[2 * g:2 * g + 2], nb=nb, first=first)


def _attention_and_cache(qkv, tbs, cache_args):
    db = cache_args[0].shape[0]
    nbatch = qkv[0].shape[0]
    total_blocks = nbatch * qkv[0].shape[1] * qkv[0].shape[2] // ATT_BLOCK
    assert total_blocks % db == 0
    nb = total_blocks // db
    tq = nb * ATT_BLOCK
    in_specs, out_specs, out_shape, cache_kw = _cache_specs(cache_args, db, lambda s: s)
    args = list(cache_args)
    rows_blocks = []
    for g in range(N_GROUPS):
        q, k, v = qkv[3 * g:3 * g + 3]
        _, dil, nl, _ = q.shape
        rb = nl // ATT_BLOCK
        assert nl % ATT_BLOCK == 0 and rb % nb == 0
        rows_blocks.append(rb)

        def cur(s, rb=rb, dil=dil):
            blk = s * nb
            return (blk // (dil * rb), (blk // rb) % dil, (blk % rb) // nb, 0)

        def prev(s, rb=rb, dil=dil):
            blk = s * nb
            return (blk // (dil * rb), (blk // rb) % dil, jnp.maximum(blk % rb - 1, 0), 0)

        in_specs += [pl.BlockSpec((1, 1, tq, D_ATT), cur),
                     pl.BlockSpec((1, 1, tq, D_ATT), cur),
                     pl.BlockSpec((1, 1, ATT_BLOCK, D_ATT), prev),
                     pl.BlockSpec((1, 1, tq, D_ATT), cur),
                     pl.BlockSpec((1, 1, ATT_BLOCK, D_ATT), prev),
                     pl.BlockSpec((N_HEADS * ATT_BLOCK, 2 * ATT_BLOCK), lambda s: (0, 0))]
        args += [q, k, k, v, v, tbs[g]]
        out_specs += [pl.BlockSpec((1, 1, tq, D_ATT), cur), pl.BlockSpec((1, 1, tq, LANES), cur)]
        out_shape += [jax.ShapeDtypeStruct(q.shape, BF16),
                      jax.ShapeDtypeStruct(q.shape[:3] + (LANES,), F32)]
    outs = pl.pallas_call(
        functools.partial(_attn_cache_kernel, nb=nb, rows_blocks=tuple(rows_blocks), cache_kw=cache_kw),
        grid=(db,),
        in_specs=in_specs,
        out_specs=out_specs,
        out_shape=out_shape,
        compiler_params=pltpu.CompilerParams(dimension_semantics=("arbitrary",),
                                             vmem_limit_bytes=VMEM_LIMIT_BYTES),
        name="attn_and_cache",
    )(*args)
    return outs[N_CACHE_OUT:], outs[:N_CACHE_OUT]


def _out_kernel(*refs, tm, dils, cache_kw):
    x_ref, cm_ref, sga_ref = refs[:3]
    o_refs = refs[3:3 + N_GROUPS]
    l_refs = refs[3 + N_GROUPS:3 + 2 * N_GROUPS]
    n_in = 3 + 2 * N_GROUPS + 9
    g1_ref, sh2_ref, sc2_ref, g2_ref, n2_ref, wao_ref, wo_ref, w1_ref, w2_ref = refs[3 + 2 * N_GROUPS:n_in]
    cache_in = refs[n_in:n_in + N_CACHE_IN]
    if cache_kw is not None:
        n_in += N_CACHE_IN
    y_ref = refs[n_in]
    cache_out = refs[n_in + 1:n_in + 1 + N_CACHE_OUT]
    ostg, lstg = refs[-2:]

    def token_order(ref, stg, dil):
        if dil == 1:
            return ref[0, 0].astype(F32)
        n_slab = ref.shape[-1] // LANES
        for r in range(dil):
            blk = ref[0, r].astype(F32)
            for i in range(n_slab):
                stg[i, pl.ds(r, tm // dil, stride=dil), :] = blk[:, i * LANES:(i + 1) * LANES]
        return jnp.concatenate([stg[i] for i in range(n_slab)], axis=1)

    lses = [token_order(l_refs[g], lstg, dils[g]) for g in range(N_GROUPS)]
    mx = jnp.maximum(jnp.maximum(lses[0], lses[1]), lses[2])
    es = [jnp.exp(l - mx) for l in lses]
    inv = 1.0 / (es[0] + es[1] + es[2])
    er = lax.broadcasted_iota(jnp.int32, (LANES, D_ATT), 0)
    ec = lax.broadcasted_iota(jnp.int32, (LANES, D_ATT), 1) // HEAD_DIM
    expand = jnp.where(er == ec, 1.0, 0.0).astype(BF16)
    o = None
    for g in range(N_GROUPS):
        w = es[g] * inv
        w_hi = w.astype(BF16)
        w_lo = (w - w_hi.astype(F32)).astype(BF16)
        wexp = _mm(w_hi, expand) + _mm(w_lo, expand)
        term = wexp * token_order(o_refs[g], ostg, dils[g])
        o = term if o is None else o + term
    attn_out = _mm(o.astype(BF16), wao_ref[...])
    mixed = cm_ref[0].astype(F32) + sga_ref[0].astype(F32) * attn_out
    x1 = x_ref[0] + _mod_rows(g1_ref, tm) * _mm(mixed.astype(BF16), wo_ref[...])
    ms = jnp.mean(x1 * x1, axis=-1, keepdims=True)
    xn2 = x1 * lax.rsqrt(ms + RMS_EPS) * n2_ref[...]
    xn2 = (xn2 * (1.0 + _mod_rows(sc2_ref, tm)) + _mod_rows(sh2_ref, tm)).astype(BF16)
    acc = None
    n_ff = D_FF // FF_CHUNK
    step = pl.program_id(0) * pl.num_programs(1) + pl.program_id(1)
    for f in range(n_ff):
        if cache_kw is not None and f == n_ff // 2:
            scores = _cache_scores(step, cache_in, cache_out, **cache_kw)
        cols = slice(f * FF_CHUNK, (f + 1) * FF_CHUNK)
        hid = jnp.maximum(_mm(xn2, w1_ref[:, cols]), 0.0)
        part = _mm((hid * hid).astype(BF16), w2_ref[cols, :])
        acc = part if acc is None else acc + part
        if cache_kw is not None and f == n_ff // 2:
            _cache_finish(scores, step, cache_in, cache_out, **cache_kw)
    y_ref[0] = x1 + _mod_rows(g2_ref, tm) * acc


def _out_proj(x, cm, sga, os, lses, gate1, shift2, scale2, gate2, n2, wao, wo, w1, w2, *, tm, cache_args=None):
    nb, ns, d = x.shape
    dils = tuple(o.shape[1] for o in os)
    assert ns % tm == 0 and all(tm % (dil * 2 * SUBLANES) == 0 for dil in dils)
    nmod = gate1.shape[1]
    assert nmod == 1 or (tm * nmod) % ns == 0
    tmod = 1 if nmod == 1 else tm * nmod // ns
    row = lambda b, s: (b, s, 0)
    mod_map = (lambda b, s: (b, 0, 0)) if nmod == 1 else row
    mod_spec = pl.BlockSpec((1, tmod, d), mod_map)
    res_spec = lambda dil, n: pl.BlockSpec((1, dil, tm // dil, n), lambda b, s: (b, 0, s, 0))
    in_specs = ([pl.BlockSpec((1, tm, d), row)] * 3
                + [res_spec(dil, D_ATT) for dil in dils]
                + [res_spec(dil, LANES) for dil in dils]
                + [mod_spec] * 4
                + [_const_spec((1, d)), _const_spec((D_ATT, d)), _const_spec((d, d)),
                   _const_spec((d, D_FF)), _const_spec((D_FF, d))])
    args = [x, cm, sga, *os, *lses, gate1, shift2, scale2, gate2, n2, wao, wo, w1, w2]
    out_specs = [pl.BlockSpec((1, tm, d), row)]
    out_shape = [jax.ShapeDtypeStruct((nb, ns, d), F32)]
    cache_kw = None
    if cache_args is not None:
        nst = ns // tm
        c_in, c_out, c_shape, cache_kw = _cache_specs(cache_args, nb * nst, lambda b, s: b * nst + s)
        in_specs += c_in
        args += list(cache_args)
        out_specs += c_out
        out_shape += c_shape
    outs = pl.pallas_call(
        functools.partial(_out_kernel, tm=tm, dils=dils, cache_kw=cache_kw),
        grid=(nb, ns // tm),
        in_specs=in_specs,
        out_specs=out_specs,
        out_shape=out_shape,
        scratch_shapes=[pltpu.VMEM((D_ATT // LANES, tm, LANES), F32), pltpu.VMEM((1, tm, LANES), F32)],
        compiler_params=pltpu.CompilerParams(dimension_semantics=("arbitrary", "arbitrary"),
                                             vmem_limit_bytes=VMEM_LIMIT_BYTES),
        name="out_proj",
    )(*args)
    return outs[0] if cache_args is None else outs


def _tail_kernel(k_ref, v_ref, ko_ref, vo_ref, stg, *, dil, rows):
    for src, dst in ((k_ref, ko_ref), (v_ref, vo_ref)):
        if dil == 1:
            x = src[0, 0].astype(F32)
        else:
            for r in range(dil):
                blk = src[0, r].astype(F32)
                for i in range(D_ATT // LANES):
                    stg[i, pl.ds(r, rows // dil, stride=dil), :] = blk[:, i * LANES:(i + 1) * LANES]
            x = jnp.concatenate([stg[i] for i in range(D_ATT // LANES)], axis=1)
        dst[0] = x.T


def _state_tail(k, v, keep):
    nbatch, dil, nl, _ = k.shape
    rows = min(keep, TM_OUT)
    assert keep % rows == 0 and (nl * dil - keep) % rows == 0 and rows % (dil * 2 * SUBLANES) == 0
    first = (nl * dil - keep) // rows
    src = pl.BlockSpec((1, dil, rows // dil, D_ATT), lambda b, j: (b, 0, first + j, 0))
    dst = pl.BlockSpec((1, D_ATT, rows), lambda b, j: (b, 0, j))
    return pl.pallas_call(
        functools.partial(_tail_kernel, dil=dil, rows=rows),
        grid=(nbatch, keep // rows),
        in_specs=[src, src],
        out_specs=[dst, dst],
        out_shape=[jax.ShapeDtypeStruct((nbatch, D_ATT, keep), F32)] * 2,
        scratch_shapes=[pltpu.VMEM((D_ATT // LANES, rows, LANES), F32)],
        compiler_params=pltpu.CompilerParams(dimension_semantics=("arbitrary", "arbitrary"),
                                             vmem_limit_bytes=VMEM_LIMIT_BYTES),
        name=f"state_tail_d{dil}",
    )(k, v)


def _t5_causal_bucket(dist):
    max_exact = N_BUCKETS // 2
    ratio = jnp.maximum(dist, 1).astype(F32) / max_exact
    large = max_exact + (jnp.log(ratio) / math.log(MAX_DISTANCE / max_exact)
                         * (N_BUCKETS - max_exact)).astype(jnp.int32)
    large = jnp.minimum(large, N_BUCKETS - 1)
    return jnp.where(dist < max_exact, dist, large)


def _group_bias(rel_bias, g):
    steps = jnp.arange(WK + 1)
    bucket = _t5_causal_bucket(steps * DILS[g])
    onehot = (bucket[:, None] == jnp.arange(N_BUCKETS)[None, :]).astype(F32)
    b = jnp.dot(onehot, rel_bias[:, g * N_HEADS:(g + 1) * N_HEADS].astype(F32), precision=lax.Precision.HIGHEST)
    return b.T


def _prompt_bias_table(bias):
    blk, period = ATT_BLOCK, 4 * ATT_BLOCK
    rev = bias[:, ::-1]
    row = jnp.concatenate([rev, jnp.full((N_HEADS, period - (WK + 1)), NEG, F32)], axis=1)
    tiled = jnp.tile(row, (1, blk))[:, :blk * (period - 1)].reshape(N_HEADS, blk, period - 1)
    return tiled[:, :, :2 * blk].reshape(N_HEADS * blk, 2 * blk)


def _decode_bias_table(bias, width, dil, dec_seq):
    assert width == WK * dil and (dil == 1 or dil >= dec_seq)
    rev = bias[:, ::-1]
    neg_col = jnp.full((N_HEADS, 1), NEG, F32)
    rows = []
    for t in range(dec_seq):
        if dil == 1:
            cache = jnp.pad(rev[:, :width - t], ((0, 0), (t, 0)), constant_values=NEG)
        else:
            mine = (np.arange(dil) == t)[None, None, :]
            cache = jnp.where(mine, rev[:, :WK, None], NEG).reshape(N_HEADS, width)
        new = [bias[:, (t - tn) // dil:(t - tn) // dil + 1] if tn <= t and (t - tn) % dil == 0 else neg_col
               for tn in range(dec_seq)]
        rows.append(jnp.concatenate([cache, jnp.full((N_HEADS, LANES - dec_seq), NEG, F32)] + new, axis=1))
    return jnp.concatenate(rows, axis=0)


def _layer(x_prompt, x_sample, c_prompt, c_sample, state_conv, caches, rel_bias, norm1_g, norm2_g,
           w_ada, b_ada, w_in, conv_w, q_norm_g, k_norm_g, w_conv_out, w_attn_out, w_o, w_mlp_in,
           w_mlp_out):
    nbatch, ns, d = x_prompt.shape
    db, dec_seq, _ = x_sample.shape
    ntok = db * dec_seq

    w_in_b = w_in.astype(BF16)
    wco_b = w_conv_out.astype(BF16)
    wao_b = w_attn_out.astype(BF16)
    wo_b = w_o.astype(BF16)
    w1_b = w_mlp_in.astype(BF16)
    w2_b = w_mlp_out.astype(BF16)
    g1 = norm1_g.reshape(1, d)
    g2n = norm2_g.reshape(1, d)
    qg = jnp.tile(q_norm_g.reshape(1, HEAD_DIM), (1, N_HEADS))
    kg = jnp.tile(k_norm_g.reshape(1, HEAD_DIM), (1, N_HEADS))

    n_c = nbatch + db
    n_pad = -(-n_c // SUBLANES) * SUBLANES
    c_all = jnp.concatenate([c_prompt, c_sample, jnp.zeros((n_pad - n_c, d), F32)], axis=0)
    mod = _ada(c_all, w_ada, b_ada)
    mod_p = mod[:nbatch].reshape(nbatch, 1, N_MOD, d)
    mod_s = mod[nbatch:n_c].reshape(1, db, N_MOD, d)
    mp = [mod_p[:, :, i] for i in range(N_MOD)]
    msn = [mod_s[:, :, i] for i in range(N_MOD)]

    biases = [_group_bias(rel_bias, g) for g in range(N_GROUPS)]

    xs = x_sample.reshape(1, ntok, d)
    zero_row = jnp.zeros((db, 1, D_CONV), F32)
    hist1 = jnp.concatenate([state_conv[:, 1:2]] + [zero_row] * (dec_seq - 1), axis=1).reshape(1, ntok, D_CONV)
    hist2 = jnp.concatenate([state_conv[:, 0:1], state_conv[:, 1:2]] + [zero_row] * (dec_seq - 2),
                            axis=1).reshape(1, ntok, D_CONV)
    cm_s, sga_s, qkv_s, u_s, _ = _in_proj(xs, msn[0], msn[1], g1, w_in_b, conv_w, wco_b, qg, kg, tm=ntok,
                                          dils=(1,) * N_GROUPS, hist1=hist1, hist2=hist2, dec_seq=dec_seq)
    s_conv = u_s.reshape(db, dec_seq, D_CONV)[:, dec_seq - (CONV_WIDTH - 1):]

    def pack_cols(a):
        return a.reshape(ntok // DEC_PACK, DEC_PACK, D_ATT).transpose(0, 2, 1)

    cache_args = []
    for g, (window, dil) in enumerate(DILATED_GROUPS):
        ck, cv = caches[2 * g], caches[2 * g + 1]
        width = ck.shape[1]
        assert width == window
        kt = ck.transpose(0, 2, 3, 1).reshape(db, D_ATT, width)
        vt = cv.transpose(0, 2, 3, 1).reshape(db, D_ATT, width)
        cache_args.append((qkv_s[3 * g].reshape(db, dec_seq, D_ATT), pack_cols(qkv_s[3 * g + 1]),
                           pack_cols(qkv_s[3 * g + 2]), kt, vt,
                           _decode_bias_table(biases[g], width, dil, dec_seq)))
    by_width = sorted(range(N_GROUPS), key=lambda g: DILATED_GROUPS[g][0])
    on_in_proj, on_out_proj, on_attention = by_width[0], by_width[1], by_width[2]
    cache_out = [None] * N_GROUPS

    cm, sga, qkv, utail, cache_out[on_in_proj] = _in_proj(
        x_prompt, mp[0], mp[1], g1, w_in_b, conv_w, wco_b, qg, kg, tm=TM_IN, dils=DILS,
        cache_args=cache_args[on_in_proj])
    p_conv = utail[:, SUBLANES - (CONV_WIDTH - 1):, :]
    p_kv = []
    for g, (window, dil) in enumerate(DILATED_GROUPS):
        keep = min(window, ns)
        for tail in _state_tail(qkv[3 * g + 1], qkv[3 * g + 2], keep):
            p_kv.append(tail.reshape(nbatch, N_HEADS, HEAD_DIM, keep).transpose(0, 3, 1, 2))
    prompt_att, cache_out[on_attention] = _attention_and_cache(
        qkv, [_prompt_bias_table(b) for b in biases], cache_args[on_attention])
    y_prompt, *cache_out[on_out_proj] = _out_proj(
        x_prompt, cm, sga, prompt_att[0::2], prompt_att[1::2], mp[2], mp[3], mp[4], mp[5], g2n,
        wao_b, wo_b, w1_b, w2_b, tm=TM_OUT, cache_args=cache_args[on_out_proj])

    os_s, lses_s, s_kv = [], [], []
    for g, (window, _) in enumerate(DILATED_GROUPS):
        skt, svt, o_s, lse_s = cache_out[g]
        s_kv.append(skt.reshape(db, N_HEADS, HEAD_DIM, window).transpose(0, 3, 1, 2))
        s_kv.append(svt.reshape(db, N_HEADS, HEAD_DIM, window).transpose(0, 3, 1, 2))
        os_s.append(o_s.reshape(1, 1, ntok, D_ATT).astype(BF16))
        lses_s.append(lse_s.reshape(1, 1, ntok, LANES))
    y_sample = _out_proj(xs, cm_s, sga_s, os_s, lses_s, msn[2], msn[3], msn[4], msn[5], g2n,
                         wao_b, wo_b, w1_b, w2_b, tm=ntok).reshape(db, dec_seq, d)
    return y_prompt, y_sample, [p_conv] + p_kv, [s_conv] + s_kv


def kernel(x_prompt, x_sample, c_prompt, c_sample, state_conv, cache_k1, cache_v1, cache_k2, cache_v2,
           cache_k3, cache_v3, rel_bias, norm1_g, norm2_g, w_ada, b_ada, w_in, conv_w, q_norm_g, k_norm_g,
           w_conv_out, w_attn_out, w_o, w_mlp_in, w_mlp_out):
    depth = w_in.shape[0]
    caches = (cache_k1, cache_v1, cache_k2, cache_v2, cache_k3, cache_v3)
    yp, ys = x_prompt, x_sample
    p_states = [[] for _ in range(1 + 2 * N_GROUPS)]
    s_states = [[] for _ in range(1 + 2 * N_GROUPS)]
    for l in range(depth):
        yp, ys, p_new, s_new = _layer(
            yp, ys, c_prompt, c_sample, state_conv[l], [c[l] for c in caches], rel_bias,
            norm1_g[l], norm2_g[l], w_ada[l], b_ada[l], w_in[l], conv_w[l], q_norm_g[l], k_norm_g[l],
            w_conv_out[l], w_attn_out[l], w_o[l], w_mlp_in[l], w_mlp_out[l])
        for lst, a in zip(p_states, p_new):
            lst.append(a)
        for lst, a in zip(s_states, s_new):
            lst.append(a)
    p_out = [jnp.stack(a) for a in p_states]
    s_out = [jnp.stack(a) for a in s_states]
    return (yp, ys, *p_out, *s_out)
```

```python
import functools
import math

import numpy as np
import jax
import jax.numpy as jnp
from jax import lax
from jax.experimental import pallas as pl
from jax.experimental.pallas import tpu as pltpu

F32 = jnp.float32
BF16 = jnp.bfloat16

D_MODEL = 1024
D_CONV = D_MODEL
CONV_WIDTH = 3
HEAD_DIM = 64
N_HEADS = 8
D_ATT = N_HEADS * HEAD_DIM
DILATED_GROUPS = ((128, 1), (512, 4), (2048, 16))
N_GROUPS = len(DILATED_GROUPS)
DILS = tuple(d for _, d in DILATED_GROUPS)
D_QKV = N_GROUPS * D_ATT
D_FF = 4 * D_MODEL
N_BUCKETS = 32
MAX_DISTANCE = 2048
ATT_BLOCK = 128
WK = 128
N_MOD = 6
RMS_EPS = 1e-6
ATT_SCALE = HEAD_DIM ** -0.5
NEG = -1e30

OFF_H, OFF_B, OFF_C = 0, D_CONV, 2 * D_CONV
OFF_Q = 3 * D_CONV
OFF_K = OFF_Q + D_QKV
OFF_V = OFF_K + D_QKV
OFF_GC = OFF_V + D_QKV
OFF_GA = OFF_GC + D_MODEL
D_PROJ = OFF_GA + D_MODEL

LANES = 128
SUBLANES = 8
MXU_DIM = 256
VMEM_LIMIT_BYTES = 60 * 1024 * 1024

TM_IN = 512
TM_OUT = 512
FF_CHUNK = 512
DEC_PACK = LANES

assert all(w // d == WK for w, d in DILATED_GROUPS)


def _mm(a, b):
    return jnp.dot(a, b, preferred_element_type=F32)


def _mod_rows(ref, tm):
    m = ref[0]
    n = m.shape[0]
    if n == 1 or n == tm:
        return m
    covered = lax.broadcasted_iota(jnp.int32, (tm, n), 0) // (tm // n)
    sel = jnp.where(covered == lax.broadcasted_iota(jnp.int32, (tm, n), 1), 1.0, 0.0).astype(BF16)
    hi = m.astype(BF16)
    rest = m - hi.astype(F32)
    mid = rest.astype(BF16)
    lo = (rest - mid.astype(F32)).astype(BF16)
    return (_mm(sel, hi) + _mm(sel, mid)) + _mm(sel, lo)


def _const_spec(shape):
    nd = len(shape)
    return pl.BlockSpec(shape, lambda *_: (0,) * nd, pipeline_mode=pl.Buffered(1))


def _ada_kernel(c_ref, w_ref, b_ref, o_ref):
    c = c_ref[...]
    s = c * jax.nn.sigmoid(c)
    o_ref[...] = _mm(s.astype(BF16), w_ref[...].astype(BF16)) + b_ref[...]


def _ada(c_all, w_ada, b_ada):
    n, d = c_all.shape
    nout = w_ada.shape[1]
    tn = 1024
    return pl.pallas_call(
        _ada_kernel,
        grid=(nout // tn,),
        in_specs=[pl.BlockSpec((n, d), lambda j: (0, 0)),
                  pl.BlockSpec((d, tn), lambda j: (0, j)),
                  pl.BlockSpec((1, tn), lambda j: (0, j))],
        out_specs=pl.BlockSpec((n, tn), lambda j: (0, j)),
        out_shape=jax.ShapeDtypeStruct((n, nout), F32),
        compiler_params=pltpu.CompilerParams(dimension_semantics=("arbitrary",),
                                             vmem_limit_bytes=VMEM_LIMIT_BYTES),
        name="ada",
    )(c_all, w_ada, b_ada.reshape(1, nout))


def _in_kernel(*refs, decode, tm, dec_seq, dils, cache_kw):
    n_in = 11 if decode else 9
    x_ref, sh_ref, sc_ref, g1_ref, win_ref, cw_ref, wco_ref, qg_ref, kg_ref = refs[:9]
    if decode:
        s1_ref, s2_ref = refs[9:11]
    n_out = 3 + 3 * N_GROUPS
    cache_in = refs[n_in:n_in + N_CACHE_IN]
    if cache_kw is not None:
        n_in += N_CACHE_IN
    cm_ref, sga_ref = refs[n_in:n_in + 2]
    qkv_refs = refs[n_in + 2:n_in + 2 + 3 * N_GROUPS]
    ustate_ref = refs[n_in + n_out - 1]
    cache_out = refs[n_in + n_out:n_in + n_out + N_CACHE_OUT]
    uext, stg = refs[-2:]
    step = pl.program_id(0) * pl.num_programs(1) + pl.program_id(1)

    x = x_ref[0]
    ms = jnp.mean(x * x, axis=-1, keepdims=True)
    xn = x * lax.rsqrt(ms + RMS_EPS) * g1_ref[...]
    xn_f32 = xn * (1.0 + _mod_rows(sc_ref, tm)) + _mod_rows(sh_ref, tm)
    xn = xn_f32.astype(BF16)

    h = _mm(xn, win_ref[:, OFF_H:OFF_H + D_CONV])
    c = _mm(xn, win_ref[:, OFF_C:OFF_C + D_CONV])
    u = c * h
    hist = SUBLANES
    if decode:
        uext[0:hist, :] = jnp.zeros((hist, D_CONV), F32)
    else:
        @pl.when(pl.program_id(1) == 0)
        def _():
            uext[0:hist, :] = jnp.zeros((hist, D_CONV), F32)
    uext[hist:hist + tm, :] = u
    um1 = uext[hist - 1:hist - 1 + tm, :]
    um2 = uext[hist - 2:hist - 2 + tm, :]
    if decode:
        t = lax.broadcasted_iota(jnp.int32, (tm, 1), 0) % dec_seq
        um1 = jnp.where(t >= 1, um1, s1_ref[0])
        um2 = jnp.where(t >= 2, um2, s2_ref[0])
        ustate_ref[0] = u
    else:
        uext[0:hist, :] = uext[tm:tm + hist, :]
        ustate_ref[0] = u[tm - hist:tm, :]
    y = cw_ref[0:1, :] * um2 + cw_ref[1:2, :] * um1 + cw_ref[2:3, :] * u
    if cache_kw is not None:
        scores = _cache_scores(step, cache_in, cache_out, **cache_kw)
    bg = _mm(xn, win_ref[:, OFF_B:OFF_B + D_CONV])
    conv_out = _mm((bg * y).astype(BF16), wco_ref[...])
    gc = _mm(xn, win_ref[:, OFF_GC:OFF_GC + D_MODEL])
    cm_ref[0] = (jax.nn.sigmoid(gc) * conv_out).astype(cm_ref.dtype)
    if cache_kw is not None:
        _cache_finish(scores, step, cache_in, cache_out, **cache_kw)
    ga = _mm(xn, win_ref[:, OFF_GA:OFF_GA + D_MODEL])
    sga_ref[0] = jax.nn.sigmoid(ga).astype(sga_ref.dtype)

    seg_r = lax.broadcasted_iota(jnp.int32, (MXU_DIM, MXU_DIM), 0) // HEAD_DIM
    seg_c = lax.broadcasted_iota(jnp.int32, (MXU_DIM, MXU_DIM), 1) // HEAD_DIM
    seg = jnp.where(seg_r == seg_c, 1.0, 0.0).astype(BF16)

    n_slab = D_MODEL // LANES
    if any(dil > 1 for dil in dils):
        for i in range(n_slab):
            stg[i] = xn_f32[:, i * LANES:(i + 1) * LANES]
    lhs = {1: xn}
    for dil in dils:
        if dil not in lhs:
            rows = [jnp.concatenate([stg[i, pl.ds(r, tm // dil, stride=dil), :] for i in range(n_slab)], axis=1)
                    for r in range(dil)]
            lhs[dil] = jnp.concatenate(rows, axis=0).astype(BF16)

    def put(out_ref, val, dil):
        n = tm // dil
        for r in range(dil):
            out_ref[0, r] = val[r * n:(r + 1) * n, :].astype(out_ref.dtype)

    def head_norm(a, gain):
        sq = (a * a).astype(BF16)
        ss = jnp.concatenate([_mm(sq[:, i * MXU_DIM:(i + 1) * MXU_DIM], seg)
                              for i in range(D_ATT // MXU_DIM)], axis=1)
        return a * lax.rsqrt(ss * (1.0 / HEAD_DIM) + RMS_EPS) * gain

    for g in range(N_GROUPS):
        q_ref, k_ref, v_ref = qkv_refs[3 * g:3 * g + 3]
        lo = g * D_ATT
        a = lhs[dils[g]]
        q = _mm(a, win_ref[:, OFF_Q + lo:OFF_Q + lo + D_ATT])
        put(q_ref, head_norm(q, qg_ref[...] * ATT_SCALE), dils[g])
        k = _mm(a, win_ref[:, OFF_K + lo:OFF_K + lo + D_ATT])
        put(k_ref, head_norm(k, kg_ref[...]), dils[g])
        put(v_ref, _mm(a, win_ref[:, OFF_V + lo:OFF_V + lo + D_ATT]), dils[g])


def _in_proj(x, shift1, scale1, g1, w_in, conv_w, w_conv_out, qg, kg, *, tm, dils, hist1=None, hist2=None,
             dec_seq=1, cache_args=None):
    nb, ns, d = x.shape
    decode = hist1 is not None
    assert ns % tm == 0 and all(tm % (dil * 2 * SUBLANES) == 0 for dil in dils)
    nmod = shift1.shape[1]
    assert nmod == 1 or (tm * nmod) % ns == 0
    tmod = 1 if nmod == 1 else tm * nmod // ns
    row = lambda b, s: (b, s, 0)
    mod_map = (lambda b, s: (b, 0, 0)) if nmod == 1 else row
    act_dtype = F32 if decode else BF16
    in_specs = [pl.BlockSpec((1, tm, d), row),
                pl.BlockSpec((1, tmod, d), mod_map),
                pl.BlockSpec((1, tmod, d), mod_map),
                _const_spec((1, d)),
                _const_spec((d, D_PROJ)),
                _const_spec((CONV_WIDTH, D_CONV)),
                _const_spec((D_CONV, D_MODEL)),
                _const_spec((1, D_ATT)),
                _const_spec((1, D_ATT))]
    args = [x, shift1, scale1, g1, w_in, conv_w, w_conv_out, qg, kg]
    if decode:
        in_specs += [pl.BlockSpec((1, tm, D_CONV), row), pl.BlockSpec((1, tm, D_CONV), row)]
        args += [hist1, hist2]
    out_shape = [jax.ShapeDtypeStruct((nb, ns, D_MODEL), BF16),
                 jax.ShapeDtypeStruct((nb, ns, D_MODEL), BF16)]
    out_specs = [pl.BlockSpec((1, tm, D_MODEL), row),
                 pl.BlockSpec((1, tm, D_MODEL), row)]
    for dil in dils:
        for _ in range(3):
            out_shape.append(jax.ShapeDtypeStruct((nb, dil, ns // dil, D_ATT), act_dtype))
            out_specs.append(pl.BlockSpec((1, dil, tm // dil, D_ATT), lambda b, s: (b, 0, s, 0)))
    if decode:
        out_shape.append(jax.ShapeDtypeStruct((nb, ns, D_CONV), F32))
        out_specs.append(pl.BlockSpec((1, tm, D_CONV), row))
    else:
        out_shape.append(jax.ShapeDtypeStruct((nb, SUBLANES, D_CONV), F32))
        out_specs.append(pl.BlockSpec((1, SUBLANES, D_CONV), lambda b, s: (b, 0, 0)))
    n_out = len(out_specs)
    cache_kw = None
    if cache_args is not None:
        nst = ns // tm
        c_in, c_out, c_shape, cache_kw = _cache_specs(cache_args, nb * nst, lambda b, s: b * nst + s)
        in_specs += c_in
        args += list(cache_args)
        out_specs += c_out
        out_shape += c_shape
    outs = pl.pallas_call(
        functools.partial(_in_kernel, decode=decode, tm=tm, dec_seq=dec_seq, dils=tuple(dils),
                          cache_kw=cache_kw),
        grid=(nb, ns // tm),
        in_specs=in_specs,
        out_specs=out_specs,
        out_shape=out_shape,
        scratch_shapes=[pltpu.VMEM((tm + SUBLANES, D_CONV), F32),
                        pltpu.VMEM((D_MODEL // LANES, tm, LANES), F32)],
        compiler_params=pltpu.CompilerParams(dimension_semantics=("arbitrary", "arbitrary"),
                                             vmem_limit_bytes=VMEM_LIMIT_BYTES),
        name="in_proj_decode" if decode else "in_proj",
    )(*args)
    return outs[0], outs[1], outs[2:2 + 3 * N_GROUPS], outs[n_out - 1], outs[n_out:]


HEADS_PER_PASS = MXU_DIM // HEAD_DIM


def _attn_blocks(q_ref, kc_ref, kp_ref, vc_ref, vp_ref, tb_ref, o_ref, lse_ref, *, nb, first):
    blk = ATT_BLOCK
    lane = lax.broadcasted_iota(jnp.int32, (1, MXU_DIM), 1)
    hmask = [(lane >= HEAD_DIM * h) & (lane < HEAD_DIM * (h + 1)) for h in range(HEADS_PER_PASS)]
    col = lax.broadcasted_iota(jnp.int32, (1, 2 * blk), 1)
    lane_out = lax.broadcasted_iota(jnp.int32, (1, LANES), 1)
    for n in range(nb):
        rows = slice(n * blk, (n + 1) * blk)
        lse_acc = jnp.zeros((blk, LANES), F32)
        for hp in range(N_HEADS // HEADS_PER_PASS):
            cols = slice(hp * MXU_DIM, (hp + 1) * MXU_DIM)
            q4 = q_ref[0, 0, rows, cols]
            lhs = jnp.concatenate([jnp.where(hmask[h], q4, jnp.zeros_like(q4))
                                   for h in range(HEADS_PER_PASS)], axis=0)
            if n == 0:
                k_prev, v_prev = kp_ref[0, 0, :, cols], vp_ref[0, 0, :, cols]
            else:
                prev = slice((n - 1) * blk, n * blk)
                k_prev, v_prev = kc_ref[0, 0, prev, cols], vc_ref[0, 0, prev, cols]
            kk = jnp.concatenate([k_prev, kc_ref[0, 0, rows, cols]], axis=0)
            vv = jnp.concatenate([v_prev, vc_ref[0, 0, rows, cols]], axis=0)
            s = lax.dot_general(lhs, kk, (((1,), (1,)), ((), ())), preferred_element_type=F32)
            s = s + tb_ref[hp * HEADS_PER_PASS * blk:(hp + 1) * HEADS_PER_PASS * blk, :]
            if n == 0:
                s = jnp.where(first & (col < blk), NEG, s)
            m = jnp.max(s, axis=-1, keepdims=True)
            p = jnp.exp(s - m)
            l = jnp.sum(p, axis=-1, keepdims=True)
            pv = _mm(p.astype(BF16), vv) * (1.0 / l)
            lse = m + jnp.log(l)
            o4 = jnp.zeros((blk, MXU_DIM), F32)
            for h in range(HEADS_PER_PASS):
                hr = slice(h * blk, (h + 1) * blk)
                o4 = jnp.where(hmask[h], pv[hr, :], o4)
                lse_acc = jnp.where(lane_out == hp * HEADS_PER_PASS + h, lse[hr, :], lse_acc)
            o_ref[0, 0, rows, cols] = o4.astype(o_ref.dtype)
        lse_ref[0, 0, rows, :] = lse_acc


def _shifted_cache(src_ref, tail, dst_ref, i, *, width, dec_seq):
    keep = LANES - dec_seq
    lane = lax.broadcasted_iota(jnp.int32, (1, LANES), 1)
    nch = width // LANES
    cur = pltpu.roll(src_ref[i, :, 0:LANES], keep, 1)
    for c in range(nch):
        nxt = pltpu.roll(src_ref[i, :, (c + 1) * LANES:(c + 2) * LANES], keep, 1) if c + 1 < nch else tail
        dst_ref[i, :, c * LANES:(c + 1) * LANES] = jnp.where(lane < keep, cur, nxt)
        cur = nxt


def _decode_scores(q, k_tail, kt_ref, bias_ref, skt_ref, i, *, width, dec_seq):
    nrow = dec_seq * N_HEADS
    own = (lax.broadcasted_iota(jnp.int32, (nrow, D_ATT), 0) % N_HEADS
           == lax.broadcasted_iota(jnp.int32, (nrow, D_ATT), 1) // HEAD_DIM)
    q_rows = jnp.concatenate([jnp.broadcast_to(q[t:t + 1, :], (N_HEADS, D_ATT)) for t in range(dec_seq)], axis=0)
    q_bd = jnp.where(own, q_rows, 0.0).astype(BF16)
    s_c = _mm(q_bd, kt_ref[i].astype(BF16)) + bias_ref[:, 0:width]
    s_t = _mm(q_bd, k_tail.astype(BF16)) + bias_ref[:, width:width + LANES]
    _shifted_cache(kt_ref, k_tail, skt_ref, i, width=width, dec_seq=dec_seq)
    return s_c, s_t


def _decode_finish(s_c, s_t, v_tail, vt_ref, svt_ref, i, *, width, dec_seq):
    nrow = dec_seq * N_HEADS
    own = (lax.broadcasted_iota(jnp.int32, (nrow, D_ATT), 0) % N_HEADS
           == lax.broadcasted_iota(jnp.int32, (nrow, D_ATT), 1) // HEAD_DIM)
    m = jnp.maximum(jnp.max(s_c, axis=-1, keepdims=True), jnp.max(s_t, axis=-1, keepdims=True))
    p_c = jnp.exp(s_c - m)
    p_t = jnp.exp(s_t - m)
    l = jnp.sum(p_c, axis=-1, keepdims=True) + jnp.sum(p_t, axis=-1, keepdims=True)
    nt = (((1,), (1,)), ((), ()))
    o = (lax.dot_general(p_c.astype(BF16), vt_ref[i].astype(BF16), nt, preferred_element_type=F32)
         + lax.dot_general(p_t.astype(BF16), v_tail.astype(BF16), nt, preferred_element_type=F32))
    o = jnp.where(own, o, 0.0) * (1.0 / l)
    o_tok = jnp.sum(o.reshape(dec_seq, N_HEADS, D_ATT), axis=1)
    lse = m + jnp.log(l)
    head_lane = (lax.broadcasted_iota(jnp.int32, (nrow, LANES), 0) % N_HEADS
                 == lax.broadcasted_iota(jnp.int32, (nrow, LANES), 1))
    lse_tok = jnp.sum(jnp.where(head_lane, lse, 0.0).reshape(dec_seq, N_HEADS, LANES), axis=1)
    _shifted_cache(vt_ref, v_tail, svt_ref, i, width=width, dec_seq=dec_seq)
    return o_tok, lse_tok


N_CACHE_IN, N_CACHE_OUT = 6, 4


def _cache_specs(cache_args, n_steps, step_of):
    q, _, _, kt, _, bias = cache_args
    db, dec_seq, _ = q.shape
    width = kt.shape[-1]
    pack = DEC_PACK // dec_seq
    assert db % n_steps == 0 and db % pack == 0 and width % LANES == 0
    bb = db // n_steps
    assert pack % bb == 0 and bb <= 4
    per_b = lambda *i: (step_of(*i), 0, 0)
    packed = lambda *i: (step_of(*i) * bb // pack, 0, 0)
    in_specs = [pl.BlockSpec((bb, dec_seq, D_ATT), per_b),
                pl.BlockSpec((1, D_ATT, DEC_PACK), packed),
                pl.BlockSpec((1, D_ATT, DEC_PACK), packed),
                pl.BlockSpec((bb, D_ATT, width), per_b),
                pl.BlockSpec((bb, D_ATT, width), per_b),
                pl.BlockSpec(bias.shape, lambda *i: (0, 0))]
    out_specs = [pl.BlockSpec((bb, D_ATT, width), per_b),
                 pl.BlockSpec((bb, D_ATT, width), per_b),
                 pl.BlockSpec((bb, dec_seq, D_ATT), per_b),
                 pl.BlockSpec((bb, dec_seq, LANES), per_b)]
    out_shape = [jax.ShapeDtypeStruct((db, D_ATT, width), F32),
                 jax.ShapeDtypeStruct((db, D_ATT, width), F32),
                 jax.ShapeDtypeStruct((db, dec_seq, D_ATT), F32),
                 jax.ShapeDtypeStruct((db, dec_seq, LANES), F32)]
    return in_specs, out_specs, out_shape, dict(width=width, dec_seq=dec_seq, bb=bb)


def _cache_scores(step, in_refs, out_refs, *, width, dec_seq, bb):
    q_ref, knp_ref, _, kt_ref, _, bias_ref = in_refs
    pack = DEC_PACK // dec_seq
    scores = []
    for i in range(bb):
        slot = ((step * bb + i) % pack) * dec_seq
        to_tail = (2 * LANES - dec_seq - slot) % LANES
        k_tail = pltpu.roll(knp_ref[0], to_tail, 1)
        scores.append(_decode_scores(q_ref[i], k_tail, kt_ref, bias_ref, out_refs[0], i,
                                     width=width, dec_seq=dec_seq))
    return scores


def _cache_finish(scores, step, in_refs, out_refs, *, width, dec_seq, bb):
    vnp_ref, vt_ref = in_refs[2], in_refs[4]
    _, svt_ref, o_ref, lse_ref = out_refs
    pack = DEC_PACK // dec_seq
    for i in range(bb):
        slot = ((step * bb + i) % pack) * dec_seq
        to_tail = (2 * LANES - dec_seq - slot) % LANES
        v_tail = pltpu.roll(vnp_ref[0], to_tail, 1)
        o_ref[i], lse_ref[i] = _decode_finish(*scores[i], v_tail, vt_ref, svt_ref, i, width=width, dec_seq=dec_seq)


def _attn_cache_kernel(*refs, nb, rows_blocks, cache_kw):
    n_in = N_CACHE_IN + 6 * N_GROUPS
    att_in = refs[N_CACHE_IN:n_in]
    att_out = refs[n_in + N_CACHE_OUT:]
    step = pl.program_id(0)
    cache_in, cache_out = refs[:N_CACHE_IN], refs[n_in:n_in + N_CACHE_OUT]
    scores = _cache_scores(step, cache_in, cache_out, **cache_kw)
    _cache_finish(scores, step, cache_in, cache_out, **cache_kw)
    for g in range(N_GROUPS):
        first = (step * nb) % rows_blocks[g] == 0
        _attn_blocks(*att_in[6 * g:6 * g + 6], *att_out[2 * g:2 * g + 2], nb=nb, first=first)


def _attention_and_cache(qkv, tbs, cache_args):
    db = cache_args[0].shape[0]
    nbatch = qkv[0].shape[0]
    total_blocks = nbatch * qkv[0].shape[1] * qkv[0].shape[2] // ATT_BLOCK
    assert total_blocks % db == 0
    nb = total_blocks // db
    tq = nb * ATT_BLOCK
    in_specs, out_specs, out_shape, cache_kw = _cache_specs(cache_args, db, lambda s: s)
    args = list(cache_args)
    rows_blocks = []
    for g in range(N_GROUPS):
        q, k, v = qkv[3 * g:3 * g + 3]
        _, dil, nl, _ = q.shape
        rb = nl // ATT_BLOCK
        assert nl % ATT_BLOCK == 0 and rb % nb == 0
        rows_blocks.append(rb)

        def cur(s, rb=rb, dil=dil):
            blk = s * nb
            return (blk // (dil * rb), (blk // rb) % dil, (blk % rb) // nb, 0)

        def prev(s, rb=rb, dil=dil):
            blk = s * nb
            return (blk // (dil * rb), (blk // rb) % dil, jnp.maximum(blk % rb - 1, 0), 0)

        in_specs += [pl.BlockSpec((1, 1, tq, D_ATT), cur),
                     pl.BlockSpec((1, 1, tq, D_ATT), cur),
                     pl.BlockSpec((1, 1, ATT_BLOCK, D_ATT), prev),
                     pl.BlockSpec((1, 1, tq, D_ATT), cur),
                     pl.BlockSpec((1, 1, ATT_BLOCK, D_ATT), prev),
                     pl.BlockSpec((N_HEADS * ATT_BLOCK, 2 * ATT_BLOCK), lambda s: (0, 0))]
        args += [q, k, k, v, v, tbs[g]]
        out_specs += [pl.BlockSpec((1, 1, tq, D_ATT), cur), pl.BlockSpec((1, 1, tq, LANES), cur)]
        out_shape += [jax.ShapeDtypeStruct(q.shape, BF16),
                      jax.ShapeDtypeStruct(q.shape[:3] + (LANES,), F32)]
    outs = pl.pallas_call(
        functools.partial(_attn_cache_kernel, nb=nb, rows_blocks=tuple(rows_blocks), cache_kw=cache_kw),
        grid=(db,),
        in_specs=in_specs,
        out_specs=out_specs,
        out_shape=out_shape,
        compiler_params=pltpu.CompilerParams(dimension_semantics=("arbitrary",),
                                             vmem_limit_bytes=VMEM_LIMIT_BYTES),
        name="attn_and_cache",
    )(*args)
    return outs[N_CACHE_OUT:], outs[:N_CACHE_OUT]


def _out_kernel(*refs, tm, dils, cache_kw):
    x_ref, cm_ref, sga_ref = refs[:3]
    o_refs = refs[3:3 + N_GROUPS]
    l_refs = refs[3 + N_GROUPS:3 + 2 * N_GROUPS]
    n_in = 3 + 2 * N_GROUPS + 9
    g1_ref, sh2_ref, sc2_ref, g2_ref, n2_ref, wao_ref, wo_ref, w1_ref, w2_ref = refs[3 + 2 * N_GROUPS:n_in]
    cache_in = refs[n_in:n_in + N_CACHE_IN]
    if cache_kw is not None:
        n_in += N_CACHE_IN
    y_ref = refs[n_in]
    cache_out = refs[n_in + 1:n_in + 1 + N_CACHE_OUT]
    ostg, lstg = refs[-2:]

    def token_order(ref, stg, dil):
        if dil == 1:
            return ref[0, 0].astype(F32)
        n_slab = ref.shape[-1] // LANES
        for r in range(dil):
            blk = ref[0, r].astype(F32)
            for i in range(n_slab):
                stg[i, pl.ds(r, tm // dil, stride=dil), :] = blk[:, i * LANES:(i + 1) * LANES]
        return jnp.concatenate([stg[i] for i in range(n_slab)], axis=1)

    lses = [token_order(l_refs[g], lstg, dils[g]) for g in range(N_GROUPS)]
    mx = jnp.maximum(jnp.maximum(lses[0], lses[1]), lses[2])
    es = [jnp.exp(l - mx) for l in lses]
    inv = 1.0 / (es[0] + es[1] + es[2])
    er = lax.broadcasted_iota(jnp.int32, (2 * LANES, D_ATT), 0) % LANES
    ec = lax.broadcasted_iota(jnp.int32, (2 * LANES, D_ATT), 1) // HEAD_DIM
    expand = jnp.where(er == ec, 1.0, 0.0).astype(BF16)
    o = None
    for g in range(N_GROUPS):
        w = es[g] * inv
        w_hi = w.astype(BF16)
        w_lo = (w - w_hi.astype(F32)).astype(BF16)
        wexp = _mm(jnp.concatenate([w_hi, w_lo], axis=1), expand)
        term = wexp * token_order(o_refs[g], ostg, dils[g])
        o = term if o is None else o + term
    attn_out = _mm(o.astype(BF16), wao_ref[...])
    mixed = cm_ref[0].astype(F32) + sga_ref[0].astype(F32) * attn_out
    x1 = x_ref[0] + _mod_rows(g1_ref, tm) * _mm(mixed.astype(BF16), wo_ref[...])
    ms = jnp.mean(x1 * x1, axis=-1, keepdims=True)
    xn2 = x1 * lax.rsqrt(ms + RMS_EPS) * n2_ref[...]
    xn2 = (xn2 * (1.0 + _mod_rows(sc2_ref, tm)) + _mod_rows(sh2_ref, tm)).astype(BF16)
    acc = None
    n_ff = D_FF // FF_CHUNK
    step = pl.program_id(0) * pl.num_programs(1) + pl.program_id(1)
    for f in range(n_ff):
        if cache_kw is not None and f == n_ff // 2:
            scores = _cache_scores(step, cache_in, cache_out, **cache_kw)
        cols = slice(f * FF_CHUNK, (f + 1) * FF_CHUNK)
        hid = jnp.maximum(_mm(xn2, w1_ref[:, cols]), 0.0)
        part = _mm((hid * hid).astype(BF16), w2_ref[cols, :])
        acc = part if acc is None else acc + part
        if cache_kw is not None and f == n_ff // 2:
            _cache_finish(scores, step, cache_in, cache_out, **cache_kw)
    y_ref[0] = x1 + _mod_rows(g2_ref, tm) * acc


def _out_proj(x, cm, sga, os, lses, gate1, shift2, scale2, gate2, n2, wao, wo, w1, w2, *, tm, cache_args=None):
    nb, ns, d = x.shape
    dils = tuple(o.shape[1] for o in os)
    assert ns % tm == 0 and all(tm % (dil * 2 * SUBLANES) == 0 for dil in dils)
    nmod = gate1.shape[1]
    assert nmod == 1 or (tm * nmod) % ns == 0
    tmod = 1 if nmod == 1 else tm * nmod // ns
    row = lambda b, s: (b, s, 0)
    mod_map = (lambda b, s: (b, 0, 0)) if nmod == 1 else row
    mod_spec = pl.BlockSpec((1, tmod, d), mod_map)
    res_spec = lambda dil, n: pl.BlockSpec((1, dil, tm // dil, n), lambda b, s: (b, 0, s, 0))
    in_specs = ([pl.BlockSpec((1, tm, d), row)] * 3
                + [res_spec(dil, D_ATT) for dil in dils]
                + [res_spec(dil, LANES) for dil in dils]
                + [mod_spec] * 4
                + [_const_spec((1, d)), _const_spec((D_ATT, d)), _const_spec((d, d)),
                   _const_spec((d, D_FF)), _const_spec((D_FF, d))])
    args = [x, cm, sga, *os, *lses, gate1, shift2, scale2, gate2, n2, wao, wo, w1, w2]
    out_specs = [pl.BlockSpec((1, tm, d), row)]
    out_shape = [jax.ShapeDtypeStruct((nb, ns, d), F32)]
    cache_kw = None
    if cache_args is not None:
        nst = ns // tm
        c_in, c_out, c_shape, cache_kw = _cache_specs(cache_args, nb * nst, lambda b, s: b * nst + s)
        in_specs += c_in
        args += list(cache_args)
        out_specs += c_out
        out_shape += c_shape
    outs = pl.pallas_call(
        functools.partial(_out_kernel, tm=tm, dils=dils, cache_kw=cache_kw),
        grid=(nb, ns // tm),
        in_specs=in_specs,
        out_specs=out_specs,
        out_shape=out_shape,
        scratch_shapes=[pltpu.VMEM((D_ATT // LANES, tm, LANES), F32), pltpu.VMEM((1, tm, LANES), F32)],
        compiler_params=pltpu.CompilerParams(dimension_semantics=("arbitrary", "arbitrary"),
                                             vmem_limit_bytes=VMEM_LIMIT_BYTES),
        name="out_proj",
    )(*args)
    return outs[0] if cache_args is None else outs


def _tail_kernel(k_ref, v_ref, ko_ref, vo_ref, stg, *, dil, rows):
    for src, dst in ((k_ref, ko_ref), (v_ref, vo_ref)):
        if dil == 1:
            x = src[0, 0].astype(F32)
        else:
            for r in range(dil):
                blk = src[0, r].astype(F32)
                for i in range(D_ATT // LANES):
                    stg[i, pl.ds(r, rows // dil, stride=dil), :] = blk[:, i * LANES:(i + 1) * LANES]
            x = jnp.concatenate([stg[i] for i in range(D_ATT // LANES)], axis=1)
        dst[0] = x.T


def _state_tail(k, v, keep):
    nbatch, dil, nl, _ = k.shape
    rows = min(keep, TM_OUT)
    assert keep % rows == 0 and (nl * dil - keep) % rows == 0 and rows % (dil * 2 * SUBLANES) == 0
    first = (nl * dil - keep) // rows
    src = pl.BlockSpec((1, dil, rows // dil, D_ATT), lambda b, j: (b, 0, first + j, 0))
    dst = pl.BlockSpec((1, D_ATT, rows), lambda b, j: (b, 0, j))
    return pl.pallas_call(
        functools.partial(_tail_kernel, dil=dil, rows=rows),
        grid=(nbatch, keep // rows),
        in_specs=[src, src],
        out_specs=[dst, dst],
        out_shape=[jax.ShapeDtypeStruct((nbatch, D_ATT, keep), F32)] * 2,
        scratch_shapes=[pltpu.VMEM((D_ATT // LANES, rows, LANES), F32)],
        compiler_params=pltpu.CompilerParams(dimension_semantics=("arbitrary", "arbitrary"),
                                             vmem_limit_bytes=VMEM_LIMIT_BYTES),
        name=f"state_tail_d{dil}",
    )(k, v)


def _t5_causal_bucket(dist):
    max_exact = N_BUCKETS // 2
    ratio = jnp.maximum(dist, 1).astype(F32) / max_exact
    large = max_exact + (jnp.log(ratio) / math.log(MAX_DISTANCE / max_exact)
                         * (N_BUCKETS - max_exact)).astype(jnp.int32)
    large = jnp.minimum(large, N_BUCKETS - 1)
    return jnp.where(dist < max_exact, dist, large)


def _group_bias(rel_bias, g):
    steps = jnp.arange(WK + 1)
    bucket = _t5_causal_bucket(steps * DILS[g])
    onehot = (bucket[:, None] == jnp.arange(N_BUCKETS)[None, :]).astype(F32)
    b = jnp.dot(onehot, rel_bias[:, g * N_HEADS:(g + 1) * N_HEADS].astype(F32), precision=lax.Precision.HIGHEST)
    return b.T


def _prompt_bias_table(bias):
    blk, period = ATT_BLOCK, 4 * ATT_BLOCK
    rev = bias[:, ::-1]
    row = jnp.concatenate([rev, jnp.full((N_HEADS, period - (WK + 1)), NEG, F32)], axis=1)
    tiled = jnp.tile(row, (1, blk))[:, :blk * (period - 1)].reshape(N_HEADS, blk, period - 1)
    return tiled[:, :, :2 * blk].reshape(N_HEADS * blk, 2 * blk)


def _decode_bias_table(bias, width, dil, dec_seq):
    assert width == WK * dil and (dil == 1 or dil >= dec_seq)
    rev = bias[:, ::-1]
    neg_col = jnp.full((N_HEADS, 1), NEG, F32)
    rows = []
    for t in range(dec_seq):
        if dil == 1:
            cache = jnp.pad(rev[:, :width - t], ((0, 0), (t, 0)), constant_values=NEG)
        else:
            mine = (np.arange(dil) == t)[None, None, :]
            cache = jnp.where(mine, rev[:, :WK, None], NEG).reshape(N_HEADS, width)
        new = [bias[:, (t - tn) // dil:(t - tn) // dil + 1] if tn <= t and (t - tn) % dil == 0 else neg_col
               for tn in range(dec_seq)]
        rows.append(jnp.concatenate([cache, jnp.full((N_HEADS, LANES - dec_seq), NEG, F32)] + new, axis=1))
    return jnp.concatenate(rows, axis=0)


def _layer(x_prompt, x_sample, c_prompt, c_sample, state_conv, caches, rel_bias, norm1_g, norm2_g,
           w_ada, b_ada, w_in, conv_w, q_norm_g, k_norm_g, w_conv_out, w_attn_out, w_o, w_mlp_in,
           w_mlp_out):
    nbatch, ns, d = x_prompt.shape
    db, dec_seq, _ = x_sample.shape
    ntok = db * dec_seq

    w_in_b = w_in.astype(BF16)
    wco_b = w_conv_out.astype(BF16)
    wao_b = w_attn_out.astype(BF16)
    wo_b = w_o.astype(BF16)
    w1_b = w_mlp_in.astype(BF16)
    w2_b = w_mlp_out.astype(BF16)
    g1 = norm1_g.reshape(1, d)
    g2n = norm2_g.reshape(1, d)
    qg = jnp.tile(q_norm_g.reshape(1, HEAD_DIM), (1, N_HEADS))
    kg = jnp.tile(k_norm_g.reshape(1, HEAD_DIM), (1, N_HEADS))

    n_c = nbatch + db
    n_pad = -(-n_c // SUBLANES) * SUBLANES
    c_all = jnp.concatenate([c_prompt, c_sample, jnp.zeros((n_pad - n_c, d), F32)], axis=0)
    mod = _ada(c_all, w_ada, b_ada)
    mod_p = mod[:nbatch].reshape(nbatch, 1, N_MOD, d)
    mod_s = mod[nbatch:n_c].reshape(1, db, N_MOD, d)
    mp = [mod_p[:, :, i] for i in range(N_MOD)]
    msn = [mod_s[:, :, i] for i in range(N_MOD)]

    biases = [_group_bias(rel_bias, g) for g in range(N_GROUPS)]

    xs = x_sample.reshape(1, ntok, d)
    zero_row = jnp.zeros((db, 1, D_CONV), F32)
    hist1 = jnp.concatenate([state_conv[:, 1:2]] + [zero_row] * (dec_seq - 1), axis=1).reshape(1, ntok, D_CONV)
    hist2 = jnp.concatenate([state_conv[:, 0:1], state_conv[:, 1:2]] + [zero_row] * (dec_seq - 2),
                            axis=1).reshape(1, ntok, D_CONV)
    cm_s, sga_s, qkv_s, u_s, _ = _in_proj(xs, msn[0], msn[1], g1, w_in_b, conv_w, wco_b, qg, kg, tm=ntok,
                                          dils=(1,) * N_GROUPS, hist1=hist1, hist2=hist2, dec_seq=dec_seq)
    s_conv = u_s.reshape(db, dec_seq, D_CONV)[:, dec_seq - (CONV_WIDTH - 1):]

    def pack_cols(a):
        return a.reshape(ntok // DEC_PACK, DEC_PACK, D_ATT).transpose(0, 2, 1)

    cache_args = []
    for g, (window, dil) in enumerate(DILATED_GROUPS):
        ck, cv = caches[2 * g], caches[2 * g + 1]
        width = ck.shape[1]
        assert width == window
        kt = ck.transpose(0, 2, 3, 1).reshape(db, D_ATT, width)
        vt = cv.transpose(0, 2, 3, 1).reshape(db, D_ATT, width)
        cache_args.append((qkv_s[3 * g].reshape(db, dec_seq, D_ATT), pack_cols(qkv_s[3 * g + 1]),
                           pack_cols(qkv_s[3 * g + 2]), kt, vt,
                           _decode_bias_table(biases[g], width, dil, dec_seq)))
    by_width = sorted(range(N_GROUPS), key=lambda g: DILATED_GROUPS[g][0])
    on_in_proj, on_out_proj, on_attention = by_width[0], by_width[1], by_width[2]
    cache_out = [None] * N_GROUPS

    cm, sga, qkv, utail, cache_out[on_in_proj] = _in_proj(
        x_prompt, mp[0], mp[1], g1, w_in_b, conv_w, wco_b, qg, kg, tm=TM_IN, dils=DILS,
        cache_args=cache_args[on_in_proj])
    p_conv = utail[:, SUBLANES - (CONV_WIDTH - 1):, :]
    p_kv = []
    for g, (window, dil) in enumerate(DILATED_GROUPS):
        keep = min(window, ns)
        for tail in _state_tail(qkv[3 * g + 1], qkv[3 * g + 2], keep):
            p_kv.append(tail.reshape(nbatch, N_HEADS, HEAD_DIM, keep).transpose(0, 3, 1, 2))
    prompt_att, cache_out[on_attention] = _attention_and_cache(
        qkv, [_prompt_bias_table(b) for b in biases], cache_args[on_attention])
    y_prompt, *cache_out[on_out_proj] = _out_proj(
        x_prompt, cm, sga, prompt_att[0::2], prompt_att[1::2], mp[2], mp[3], mp[4], mp[5], g2n,
        wao_b, wo_b, w1_b, w2_b, tm=TM_OUT, cache_args=cache_args[on_out_proj])

    os_s, lses_s, s_kv = [], [], []
    for g, (window, _) in enumerate(DILATED_GROUPS):
        skt, svt, o_s, lse_s = cache_out[g]
        s_kv.append(skt.reshape(db, N_HEADS, HEAD_DIM, window).transpose(0, 3, 1, 2))
        s_kv.append(svt.reshape(db, N_HEADS, HEAD_DIM, window).transpose(0, 3, 1, 2))
        os_s.append(o_s.reshape(1, 1, ntok, D_ATT).astype(BF16))
        lses_s.append(lse_s.reshape(1, 1, ntok, LANES))
    y_sample = _out_proj(xs, cm_s, sga_s, os_s, lses_s, msn[2], msn[3], msn[4], msn[5], g2n,
                         wao_b, wo_b, w1_b, w2_b, tm=ntok).reshape(db, dec_seq, d)
    return y_prompt, y_sample, [p_conv] + p_kv, [s_conv] + s_kv


def kernel(x_prompt, x_sample, c_prompt, c_sample, state_conv, cache_k1, cache_v1, cache_k2, cache_v2,
           cache_k3, cache_v3, rel_bias, norm1_g, norm2_g, w_ada, b_ada, w_in, conv_w, q_norm_g, k_norm_g,
           w_conv_out, w_attn_out, w_o, w_mlp_in, w_mlp_out):
    depth = w_in.shape[0]
    caches = (cache_k1, cache_v1, cache_k2, cache_v2, cache_k3, cache_v3)
    yp, ys = x_prompt, x_sample
    p_states = [[] for _ in range(1 + 2 * N_GROUPS)]
    s_states = [[] for _ in range(1 + 2 * N_GROUPS)]
    for l in range(depth):
        yp, ys, p_new, s_new = _layer(
            yp, ys, c_prompt, c_sample, state_conv[l], [c[l] for c in caches], rel_bias,
            norm1_g[l], norm2_g[l], w_ada[l], b_ada[l], w_in[l], conv_w[l], q_norm_g[l], k_norm_g[l],
            w_conv_out[l], w_attn_out[l], w_o[l], w_mlp_in[l], w_mlp_out[l])
        for lst, a in zip(p_states, p_new):
            lst.append(a)
        for lst, a in zip(s_states, s_new):
            lst.append(a)
    p_out = [jnp.stack(a) for a in p_states]
    s_out = [jnp.stack(a) for a in s_states]
    return (yp, ys, *p_out, *s_out)
```

```python
import functools
import math

import numpy as np
import jax
import jax.numpy as jnp
from jax import lax
from jax.experimental import pallas as pl
from jax.experimental.pallas import tpu as pltpu

F32 = jnp.float32
BF16 = jnp.bfloat16

D_MODEL = 1024
D_CONV = D_MODEL
CONV_WIDTH = 3
HEAD_DIM = 64
N_HEADS = 8
D_ATT = N_HEADS * HEAD_DIM
DILATED_GROUPS = ((128, 1), (512, 4), (2048, 16))
N_GROUPS = len(DILATED_GROUPS)
DILS = tuple(d for _, d in DILATED_GROUPS)
D_QKV = N_GROUPS * D_ATT
D_FF = 4 * D_MODEL
N_BUCKETS = 32
MAX_DISTANCE = 2048
ATT_BLOCK = 128
WK = 128
N_MOD = 6
RMS_EPS = 1e-6
ATT_SCALE = HEAD_DIM ** -0.5
NEG = -1e30

OFF_H, OFF_B, OFF_C = 0, D_CONV, 2 * D_CONV
OFF_Q = 3 * D_CONV
OFF_K = OFF_Q + D_QKV
OFF_V = OFF_K + D_QKV
OFF_GC = OFF_V + D_QKV
OFF_GA = OFF_GC + D_MODEL
D_PROJ = OFF_GA + D_MODEL

LANES = 128
SUBLANES = 8
MXU_DIM = 256
VMEM_LIMIT_BYTES = 60 * 1024 * 1024

TM_IN = 512
TM_OUT = 512
FF_CHUNK = 512
DEC_PACK = LANES

assert all(w // d == WK for w, d in DILATED_GROUPS)


def _mm(a, b):
    return jnp.dot(a, b, preferred_element_type=F32)


def _mod_rows(ref, tm):
    m = ref[0]
    n = m.shape[0]
    if n == 1 or n == tm:
        return m
    covered = lax.broadcasted_iota(jnp.int32, (tm, n), 0) // (tm // n)
    sel = jnp.where(covered == lax.broadcasted_iota(jnp.int32, (tm, n), 1), 1.0, 0.0).astype(BF16)
    hi = m.astype(BF16)
    rest = m - hi.astype(F32)
    mid = rest.astype(BF16)
    lo = (rest - mid.astype(F32)).astype(BF16)
    return (_mm(sel, hi) + _mm(sel, mid)) + _mm(sel, lo)


def _const_spec(shape):
    nd = len(shape)
    return pl.BlockSpec(shape, lambda *_: (0,) * nd, pipeline_mode=pl.Buffered(1))


def _ada_kernel(c_ref, w_ref, b_ref, o_ref):
    c = c_ref[...]
    s = c * jax.nn.sigmoid(c)
    o_ref[...] = _mm(s.astype(BF16), w_ref[...].astype(BF16)) + b_ref[...]


def _ada(c_all, w_ada, b_ada):
    n, d = c_all.shape
    nout = w_ada.shape[1]
    tn = 1024
    return pl.pallas_call(
        _ada_kernel,
        grid=(nout // tn,),
        in_specs=[pl.BlockSpec((n, d), lambda j: (0, 0)),
                  pl.BlockSpec((d, tn), lambda j: (0, j)),
                  pl.BlockSpec((1, tn), lambda j: (0, j))],
        out_specs=pl.BlockSpec((n, tn), lambda j: (0, j)),
        out_shape=jax.ShapeDtypeStruct((n, nout), F32),
        compiler_params=pltpu.CompilerParams(dimension_semantics=("arbitrary",),
                                             vmem_limit_bytes=VMEM_LIMIT_BYTES),
        name="ada",
    )(c_all, w_ada, b_ada.reshape(1, nout))


def _in_kernel(*refs, decode, tm, dec_seq, dils, cache_kw):
    n_in = 11 if decode else 9
    x_ref, sh_ref, sc_ref, g1_ref, win_ref, cw_ref, wco_ref, qg_ref, kg_ref = refs[:9]
    if decode:
        s1_ref, s2_ref = refs[9:11]
    n_out = 3 + 3 * N_GROUPS
    cache_in = refs[n_in:n_in + N_CACHE_IN]
    if cache_kw is not None:
        n_in += N_CACHE_IN
    cm_ref, sga_ref = refs[n_in:n_in + 2]
    qkv_refs = refs[n_in + 2:n_in + 2 + 3 * N_GROUPS]
    ustate_ref = refs[n_in + n_out - 1]
    cache_out = refs[n_in + n_out:n_in + n_out + N_CACHE_OUT]
    uext, stg = refs[-2:]
    step = pl.program_id(0) * pl.num_programs(1) + pl.program_id(1)

    x = x_ref[0]
    ms = jnp.mean(x * x, axis=-1, keepdims=True)
    xn = x * lax.rsqrt(ms + RMS_EPS) * g1_ref[...]
    xn_f32 = xn * (1.0 + _mod_rows(sc_ref, tm)) + _mod_rows(sh_ref, tm)
    xn = xn_f32.astype(BF16)
    n_slab = D_MODEL // LANES
    if any(dil > 1 for dil in dils):
        for i in range(n_slab):
            stg[i] = xn_f32[:, i * LANES:(i + 1) * LANES]

    h = _mm(xn, win_ref[:, OFF_H:OFF_H + D_CONV])
    c = _mm(xn, win_ref[:, OFF_C:OFF_C + D_CONV])
    u = c * h
    hist = SUBLANES
    if decode:
        uext[0:hist, :] = jnp.zeros((hist, D_CONV), F32)
    else:
        @pl.when(pl.program_id(1) == 0)
        def _():
            uext[0:hist, :] = jnp.zeros((hist, D_CONV), F32)
    uext[hist:hist + tm, :] = u
    um1 = uext[hist - 1:hist - 1 + tm, :]
    um2 = uext[hist - 2:hist - 2 + tm, :]
    if decode:
        t = lax.broadcasted_iota(jnp.int32, (tm, 1), 0) % dec_seq
        um1 = jnp.where(t >= 1, um1, s1_ref[0])
        um2 = jnp.where(t >= 2, um2, s2_ref[0])
        ustate_ref[0] = u
    else:
        uext[0:hist, :] = uext[tm:tm + hist, :]
        ustate_ref[0] = u[tm - hist:tm, :]
    y = cw_ref[0:1, :] * um2 + cw_ref[1:2, :] * um1 + cw_ref[2:3, :] * u
    if cache_kw is not None:
        scores = _cache_scores(step, cache_in, cache_out, **cache_kw)
    bg = _mm(xn, win_ref[:, OFF_B:OFF_B + D_CONV])
    conv_out = _mm((bg * y).astype(BF16), wco_ref[...])
    gc = _mm(xn, win_ref[:, OFF_GC:OFF_GC + D_MODEL])
    cm_ref[0] = (jax.nn.sigmoid(gc) * conv_out).astype(cm_ref.dtype)
    if cache_kw is not None:
        _cache_finish(scores, step, cache_in, cache_out, **cache_kw)
    ga = _mm(xn, win_ref[:, OFF_GA:OFF_GA + D_MODEL])
    sga_ref[0] = jax.nn.sigmoid(ga).astype(sga_ref.dtype)

    seg_r = lax.broadcasted_iota(jnp.int32, (MXU_DIM, MXU_DIM), 0) // HEAD_DIM
    seg_c = lax.broadcasted_iota(jnp.int32, (MXU_DIM, MXU_DIM), 1) // HEAD_DIM
    seg = jnp.where(seg_r == seg_c, 1.0, 0.0).astype(BF16)

    lhs = {1: xn}
    for dil in dils:
        if dil not in lhs:
            rows = [jnp.concatenate([stg[i, pl.ds(r, tm // dil, stride=dil), :] for i in range(n_slab)], axis=1)
                    for r in range(dil)]
            lhs[dil] = jnp.concatenate(rows, axis=0).astype(BF16)

    def put(out_ref, val, dil):
        n = tm // dil
        for r in range(dil):
            out_ref[0, r] = val[r * n:(r + 1) * n, :].astype(out_ref.dtype)

    def head_norm(a, gain):
        sq = (a * a).astype(BF16)
        ss = jnp.concatenate([_mm(sq[:, i * MXU_DIM:(i + 1) * MXU_DIM], seg)
                              for i in range(D_ATT // MXU_DIM)], axis=1)
        return a * lax.rsqrt(ss * (1.0 / HEAD_DIM) + RMS_EPS) * gain

    for g in range(N_GROUPS):
        q_ref, k_ref, v_ref = qkv_refs[3 * g:3 * g + 3]
        lo = g * D_ATT
        a = lhs[dils[g]]
        q = _mm(a, win_ref[:, OFF_Q + lo:OFF_Q + lo + D_ATT])
        put(q_ref, head_norm(q, qg_ref[...] * ATT_SCALE), dils[g])
        k = _mm(a, win_ref[:, OFF_K + lo:OFF_K + lo + D_ATT])
        put(k_ref, head_norm(k, kg_ref[...]), dils[g])
        put(v_ref, _mm(a, win_ref[:, OFF_V + lo:OFF_V + lo + D_ATT]), dils[g])


def _in_proj(x, shift1, scale1, g1, w_in, conv_w, w_conv_out, qg, kg, *, tm, dils, hist1=None, hist2=None,
             dec_seq=1, cache_args=None):
    nb, ns, d = x.shape
    decode = hist1 is not None
    assert ns % tm == 0 and all(tm % (dil * 2 * SUBLANES) == 0 for dil in dils)
    nmod = shift1.shape[1]
    assert nmod == 1 or (tm * nmod) % ns == 0
    tmod = 1 if nmod == 1 else tm * nmod // ns
    row = lambda b, s: (b, s, 0)
    mod_map = (lambda b, s: (b, 0, 0)) if nmod == 1 else row
    act_dtype = F32 if decode else BF16
    in_specs = [pl.BlockSpec((1, tm, d), row),
                pl.BlockSpec((1, tmod, d), mod_map),
                pl.BlockSpec((1, tmod, d), mod_map),
                _const_spec((1, d)),
                _const_spec((d, D_PROJ)),
                _const_spec((CONV_WIDTH, D_CONV)),
                _const_spec((D_CONV, D_MODEL)),
                _const_spec((1, D_ATT)),
                _const_spec((1, D_ATT))]
    args = [x, shift1, scale1, g1, w_in, conv_w, w_conv_out, qg, kg]
    if decode:
        in_specs += [pl.BlockSpec((1, tm, D_CONV), row), pl.BlockSpec((1, tm, D_CONV), row)]
        args += [hist1, hist2]
    out_shape = [jax.ShapeDtypeStruct((nb, ns, D_MODEL), BF16),
                 jax.ShapeDtypeStruct((nb, ns, D_MODEL), BF16)]
    out_specs = [pl.BlockSpec((1, tm, D_MODEL), row),
                 pl.BlockSpec((1, tm, D_MODEL), row)]
    for dil in dils:
        for _ in range(3):
            out_shape.append(jax.ShapeDtypeStruct((nb, dil, ns // dil, D_ATT), act_dtype))
            out_specs.append(pl.BlockSpec((1, dil, tm // dil, D_ATT), lambda b, s: (b, 0, s, 0)))
    if decode:
        out_shape.append(jax.ShapeDtypeStruct((nb, ns, D_CONV), F32))
        out_specs.append(pl.BlockSpec((1, tm, D_CONV), row))
    else:
        out_shape.append(jax.ShapeDtypeStruct((nb, SUBLANES, D_CONV), F32))
        out_specs.append(pl.BlockSpec((1, SUBLANES, D_CONV), lambda b, s: (b, 0, 0)))
    n_out = len(out_specs)
    cache_kw = None
    if cache_args is not None:
        nst = ns // tm
        c_in, c_out, c_shape, cache_kw = _cache_specs(cache_args, nb * nst, lambda b, s: b * nst + s)
        in_specs += c_in
        args += list(cache_args)
        out_specs += c_out
        out_shape += c_shape
    outs = pl.pallas_call(
        functools.partial(_in_kernel, decode=decode, tm=tm, dec_seq=dec_seq, dils=tuple(dils),
                          cache_kw=cache_kw),
        grid=(nb, ns // tm),
        in_specs=in_specs,
        out_specs=out_specs,
        out_shape=out_shape,
        scratch_shapes=[pltpu.VMEM((tm + SUBLANES, D_CONV), F32),
                        pltpu.VMEM((D_MODEL // LANES, tm, LANES), F32)],
        compiler_params=pltpu.CompilerParams(dimension_semantics=("arbitrary", "arbitrary"),
                                             vmem_limit_bytes=VMEM_LIMIT_BYTES),
        name="in_proj_decode" if decode else "in_proj",
    )(*args)
    return outs[0], outs[1], outs[2:2 + 3 * N_GROUPS], outs[n_out - 1], outs[n_out:]


HEADS_PER_PASS = MXU_DIM // HEAD_DIM


def _attn_blocks(q_ref, kc_ref, kp_ref, vc_ref, vp_ref, tb_ref, o_ref, lse_ref, *, nb, first):
    blk = ATT_BLOCK
    lane = lax.broadcasted_iota(jnp.int32, (1, MXU_DIM), 1)
    hmask = [(lane >= HEAD_DIM * h) & (lane < HEAD_DIM * (h + 1)) for h in range(HEADS_PER_PASS)]
    col = lax.broadcasted_iota(jnp.int32, (1, 2 * blk), 1)
    lane_out = lax.broadcasted_iota(jnp.int32, (1, LANES), 1)
    for n in range(nb):
        rows = slice(n * blk, (n + 1) * blk)
        lse_acc = jnp.zeros((blk, LANES), F32)
        for hp in range(N_HEADS // HEADS_PER_PASS):
            cols = slice(hp * MXU_DIM, (hp + 1) * MXU_DIM)
            q4 = q_ref[0, 0, rows, cols]
            lhs = jnp.concatenate([jnp.where(hmask[h], q4, jnp.zeros_like(q4))
                                   for h in range(HEADS_PER_PASS)], axis=0)
            if n == 0:
                k_prev, v_prev = kp_ref[0, 0, :, cols], vp_ref[0, 0, :, cols]
            else:
                prev = slice((n - 1) * blk, n * blk)
                k_prev, v_prev = kc_ref[0, 0, prev, cols], vc_ref[0, 0, prev, cols]
            kk = jnp.concatenate([k_prev, kc_ref[0, 0, rows, cols]], axis=0)
            vv = jnp.concatenate([v_prev, vc_ref[0, 0, rows, cols]], axis=0)
            s = lax.dot_general(lhs, kk, (((1,), (1,)), ((), ())), preferred_element_type=F32)
            s = s + tb_ref[hp * HEADS_PER_PASS * blk:(hp + 1) * HEADS_PER_PASS * blk, :]
            if n == 0:
                s = jnp.where(first & (col < blk), NEG, s)
            m = jnp.max(s, axis=-1, keepdims=True)
            p = jnp.exp(s - m)
            l = jnp.sum(p, axis=-1, keepdims=True)
            pv = _mm(p.astype(BF16), vv) * (1.0 / l)
            lse = m + jnp.log(l)
            o4 = jnp.zeros((blk, MXU_DIM), F32)
            for h in range(HEADS_PER_PASS):
                hr = slice(h * blk, (h + 1) * blk)
                o4 = jnp.where(hmask[h], pv[hr, :], o4)
                lse_acc = jnp.where(lane_out == hp * HEADS_PER_PASS + h, lse[hr, :], lse_acc)
            o_ref[0, 0, rows, cols] = o4.astype(o_ref.dtype)
        lse_ref[0, 0, rows, :] = lse_acc


def _shifted_cache(src_ref, tail, dst_ref, i, *, width, dec_seq):
    keep = LANES - dec_seq
    lane = lax.broadcasted_iota(jnp.int32, (1, LANES), 1)
    nch = width // LANES
    cur = pltpu.roll(src_ref[i, :, 0:LANES], keep, 1)
    for c in range(nch):
        nxt = pltpu.roll(src_ref[i, :, (c + 1) * LANES:(c + 2) * LANES], keep, 1) if c + 1 < nch else tail
        dst_ref[i, :, c * LANES:(c + 1) * LANES] = jnp.where(lane < keep, cur, nxt)
        cur = nxt


def _decode_scores(q, k_tail, kt_ref, bias_ref, skt_ref, i, *, width, dec_seq):
    nrow = dec_seq * N_HEADS
    own = (lax.broadcasted_iota(jnp.int32, (nrow, D_ATT), 0) % N_HEADS
           == lax.broadcasted_iota(jnp.int32, (nrow, D_ATT), 1) // HEAD_DIM)
    q_rows = jnp.concatenate([jnp.broadcast_to(q[t:t + 1, :], (N_HEADS, D_ATT)) for t in range(dec_seq)], axis=0)
    q_bd = jnp.where(own, q_rows, 0.0).astype(BF16)
    s_c = _mm(q_bd, kt_ref[i].astype(BF16)) + bias_ref[:, 0:width]
    s_t = _mm(q_bd, k_tail.astype(BF16)) + bias_ref[:, width:width + LANES]
    _shifted_cache(kt_ref, k_tail, skt_ref, i, width=width, dec_seq=dec_seq)
    return s_c, s_t


def _decode_finish(s_c, s_t, v_tail, vt_ref, svt_ref, i, *, width, dec_seq):
    nrow = dec_seq * N_HEADS
    own = (lax.broadcasted_iota(jnp.int32, (nrow, D_ATT), 0) % N_HEADS
           == lax.broadcasted_iota(jnp.int32, (nrow, D_ATT), 1) // HEAD_DIM)
    m = jnp.maximum(jnp.max(s_c, axis=-1, keepdims=True), jnp.max(s_t, axis=-1, keepdims=True))
    p_c = jnp.exp(s_c - m)
    p_t = jnp.exp(s_t - m)
    l = jnp.sum(p_c, axis=-1, keepdims=True) + jnp.sum(p_t, axis=-1, keepdims=True)
    nt = (((1,), (1,)), ((), ()))
    o = (lax.dot_general(p_c.astype(BF16), vt_ref[i].astype(BF16), nt, preferred_element_type=F32)
         + lax.dot_general(p_t.astype(BF16), v_tail.astype(BF16), nt, preferred_element_type=F32))
    o = jnp.where(own, o, 0.0) * (1.0 / l)
    o_tok = jnp.sum(o.reshape(dec_seq, N_HEADS, D_ATT), axis=1)
    lse = m + jnp.log(l)
    head_lane = (lax.broadcasted_iota(jnp.int32, (nrow, LANES), 0) % N_HEADS
                 == lax.broadcasted_iota(jnp.int32, (nrow, LANES), 1))
    lse_tok = jnp.sum(jnp.where(head_lane, lse, 0.0).reshape(dec_seq, N_HEADS, LANES), axis=1)
    _shifted_cache(vt_ref, v_tail, svt_ref, i, width=width, dec_seq=dec_seq)
    return o_tok, lse_tok


N_CACHE_IN, N_CACHE_OUT = 6, 4


def _cache_specs(cache_args, n_steps, step_of):
    q, _, _, kt, _, bias = cache_args
    db, dec_seq, _ = q.shape
    width = kt.shape[-1]
    pack = DEC_PACK // dec_seq
    assert db % n_steps == 0 and db % pack == 0 and width % LANES == 0
    bb = db // n_steps
    assert pack % bb == 0 and bb <= 4
    per_b = lambda *i: (step_of(*i), 0, 0)
    packed = lambda *i: (step_of(*i) * bb // pack, 0, 0)
    in_specs = [pl.BlockSpec((bb, dec_seq, D_ATT), per_b),
                pl.BlockSpec((1, D_ATT, DEC_PACK), packed),
                pl.BlockSpec((1, D_ATT, DEC_PACK), packed),
                pl.BlockSpec((bb, D_ATT, width), per_b),
                pl.BlockSpec((bb, D_ATT, width), per_b),
                pl.BlockSpec(bias.shape, lambda *i: (0, 0))]
    out_specs = [pl.BlockSpec((bb, D_ATT, width), per_b),
                 pl.BlockSpec((bb, D_ATT, width), per_b),
                 pl.BlockSpec((bb, dec_seq, D_ATT), per_b),
                 pl.BlockSpec((bb, dec_seq, LANES), per_b)]
    out_shape = [jax.ShapeDtypeStruct((db, D_ATT, width), F32),
                 jax.ShapeDtypeStruct((db, D_ATT, width), F32),
                 jax.ShapeDtypeStruct((db, dec_seq, D_ATT), F32),
                 jax.ShapeDtypeStruct((db, dec_seq, LANES), F32)]
    return in_specs, out_specs, out_shape, dict(width=width, dec_seq=dec_seq, bb=bb)


def _cache_scores(step, in_refs, out_refs, *, width, dec_seq, bb):
    q_ref, knp_ref, _, kt_ref, _, bias_ref = in_refs
    pack = DEC_PACK // dec_seq
    scores = []
    for i in range(bb):
        slot = ((step * bb + i) % pack) * dec_seq
        to_tail = (2 * LANES - dec_seq - slot) % LANES
        k_tail = pltpu.roll(knp_ref[0], to_tail, 1)
        scores.append(_decode_scores(q_ref[i], k_tail, kt_ref, bias_ref, out_refs[0], i,
                                     width=width, dec_seq=dec_seq))
    return scores


def _cache_finish(scores, step, in_refs, out_refs, *, width, dec_seq, bb):
    vnp_ref, vt_ref = in_refs[2], in_refs[4]
    _, svt_ref, o_ref, lse_ref = out_refs
    pack = DEC_PACK // dec_seq
    for i in range(bb):
        slot = ((step * bb + i) % pack) * dec_seq
        to_tail = (2 * LANES - dec_seq - slot) % LANES
        v_tail = pltpu.roll(vnp_ref[0], to_tail, 1)
        o_ref[i], lse_ref[i] = _decode_finish(*scores[i], v_tail, vt_ref, svt_ref, i, width=width, dec_seq=dec_seq)


def _attn_cache_kernel(*refs, nb, rows_blocks, cache_kw):
    n_in = N_CACHE_IN + 6 * N_GROUPS
    att_in = refs[N_CACHE_IN:n_in]
    att_out = refs[n_in + N_CACHE_OUT:]
    step = pl.program_id(0)
    cache_in, cache_out = refs[:N_CACHE_IN], refs[n_in:n_in + N_CACHE_OUT]
    scores = _cache_scores(step, cache_in, cache_out, **cache_kw)
    _cache_finish(scores, step, cache_in, cache_out, **cache_kw)
    for g in range(N_GROUPS):
        first = (step * nb) % rows_blocks[g] == 0
        _attn_blocks(*att_in[6 * g:6 * g + 6], *att_out[2 * g:2 * g + 2], nb=nb, first=first)


def _attention_and_cache(qkv, tbs, cache_args):
    db = cache_args[0].shape[0]
    nbatch = qkv[0].shape[0]
    total_blocks = nbatch * qkv[0].shape[1] * qkv[0].shape[2] // ATT_BLOCK
    assert total_blocks % db == 0
    nb = total_blocks // db
    tq = nb * ATT_BLOCK
    in_specs, out_specs, out_shape, cache_kw = _cache_specs(cache_args, db, lambda s: s)
    args = list(cache_args)
    rows_blocks = []
    for g in range(N_GROUPS):
        q, k, v = qkv[3 * g:3 * g + 3]
        _, dil, nl, _ = q.shape
        rb = nl // ATT_BLOCK
        assert nl % ATT_BLOCK == 0 and rb % nb == 0
        rows_blocks.append(rb)

        def cur(s, rb=rb, dil=dil):
            blk = s * nb
            return (blk // (dil * rb), (blk // rb) % dil, (blk % rb) // nb, 0)

        def prev(s, rb=rb, dil=dil):
            blk = s * nb
            return (blk // (dil * rb), (blk // rb) % dil, jnp.maximum(blk % rb - 1, 0), 0)

        in_specs += [pl.BlockSpec((1, 1, tq, D_ATT), cur),
                     pl.BlockSpec((1, 1, tq, D_ATT), cur),
                     pl.BlockSpec((1, 1, ATT_BLOCK, D_ATT), prev),
                     pl.BlockSpec((1, 1, tq, D_ATT), cur),
                     pl.BlockSpec((1, 1, ATT_BLOCK, D_ATT), prev),
                     pl.BlockSpec((N_HEADS * ATT_BLOCK, 2 * ATT_BLOCK), lambda s: (0, 0))]
        args += [q, k, k, v, v, tbs[g]]
        out_specs += [pl.BlockSpec((1, 1, tq, D_ATT), cur), pl.BlockSpec((1, 1, tq, LANES), cur)]
        out_shape += [jax.ShapeDtypeStruct(q.shape, BF16),
                      jax.ShapeDtypeStruct(q.shape[:3] + (LANES,), F32)]
    outs = pl.pallas_call(
        functools.partial(_attn_cache_kernel, nb=nb, rows_blocks=tuple(rows_blocks), cache_kw=cache_kw),
        grid=(db,),
        in_specs=in_specs,
        out_specs=out_specs,
        out_shape=out_shape,
        compiler_params=pltpu.CompilerParams(dimension_semantics=("arbitrary",),
                                             vmem_limit_bytes=VMEM_LIMIT_BYTES),
        name="attn_and_cache",
    )(*args)
    return outs[N_CACHE_OUT:], outs[:N_CACHE_OUT]


def _out_kernel(*refs, tm, dils, cache_kw):
    x_ref, cm_ref, sga_ref = refs[:3]
    o_refs = refs[3:3 + N_GROUPS]
    l_refs = refs[3 + N_GROUPS:3 + 2 * N_GROUPS]
    n_in = 3 + 2 * N_GROUPS + 9
    g1_ref, sh2_ref, sc2_ref, g2_ref, n2_ref, wao_ref, wo_ref, w1_ref, w2_ref = refs[3 + 2 * N_GROUPS:n_in]
    cache_in = refs[n_in:n_in + N_CACHE_IN]
    if cache_kw is not None:
        n_in += N_CACHE_IN
    y_ref = refs[n_in]
    cache_out = refs[n_in + 1:n_in + 1 + N_CACHE_OUT]
    ostg, lstg = refs[-2:]

    def token_order(ref, stg, dil):
        if dil == 1:
            return ref[0, 0].astype(F32)
        n_slab = ref.shape[-1] // LANES
        for r in range(dil):
            blk = ref[0, r].astype(F32)
            for i in range(n_slab):
                stg[i, pl.ds(r, tm // dil, stride=dil), :] = blk[:, i * LANES:(i + 1) * LANES]
        return jnp.concatenate([stg[i] for i in range(n_slab)], axis=1)

    lses = [token_order(l_refs[g], lstg, dils[g]) for g in range(N_GROUPS)]
    mx = jnp.maximum(jnp.maximum(lses[0], lses[1]), lses[2])
    es = [jnp.exp(l - mx) for l in lses]
    inv = 1.0 / (es[0] + es[1] + es[2])
    er = lax.broadcasted_iota(jnp.int32, (2 * LANES, D_ATT), 0) % LANES
    ec = lax.broadcasted_iota(jnp.int32, (2 * LANES, D_ATT), 1) // HEAD_DIM
    expand = jnp.where(er == ec, 1.0, 0.0).astype(BF16)
    o = None
    for g in range(N_GROUPS):
        w = es[g] * inv
        w_hi = w.astype(BF16)
        w_lo = (w - w_hi.astype(F32)).astype(BF16)
        wexp = _mm(jnp.concatenate([w_hi, w_lo], axis=1), expand)
        term = wexp * token_order(o_refs[g], ostg, dils[g])
        o = term if o is None else o + term
    attn_out = _mm(o.astype(BF16), wao_ref[...])
    mixed = cm_ref[0].astype(F32) + sga_ref[0].astype(F32) * attn_out
    x1 = x_ref[0] + _mod_rows(g1_ref, tm) * _mm(mixed.astype(BF16), wo_ref[...])
    ms = jnp.mean(x1 * x1, axis=-1, keepdims=True)
    xn2 = x1 * lax.rsqrt(ms + RMS_EPS) * n2_ref[...]
    xn2 = (xn2 * (1.0 + _mod_rows(sc2_ref, tm)) + _mod_rows(sh2_ref, tm)).astype(BF16)
    acc = None
    n_ff = D_FF // FF_CHUNK
    step = pl.program_id(0) * pl.num_programs(1) + pl.program_id(1)
    for f in range(n_ff):
        if cache_kw is not None and f == n_ff // 2:
            scores = _cache_scores(step, cache_in, cache_out, **cache_kw)
        cols = slice(f * FF_CHUNK, (f + 1) * FF_CHUNK)
        hid = jnp.maximum(_mm(xn2, w1_ref[:, cols]), 0.0)
        part = _mm((hid * hid).astype(BF16), w2_ref[cols, :])
        acc = part if acc is None else acc + part
        if cache_kw is not None and f == n_ff // 2:
            _cache_finish(scores, step, cache_in, cache_out, **cache_kw)
    y_ref[0] = x1 + _mod_rows(g2_ref, tm) * acc


def _out_proj(x, cm, sga, os, lses, gate1, shift2, scale2, gate2, n2, wao, wo, w1, w2, *, tm, cache_args=None):
    nb, ns, d = x.shape
    dils = tuple(o.shape[1] for o in os)
    assert ns % tm == 0 and all(tm % (dil * 2 * SUBLANES) == 0 for dil in dils)
    nmod = gate1.shape[1]
    assert nmod == 1 or (tm * nmod) % ns == 0
    tmod = 1 if nmod == 1 else tm * nmod // ns
    row = lambda b, s: (b, s, 0)
    mod_map = (lambda b, s: (b, 0, 0)) if nmod == 1 else row
    mod_spec = pl.BlockSpec((1, tmod, d), mod_map)
    res_spec = lambda dil, n: pl.BlockSpec((1, dil, tm // dil, n), lambda b, s: (b, 0, s, 0))
    in_specs = ([pl.BlockSpec((1, tm, d), row)] * 3
                + [res_spec(dil, D_ATT) for dil in dils]
                + [res_spec(dil, LANES) for dil in dils]
                + [mod_spec] * 4
                + [_const_spec((1, d)), _const_spec((D_ATT, d)), _const_spec((d, d)),
                   _const_spec((d, D_FF)), _const_spec((D_FF, d))])
    args = [x, cm, sga, *os, *lses, gate1, shift2, scale2, gate2, n2, wao, wo, w1, w2]
    out_specs = [pl.BlockSpec((1, tm, d), row)]
    out_shape = [jax.ShapeDtypeStruct((nb, ns, d), F32)]
    cache_kw = None
    if cache_args is not None:
        nst = ns // tm
        c_in, c_out, c_shape, cache_kw = _cache_specs(cache_args, nb * nst, lambda b, s: b * nst + s)
        in_specs += c_in
        args += list(cache_args)
        out_specs += c_out
        out_shape += c_shape
    outs = pl.pallas_call(
        functools.partial(_out_kernel, tm=tm, dils=dils, cache_kw=cache_kw),
        grid=(nb, ns // tm),
        in_specs=in_specs,
        out_specs=out_specs,
        out_shape=out_shape,
        scratch_shapes=[pltpu.VMEM((D_ATT // LANES, tm, LANES), F32), pltpu.VMEM((1, tm, LANES), F32)],
        compiler_params=pltpu.CompilerParams(dimension_semantics=("arbitrary", "arbitrary"),
                                             vmem_limit_bytes=VMEM_LIMIT_BYTES),
        name="out_proj",
    )(*args)
    return outs[0] if cache_args is None else outs


def _tail_kernel(k_ref, v_ref, ko_ref, vo_ref, stg, *, dil, rows):
    for src, dst in ((k_ref, ko_ref), (v_ref, vo_ref)):
        if dil == 1:
            x = src[0, 0].astype(F32)
        else:
            for r in range(dil):
                blk = src[0, r].astype(F32)
                for i in range(D_ATT // LANES):
                    stg[i, pl.ds(r, rows // dil, stride=dil), :] = blk[:, i * LANES:(i + 1) * LANES]
            x = jnp.concatenate([stg[i] for i in range(D_ATT // LANES)], axis=1)
        dst[0] = x.T


def _state_tail(k, v, keep):
    nbatch, dil, nl, _ = k.shape
    rows = min(keep, TM_OUT)
    assert keep % rows == 0 and (nl * dil - keep) % rows == 0 and rows % (dil * 2 * SUBLANES) == 0
    first = (nl * dil - keep) // rows
    src = pl.BlockSpec((1, dil, rows // dil, D_ATT), lambda b, j: (b, 0, first + j, 0))
    dst = pl.BlockSpec((1, D_ATT, rows), lambda b, j: (b, 0, j))
    return pl.pallas_call(
        functools.partial(_tail_kernel, dil=dil, rows=rows),
        grid=(nbatch, keep // rows),
        in_specs=[src, src],
        out_specs=[dst, dst],
        out_shape=[jax.ShapeDtypeStruct((nbatch, D_ATT, keep), F32)] * 2,
        scratch_shapes=[pltpu.VMEM((D_ATT // LANES, rows, LANES), F32)],
        compiler_params=pltpu.CompilerParams(dimension_semantics=("arbitrary", "arbitrary"),
                                             vmem_limit_bytes=VMEM_LIMIT_BYTES),
        name=f"state_tail_d{dil}",
    )(k, v)


def _t5_causal_bucket(dist):
    max_exact = N_BUCKETS // 2
    ratio = jnp.maximum(dist, 1).astype(F32) / max_exact
    large = max_exact + (jnp.log(ratio) / math.log(MAX_DISTANCE / max_exact)
                         * (N_BUCKETS - max_exact)).astype(jnp.int32)
    large = jnp.minimum(large, N_BUCKETS - 1)
    return jnp.where(dist < max_exact, dist, large)


def _group_bias(rel_bias, g):
    steps = jnp.arange(WK + 1)
    bucket = _t5_causal_bucket(steps * DILS[g])
    onehot = (bucket[:, None] == jnp.arange(N_BUCKETS)[None, :]).astype(F32)
    b = jnp.dot(onehot, rel_bias[:, g * N_HEADS:(g + 1) * N_HEADS].astype(F32), precision=lax.Precision.HIGHEST)
    return b.T


def _prompt_bias_table(bias):
    blk, period = ATT_BLOCK, 4 * ATT_BLOCK
    rev = bias[:, ::-1]
    row = jnp.concatenate([rev, jnp.full((N_HEADS, period - (WK + 1)), NEG, F32)], axis=1)
    tiled = jnp.tile(row, (1, blk))[:, :blk * (period - 1)].reshape(N_HEADS, blk, period - 1)
    return tiled[:, :, :2 * blk].reshape(N_HEADS * blk, 2 * blk)


def _decode_bias_table(bias, width, dil, dec_seq):
    assert width == WK * dil and (dil == 1 or dil >= dec_seq)
    rev = bias[:, ::-1]
    neg_col = jnp.full((N_HEADS, 1), NEG, F32)
    rows = []
    for t in range(dec_seq):
        if dil == 1:
            cache = jnp.pad(rev[:, :width - t], ((0, 0), (t, 0)), constant_values=NEG)
        else:
            mine = (np.arange(dil) == t)[None, None, :]
            cache = jnp.where(mine, rev[:, :WK, None], NEG).reshape(N_HEADS, width)
        new = [bias[:, (t - tn) // dil:(t - tn) // dil + 1] if tn <= t and (t - tn) % dil == 0 else neg_col
               for tn in range(dec_seq)]
        rows.append(jnp.concatenate([cache, jnp.full((N_HEADS, LANES - dec_seq), NEG, F32)] + new, axis=1))
    return jnp.concatenate(rows, axis=0)


def _layer(x_prompt, x_sample, c_prompt, c_sample, state_conv, caches, rel_bias, norm1_g, norm2_g,
           w_ada, b_ada, w_in, conv_w, q_norm_g, k_norm_g, w_conv_out, w_attn_out, w_o, w_mlp_in,
           w_mlp_out):
    nbatch, ns, d = x_prompt.shape
    db, dec_seq, _ = x_sample.shape
    ntok = db * dec_seq

    w_in_b = w_in.astype(BF16)
    wco_b = w_conv_out.astype(BF16)
    wao_b = w_attn_out.astype(BF16)
    wo_b = w_o.astype(BF16)
    w1_b = w_mlp_in.astype(BF16)
    w2_b = w_mlp_out.astype(BF16)
    g1 = norm1_g.reshape(1, d)
    g2n = norm2_g.reshape(1, d)
    qg = jnp.tile(q_norm_g.reshape(1, HEAD_DIM), (1, N_HEADS))
    kg = jnp.tile(k_norm_g.reshape(1, HEAD_DIM), (1, N_HEADS))

    n_c = nbatch + db
    n_pad = -(-n_c // SUBLANES) * SUBLANES
    c_all = jnp.concatenate([c_prompt, c_sample, jnp.zeros((n_pad - n_c, d), F32)], axis=0)
    mod = _ada(c_all, w_ada, b_ada)
    mod_p = mod[:nbatch].reshape(nbatch, 1, N_MOD, d)
    mod_s = mod[nbatch:n_c].reshape(1, db, N_MOD, d)
    mp = [mod_p[:, :, i] for i in range(N_MOD)]
    msn = [mod_s[:, :, i] for i in range(N_MOD)]

    biases = [_group_bias(rel_bias, g) for g in range(N_GROUPS)]

    xs = x_sample.reshape(1, ntok, d)
    zero_row = jnp.zeros((db, 1, D_CONV), F32)
    hist1 = jnp.concatenate([state_conv[:, 1:2]] + [zero_row] * (dec_seq - 1), axis=1).reshape(1, ntok, D_CONV)
    hist2 = jnp.concatenate([state_conv[:, 0:1], state_conv[:, 1:2]] + [zero_row] * (dec_seq - 2),
                            axis=1).reshape(1, ntok, D_CONV)
    cm_s, sga_s, qkv_s, u_s, _ = _in_proj(xs, msn[0], msn[1], g1, w_in_b, conv_w, wco_b, qg, kg, tm=ntok,
                                          dils=(1,) * N_GROUPS, hist1=hist1, hist2=hist2, dec_seq=dec_seq)
    s_conv = u_s.reshape(db, dec_seq, D_CONV)[:, dec_seq - (CONV_WIDTH - 1):]

    def pack_cols(a):
        return a.reshape(ntok // DEC_PACK, DEC_PACK, D_ATT).transpose(0, 2, 1)

    cache_args = []
    for g, (window, dil) in enumerate(DILATED_GROUPS):
        ck, cv = caches[2 * g], caches[2 * g + 1]
        width = ck.shape[1]
        assert width == window
        kt = ck.transpose(0, 2, 3, 1).reshape(db, D_ATT, width)
        vt = cv.transpose(0, 2, 3, 1).reshape(db, D_ATT, width)
        cache_args.append((qkv_s[3 * g].reshape(db, dec_seq, D_ATT), pack_cols(qkv_s[3 * g + 1]),
                           pack_cols(qkv_s[3 * g + 2]), kt, vt,
                           _decode_bias_table(biases[g], width, dil, dec_seq)))
    by_width = sorted(range(N_GROUPS), key=lambda g: DILATED_GROUPS[g][0])
    on_in_proj, on_out_proj, on_attention = by_width[0], by_width[1], by_width[2]
    cache_out = [None] * N_GROUPS

    cm, sga, qkv, utail, cache_out[on_in_proj] = _in_proj(
        x_prompt, mp[0], mp[1], g1, w_in_b, conv_w, wco_b, qg, kg, tm=TM_IN, dils=DILS,
        cache_args=cache_args[on_in_proj])
    p_conv = utail[:, SUBLANES - (CONV_WIDTH - 1):, :]
    p_kv = []
    for g, (window, dil) in enumerate(DILATED_GROUPS):
        keep = min(window, ns)
        for tail in _state_tail(qkv[3 * g + 1], qkv[3 * g + 2], keep):
            p_kv.append(tail.reshape(nbatch, N_HEADS, HEAD_DIM, keep).transpose(0, 3, 1, 2))
    prompt_att, cache_out[on_attention] = _attention_and_cache(
        qkv, [_prompt_bias_table(b) for b in biases], cache_args[on_attention])
    y_prompt, *cache_out[on_out_proj] = _out_proj(
        x_prompt, cm, sga, prompt_att[0::2], prompt_att[1::2], mp[2], mp[3], mp[4], mp[5], g2n,
        wao_b, wo_b, w1_b, w2_b, tm=TM_OUT, cache_args=cache_args[on_out_proj])

    os_s, lses_s, s_kv = [], [], []
    for g, (window, _) in enumerate(DILATED_GROUPS):
        skt, svt, o_s, lse_s = cache_out[g]
        s_kv.append(skt.reshape(db, N_HEADS, HEAD_DIM, window).transpose(0, 3, 1, 2))
        s_kv.append(svt.reshape(db, N_HEADS, HEAD_DIM, window).transpose(0, 3, 1, 2))
        os_s.append(o_s.reshape(1, 1, ntok, D_ATT).astype(BF16))
        lses_s.append(lse_s.reshape(1, 1, ntok, LANES))
    y_sample = _out_proj(xs, cm_s, sga_s, os_s, lses_s, msn[2], msn[3], msn[4], msn[5], g2n,
                         wao_b, wo_b, w1_b, w2_b, tm=ntok).reshape(db, dec_seq, d)
    return y_prompt, y_sample, [p_conv] + p_kv, [s_conv] + s_kv


def kernel(x_prompt, x_sample, c_prompt, c_sample, state_conv, cache_k1, cache_v1, cache_k2, cache_v2,
           cache_k3, cache_v3, rel_bias, norm1_g, norm2_g, w_ada, b_ada, w_in, conv_w, q_norm_g, k_norm_g,
           w_conv_out, w_attn_out, w_o, w_mlp_in, w_mlp_out):
    depth = w_in.shape[0]
    caches = (cache_k1, cache_v1, cache_k2, cache_v2, cache_k3, cache_v3)
    yp, ys = x_prompt, x_sample
    p_states = [[] for _ in range(1 + 2 * N_GROUPS)]
    s_states = [[] for _ in range(1 + 2 * N_GROUPS)]
    for l in range(depth):
        yp, ys, p_new, s_new = _layer(
            yp, ys, c_prompt, c_sample, state_conv[l], [c[l] for c in caches], rel_bias,
            norm1_g[l], norm2_g[l], w_ada[l], b_ada[l], w_in[l], conv_w[l], q_norm_g[l], k_norm_g[l],
            w_conv_out[l], w_attn_out[l], w_o[l], w_mlp_in[l], w_mlp_out[l])
        for lst, a in zip(p_states, p_new):
            lst.append(a)
        for lst, a in zip(s_states, s_new):
            lst.append(a)
    p_out = [jnp.stack(a) for a in p_states]
    s_out = [jnp.stack(a) for a in s_states]
    return (yp, ys, *p_out, *s_out)
```
